```python
import math
import jax, jax.numpy as jnp
from jax import lax
import numpy as np


D_MODEL = 1024
BATCH = 2
SEQ = 8192
DEPTH = 2
DEC_BATCH = 2
DEC_SEQ = 16384
PAST_LEN = 128

GRID_W = 64
N_EVEN = (DEPTH + 1) // 2
N_ODD = DEPTH // 2

MLA_HEADS = 8
MLA_Q_RANK = 256
MLA_KV_RANK = 128
MLA_NOPE = 64
MLA_ROPE = 32
MLA_V = 64
ROPE_THETA = 10000.0
Q_BLOCK = 128

NAT_HEADS = 8
NAT_HEAD_DIM = 64
NAT_W = NAT_HEADS * NAT_HEAD_DIM
NAT_KH_MAX = 8
NAT_KW = 16

EV_IN = MLA_Q_RANK + MLA_KV_RANK + MLA_ROPE + 3 * NAT_W
EV_MIX = MLA_HEADS * MLA_V + NAT_W

CONV_CH = 512
CONV_WIDTH = 31

S5_CH = 512
S5_GROUP = 16
S5_GROUPS = S5_CH // S5_GROUP
S5_STATE = 64

OD_IN = 2 * CONV_CH + S5_CH
OD_MIX = CONV_CH + S5_CH

D_FF = 2816
FFN_RES_SCALE = 0.5
NORM_EPS = 1e-6
NEG_INF = -1e30

kernel_name = 'hybrid_mla_natten_conformer_s5_encoder'


def rms_norm(x, g):
    xf = x.astype(jnp.float32)
    y = xf * lax.rsqrt(jnp.mean(xf * xf, axis=-1, keepdims=True) + NORM_EPS)
    return (y * g.astype(jnp.float32)).astype(x.dtype)


def layer_norm(x, g, b):
    xf = x.astype(jnp.float32)
    mu = jnp.mean(xf, axis=-1, keepdims=True)
    xc = xf - mu
    y = xc * lax.rsqrt(jnp.mean(xc * xc, axis=-1, keepdims=True) + NORM_EPS)
    return (y * g.astype(jnp.float32) + b.astype(jnp.float32)).astype(x.dtype)


def swiglu(x, wg, wu, wd):
    return (jax.nn.silu(x @ wg) * (x @ wu)) @ wd


def rotary(x):
    L = x.shape[1]
    half = x.shape[-1] // 2
    inv = ROPE_THETA ** (-jnp.arange(half, dtype=jnp.float32) / half)
    ang = jnp.arange(L, dtype=jnp.float32)[:, None] * inv[None, :]
    cos = jnp.cos(ang)[None, :, None, :]
    sin = jnp.sin(ang)[None, :, None, :]
    xf = x.astype(jnp.float32)
    x1, x2 = xf[..., :half], xf[..., half:]
    return jnp.concatenate([x1 * cos - x2 * sin, x2 * cos + x1 * sin], axis=-1).astype(x.dtype)


def mla(q_lat, kv_lat, k_rope, q_norm, kv_norm, w_uq, w_ukv):
    B, L, _ = q_lat.shape
    q = (rms_norm(q_lat, q_norm) @ w_uq).reshape(B, L, MLA_HEADS, MLA_NOPE + MLA_ROPE)
    kv = (rms_norm(kv_lat, kv_norm) @ w_ukv).reshape(B, L, MLA_HEADS, MLA_NOPE + MLA_V)
    q = jnp.concatenate([q[..., :MLA_NOPE], rotary(q[..., MLA_NOPE:])], axis=-1)
    k_r = jnp.broadcast_to(rotary(k_rope[:, :, None, :]), (B, L, MLA_HEADS, MLA_ROPE))
    k = jnp.concatenate([kv[..., :MLA_NOPE], k_r], axis=-1)
    v = kv[..., MLA_NOPE:]
    scale = (MLA_NOPE + MLA_ROPE) ** -0.5
    qb = q.reshape(B, L // Q_BLOCK, Q_BLOCK, MLA_HEADS, MLA_NOPE + MLA_ROPE).transpose(1, 0, 2, 3, 4)

    def block(qi):
        s = jnp.einsum('bqhd,bkhd->bhqk', qi, k, preferred_element_type=jnp.float32) * scale
        p = jax.nn.softmax(s, axis=-1).astype(v.dtype)
        return jnp.einsum('bhqk,bkhd->bqhd', p, v)

    o = lax.map(block, qb)
    return o.transpose(1, 0, 2, 3, 4).reshape(B, L, MLA_HEADS * MLA_V)


def neighborhood_attention(q, k, v, rpb):
    B, L, _ = q.shape
    rows = L // GRID_W
    kh = min(NAT_KH_MAX, rows)
    kw = NAT_KW
    shp = (B, rows, GRID_W, NAT_HEADS, NAT_HEAD_DIM)
    q, k, v = q.reshape(shp), k.reshape(shp), v.reshape(shp)
    r = jnp.arange(rows)
    row_idx = jnp.clip(r - kh // 2, 0, rows - kh)[:, None] + jnp.arange(kh)[None, :]
    k_rows = k[:, row_idx]
    v_rows = v[:, row_idx]
    c = jnp.arange(GRID_W)
    col_start = jnp.clip(c - kw // 2, 0, GRID_W - kw)
    col_ok = (c[None, :] >= col_start[:, None]) & (c[None, :] < col_start[:, None] + kw)
    row_off = row_idx - r[:, None] + (NAT_KH_MAX - 1)
    col_off = jnp.clip(c[None, :] - c[:, None], -(kw - 1), kw - 1) + (kw - 1)
    bias = rpb[:, row_off[:, None, :, None], col_off[None, :, None, :]]
    bias = bias.astype(jnp.float32).transpose(1, 0, 2, 3, 4)
    bias = jnp.where(col_ok[:, None, :], bias, NEG_INF)
    s = jnp.einsum('brqhd,brkwhd->brhqkw', q, k_rows, preferred_element_type=jnp.float32)
    s = s * (NAT_HEAD_DIM ** -0.5) + bias[None]
    sh = s.shape
    p = jax.nn.softmax(s.reshape(sh[:4] + (kh * GRID_W,)), axis=-1).reshape(sh).astype(v.dtype)
    o = jnp.einsum('brhqkw,brkwhd->brqhd', p, v_rows)
    return o.reshape(B, L, NAT_W)


def conformer_conv(a, g, dw_w, dw_b, ln_g, ln_b):
    u = a * jax.nn.sigmoid(g)
    y = lax.conv_general_dilated(u, dw_w[:, None, :], window_strides=(1,),
                                 padding=[(CONV_WIDTH // 2, CONV_WIDTH // 2)],
                                 dimension_numbers=('NWC', 'WIO', 'NWC'),
                                 feature_group_count=CONV_CH) + dw_b
    return jax.nn.silu(layer_norm(y, ln_g, ln_b))


def _ssm_combine(e1, e2):
    a1, b1 = e1
    a2, b2 = e2
    return a2 * a1, a2 * b1 + b2


def s5(u, lam_re, lam_im, log_step, b_re, b_im, c_re, c_im, d, w_glu):
    B, L, _ = u.shape
    uf = u.astype(jnp.float32)
    ug = uf.reshape(B, L, S5_GROUPS, S5_GROUP).astype(jnp.complex64)
    y = d.astype(jnp.float32) * uf
    for direction, rev in ((0, False), (1, True)):
        lam = lax.complex(lam_re[direction].astype(jnp.float32), lam_im[direction].astype(jnp.float32))
        dt = jnp.exp(log_step[direction].astype(jnp.float32))[:, None]
        lam_bar = jnp.exp(lam * dt)
        bmat = lax.complex(b_re[direction].astype(jnp.float32), b_im[direction].astype(jnp.float32))
        b_bar = ((lam_bar - 1.0) / lam)[..., None] * bmat
        bu = jnp.einsum('blgc,gpc->blgp', ug, b_bar)
        a = jnp.broadcast_to(lam_bar, bu.shape)
        _, xs = lax.associative_scan(_ssm_combine, (a, bu), axis=1, reverse=rev)
        cmat = lax.complex(c_re[direction].astype(jnp.float32), c_im[direction].astype(jnp.float32))
        y = y + jnp.einsum('blgp,gcp->blgc', xs, cmat).real.reshape(B, L, S5_CH)
    z = jax.nn.gelu(y.astype(u.dtype))
    return z * jax.nn.sigmoid(z @ w_glu)


def setup_inputs(seed: int = 0) -> dict:
    key = jax.random.key(seed)
    ks = iter(jax.random.split(key, 40))

    def nrm(shape, scale):
        return jax.random.normal(next(ks), shape, jnp.float32) * scale

    def gain(shape):
        return 1.0 + 0.05 * jax.random.normal(next(ks), shape, jnp.float32)

    P, G = S5_STATE, S5_GROUPS
    lam_im0 = jnp.pi * jnp.arange(P, dtype=jnp.float32)
    return {
        'x_prompt': jax.random.normal(next(ks), (BATCH, SEQ, D_MODEL), jnp.float32),
        'x_sample': jax.random.normal(next(ks), (DEC_BATCH, DEC_SEQ, D_MODEL), jnp.float32),
        'norm_g': gain((DEPTH, 6, D_MODEL)),
        'ffn_w_gate': nrm((DEPTH, 2, D_MODEL, D_FF), D_MODEL ** -0.5),
        'ffn_w_up': nrm((DEPTH, 2, D_MODEL, D_FF), D_MODEL ** -0.5),
        'ffn_w_down': nrm((DEPTH, 2, D_FF, D_MODEL), D_FF ** -0.5),
        'ev_w_in': nrm((N_EVEN, D_MODEL, EV_IN), D_MODEL ** -0.5),
        'mla_q_norm': gain((N_EVEN, MLA_Q_RANK)),
        'mla_kv_norm': gain((N_EVEN, MLA_KV_RANK)),
        'mla_w_uq': nrm((N_EVEN, MLA_Q_RANK, MLA_HEADS * (MLA_NOPE + MLA_ROPE)), MLA_Q_RANK ** -0.5),
        'mla_w_ukv': nrm((N_EVEN, MLA_KV_RANK, MLA_HEADS * (MLA_NOPE + MLA_V)), MLA_KV_RANK ** -0.5),
        'nat_rpb': nrm((N_EVEN, NAT_HEADS, 2 * NAT_KH_MAX - 1, 2 * NAT_KW - 1), 0.02),
        'ev_w_out': nrm((N_EVEN, EV_MIX, D_MODEL), EV_MIX ** -0.5),
        'od_w_in': nrm((N_ODD, D_MODEL, OD_IN), D_MODEL ** -0.5),
        'conv_dw_w': nrm((N_ODD, CONV_WIDTH, CONV_CH), CONV_WIDTH ** -0.5),
        'conv_dw_b': nrm((N_ODD, CONV_CH), 0.02),
        'conv_ln_g': gain((N_ODD, CONV_CH)),
        'conv_ln_b': nrm((N_ODD, CONV_CH), 0.02),
        's5_lambda_re': -0.5 + nrm((N_ODD, 2, G, P), 0.01),
        's5_lambda_im': lam_im0 + nrm((N_ODD, 2, G, P), 0.01),
        's5_log_step': jax.random.uniform(next(ks), (N_ODD, 2, G), jnp.float32,
                                          minval=math.log(1e-3), maxval=math.log(1e-1)),
        's5_b_re': nrm((N_ODD, 2, G, P, S5_GROUP), (2 * S5_GROUP) ** -0.5),
        's5_b_im': nrm((N_ODD, 2, G, P, S5_GROUP), (2 * S5_GROUP) ** -0.5),
        's5_c_re': nrm((N_ODD, 2, G, S5_GROUP, P), (2 * P) ** -0.5),
        's5_c_im': nrm((N_ODD, 2, G, S5_GROUP, P), (2 * P) ** -0.5),
        's5_d': nrm((N_ODD, S5_CH), 1.0),
        's5_w_glu': nrm((N_ODD, S5_CH, S5_CH), S5_CH ** -0.5),
        'od_w_out': nrm((N_ODD, OD_MIX, D_MODEL), OD_MIX ** -0.5),
    }


def reference(x_prompt, x_sample, norm_g, ffn_w_gate, ffn_w_up, ffn_w_down,
              ev_w_in, mla_q_norm, mla_kv_norm, mla_w_uq, mla_w_ukv, nat_rpb, ev_w_out,
              od_w_in, conv_dw_w, conv_dw_b, conv_ln_g, conv_ln_b,
              s5_lambda_re, s5_lambda_im, s5_log_step, s5_b_re, s5_b_im, s5_c_re, s5_c_im,
              s5_d, s5_w_glu, od_w_out):

    def even_mixer(h, i):
        z = h @ ev_w_in[i]
        cuts = [MLA_Q_RANK, MLA_Q_RANK + MLA_KV_RANK, MLA_Q_RANK + MLA_KV_RANK + MLA_ROPE]
        cuts = cuts + [cuts[-1] + NAT_W, cuts[-1] + 2 * NAT_W]
        q_lat, kv_lat, k_rope, nq, nk, nv = jnp.split(z, cuts, axis=-1)
        a = mla(q_lat, kv_lat, k_rope, mla_q_norm[i], mla_kv_norm[i], mla_w_uq[i], mla_w_ukv[i])
        b = neighborhood_attention(nq, nk, nv, nat_rpb[i])
        return jnp.concatenate([a, b], axis=-1) @ ev_w_out[i]

    def odd_mixer(h, i):
        z = h @ od_w_in[i]
        ca, cg, su = jnp.split(z, [CONV_CH, 2 * CONV_CH], axis=-1)
        c = conformer_conv(ca, cg, conv_dw_w[i], conv_dw_b[i], conv_ln_g[i], conv_ln_b[i])
        s = s5(su, s5_lambda_re[i], s5_lambda_im[i], s5_log_step[i], s5_b_re[i], s5_b_im[i],
               s5_c_re[i], s5_c_im[i], s5_d[i], s5_w_glu[i])
        return jnp.concatenate([c, s], axis=-1) @ od_w_out[i]

    def trunk(x):
        h = x
        for layer in range(DEPTH):
            g = norm_g[layer]
            f = swiglu(rms_norm(h, g[0]), ffn_w_gate[layer, 0], ffn_w_up[layer, 0], ffn_w_down[layer, 0])
            h = h + FFN_RES_SCALE * rms_norm(f, g[1])
            m = rms_norm(h, g[2])
            if layer % 2 == 0:
                m = even_mixer(m, layer // 2)
            else:
                m = odd_mixer(m, layer // 2)
            h = h + rms_norm(m, g[3])
            f = swiglu(rms_norm(h, g[4]), ffn_w_gate[layer, 1], ffn_w_up[layer, 1], ffn_w_down[layer, 1])
            h = h + FFN_RES_SCALE * rms_norm(f, g[5])
        return h

    y_prompt = trunk(x_prompt)
    y_sample = trunk(x_sample)
    return (y_prompt, y_sample)
```

```python
import functools
import math

import jax
import jax.numpy as jnp
from jax import lax
from jax.experimental import pallas as pl
from jax.experimental.pallas import tpu as pltpu

F32 = jnp.float32
BF16 = jnp.bfloat16

D_MODEL = 1024
D_FF = 2816
GRID_W = 64

MLA_HEADS = 8
MLA_Q_RANK = 256
MLA_KV_RANK = 128
MLA_NOPE = 64
MLA_ROPE = 32
MLA_V = 64
ROPE_THETA = 10000.0
HEAD_PAD = 128

NAT_HEADS = 8
NAT_HEAD_DIM = 64
NAT_W = NAT_HEADS * NAT_HEAD_DIM
NAT_KH_MAX = 8
NAT_KW = 16

CONV_CH = 512
CONV_WIDTH = 31
CONV_HALO = 16

S5_CH = 512
S5_GROUP = 16
S5_GROUPS = S5_CH // S5_GROUP
S5_STATE = 64
S5_CHUNK = 64

FFN_RES_SCALE = 0.5
NORM_EPS = 1e-6
NEG_INF = -1e30

VMEM_LIMIT = 48 * 1024 * 1024


def _cparams(*sem):
    return pltpu.CompilerParams(dimension_semantics=sem, vmem_limit_bytes=VMEM_LIMIT)


def _rms(x, g):
    return x * lax.rsqrt(jnp.mean(x * x, axis=-1, keepdims=True) + NORM_EPS) * g


def _dot(a, b):
    return jnp.dot(a, b, preferred_element_type=F32)


def _full(shape):
    n = len(shape)
    return pl.BlockSpec(shape, lambda *_: (0,) * n)


def _ffn_kernel(x_ref, gpre_ref, wg_ref, wu_ref, wd_ref, gpost_ref, o_ref, xn_scr, acc_scr, *, nf):
    j = pl.program_id(1)

    @pl.when(j == 0)
    def _():
        xn_scr[...] = _rms(x_ref[...], gpre_ref[...]).astype(BF16)
        acc_scr[...] = jnp.zeros_like(acc_scr)

    xn = xn_scr[...]
    gate = _dot(xn, wg_ref[...])
    up = _dot(xn, wu_ref[...])
    hdn = (gate * jax.nn.sigmoid(gate) * up).astype(BF16)
    acc_scr[...] += _dot(hdn, wd_ref[...])

    @pl.when(j == nf - 1)
    def _():
        o_ref[...] = x_ref[...] + FFN_RES_SCALE * _rms(acc_scr[...], gpost_ref[...])


def _ffn(h, gpre, wg, wu, wd, gpost, *, tm=512, nf=2):
    m = h.shape[0]
    tf = D_FF // nf
    return pl.pallas_call(
        functools.partial(_ffn_kernel, nf=nf),
        grid=(m // tm, nf),
        in_specs=[
            pl.BlockSpec((tm, D_MODEL), lambda i, j: (i, 0)),
            pl.BlockSpec((1, D_MODEL), lambda i, j: (0, 0)),
            pl.BlockSpec((D_MODEL, tf), lambda i, j: (0, j)),
            pl.BlockSpec((D_MODEL, tf), lambda i, j: (0, j)),
            pl.BlockSpec((tf, D_MODEL), lambda i, j: (j, 0)),
            pl.BlockSpec((1, D_MODEL), lambda i, j: (0, 0)),
        ],
        out_specs=pl.BlockSpec((tm, D_MODEL), lambda i, j: (i, 0)),
        out_shape=jax.ShapeDtypeStruct((m, D_MODEL), F32),
        scratch_shapes=[pltpu.VMEM((tm, D_MODEL), BF16), pltpu.VMEM((tm, D_MODEL), F32)],
        compiler_params=_cparams("parallel", "arbitrary"),
        name="ffn",
    )(h, gpre, wg, wu, wd, gpost)


EV_Z = MLA_Q_RANK + MLA_KV_RANK + 2 * HEAD_PAD + 3 * NAT_W
HP_ALL = MLA_HEADS * HEAD_PAD


def _ev_in_kernel(h_ref, g_ref, win_ref, qn_ref, kvn_ref, wuq_ref, wuqr_ref, wuk_ref, wuv_ref,
                  tab_ref, vone_ref, q_ref, k_ref, v_ref, nq_ref, nk_ref, nv_ref):
    m = _rms(h_ref[...], g_ref[...]).astype(BF16)
    z = _dot(m, win_ref[...])
    c0 = MLA_Q_RANK
    c1 = c0 + MLA_KV_RANK
    c2 = c1 + HEAD_PAD
    c3 = c2 + HEAD_PAD
    q_lat = z[:, :c0]
    kv_lat = z[:, c0:c1]
    kr = z[:, c1:c2]
    kr_rot = z[:, c2:c3]
    nq_ref[...] = (z[:, c3:c3 + NAT_W] * (NAT_HEAD_DIM ** -0.5)).astype(BF16)
    nk_ref[...] = z[:, c3 + NAT_W:c3 + 2 * NAT_W].astype(BF16)
    nv_ref[...] = z[:, c3 + 2 * NAT_W:c3 + 3 * NAT_W].astype(BF16)

    qn = _rms(q_lat, qn_ref[...]).astype(BF16)
    kvn = _rms(kv_lat, kvn_ref[...]).astype(BF16)
    q_raw = _dot(qn, wuq_ref[...])
    q_rot = _dot(qn, wuqr_ref[...])
    k_nope = _dot(kvn, wuk_ref[...])
    v = _dot(kvn, wuv_ref[...]) + vone_ref[...]
    v_ref[...] = v.astype(BF16)

    tab = tab_ref[...]
    cq = tab[:, 0:HEAD_PAD]
    sq = tab[:, HEAD_PAD:2 * HEAD_PAD]
    ck = tab[:, 2 * HEAD_PAD:3 * HEAD_PAD]
    sk = tab[:, 3 * HEAD_PAD:4 * HEAD_PAD]
    k_rope = kr * ck + kr_rot * sk
    for hd in range(MLA_HEADS):
        sl = slice(hd * HEAD_PAD, (hd + 1) * HEAD_PAD)
        q_ref[:, sl] = (q_raw[:, sl] * cq + q_rot[:, sl] * sq).astype(BF16)
        k_ref[:, sl] = (k_nope[:, sl] + k_rope).astype(BF16)


def _ev_in(h, g, p, seq_len, *, tm=512):
    m = h.shape[0]
    nblk = seq_len // tm
    tok = lambda w: pl.BlockSpec((tm, w), lambda i: (i, 0))
    outs = [jax.ShapeDtypeStruct((m, HP_ALL), BF16)] * 3 + [jax.ShapeDtypeStruct((m, NAT_W), BF16)] * 3
    return pl.pallas_call(
        _ev_in_kernel,
        grid=(m // tm,),
        in_specs=[
            tok(D_MODEL), _full((1, D_MODEL)), _full((D_MODEL, EV_Z)),
            _full((1, MLA_Q_RANK)), _full((1, MLA_KV_RANK)),
            _full((MLA_Q_RANK, HP_ALL)), _full((MLA_Q_RANK, HP_ALL)),
            _full((MLA_KV_RANK, HP_ALL)), _full((MLA_KV_RANK, HP_ALL)),
            pl.BlockSpec((tm, 4 * HEAD_PAD), lambda i: (i % nblk, 0)),
            _full((1, HP_ALL)),
        ],
        out_specs=[tok(HP_ALL)] * 3 + [tok(NAT_W)] * 3,
        out_shape=outs,
        compiler_params=_cparams("parallel"),
        name="ev_in",
    )(h, g, p["w_in"], p["q_norm"], p["kv_norm"], p["w_uq"], p["w_uq_rot"], p["w_uk"], p["w_uv"],
      p["rope_tab"][seq_len], p["v_one"])


def _mla_kernel(q_ref, k_ref, v_ref, o_ref, *, tk, nk):
    q = q_ref[...]
    tq = q.shape[0]

    def body(j, carry):
        m_prev, acc = carry
        off = pl.multiple_of(j * tk, tk)
        k = k_ref[pl.ds(off, tk), :]
        v = v_ref[pl.ds(off, tk), :]
        s = lax.dot_general(q, k, (((1,), (1,)), ((), ())), preferred_element_type=F32)
        m_new = jnp.maximum(m_prev, jnp.max(s, axis=1, keepdims=True))
        alpha = jnp.exp(m_prev - m_new)
        p = jnp.exp(s - m_new).astype(BF16)
        acc = alpha * acc + _dot(p, v)
        return m_new, acc

    m0 = jnp.full((tq, 1), jnp.finfo(F32).min, F32)
    acc0 = jnp.zeros((tq, HEAD_PAD), F32)
    _, acc = lax.fori_loop(0, nk, body, (m0, acc0))
    o_ref[...] = (acc / acc[:, MLA_V:MLA_V + 1]).astype(BF16)


def _mla(q, k, v, batch, seq_len, *, tq=256, tk=512):
    m = q.shape[0]
    nq = seq_len // tq
    return pl.pallas_call(
        functools.partial(_mla_kernel, tk=tk, nk=seq_len // tk),
        grid=(batch, MLA_HEADS, nq),
        in_specs=[
            pl.BlockSpec((tq, HEAD_PAD), lambda b, h, i: (b * nq + i, h)),
            pl.BlockSpec((seq_len, HEAD_PAD), lambda b, h, i: (b, h)),
            pl.BlockSpec((seq_len, HEAD_PAD), lambda b, h, i: (b, h)),
        ],
        out_specs=pl.BlockSpec((tq, HEAD_PAD), lambda b, h, i: (b * nq + i, h)),
        out_shape=jax.ShapeDtypeStruct((m, HP_ALL), BF16),
        compiler_params=_cparams("parallel", "parallel", "arbitrary"),
        name="mla",
    )(q, k, v)


NAT_LANES = 2 * NAT_HEAD_DIM
NAT_WIN = NAT_KH_MAX * GRID_W


def _nat_kernel(q_ref, k_ref, v_ref, bias_ref, o_ref, *, rows, kh, rblk):
    i = pl.program_id(2)

    def row_body(rr, carry):
        r = i * rblk + rr
        start = jnp.clip(r - kh // 2, 0, rows - kh)
        delta = r - start
        koff = pl.multiple_of(start * GRID_W, GRID_W)
        qoff = pl.multiple_of(rr * GRID_W, GRID_W)
        qrow = q_ref[pl.ds(qoff, GRID_W), :]
        kwin = k_ref[pl.ds(koff, kh * GRID_W), :]
        vwin = v_ref[pl.ds(koff, kh * GRID_W), :]
        outs = []
        for hh in range(2):
            sl = slice(hh * NAT_HEAD_DIM, (hh + 1) * NAT_HEAD_DIM)
            s = lax.dot_general(qrow[:, sl], kwin[:, sl], (((1,), (1,)), ((), ())),
                                preferred_element_type=F32)
            s = s + bias_ref[delta, hh]
            mx = jnp.max(s, axis=1, keepdims=True)
            e = jnp.exp(s - mx)
            den = jnp.sum(e, axis=1, keepdims=True)
            pr = (e / den).astype(BF16)
            outs.append(_dot(pr, vwin[:, sl]))
        o_ref[pl.ds(qoff, GRID_W), :] = jnp.concatenate(outs, axis=1).astype(BF16)
        return carry

    lax.fori_loop(0, rblk, row_body, 0)


def _nat(nq, nk, nv, bias, batch, seq_len, *, rblk=8):
    m = nq.shape[0]
    rows = seq_len // GRID_W
    kh = min(NAT_KH_MAX, rows)
    nblk = rows // rblk
    tq = rblk * GRID_W
    return pl.pallas_call(
        functools.partial(_nat_kernel, rows=rows, kh=kh, rblk=rblk),
        grid=(batch, NAT_HEADS // 2, nblk),
        in_specs=[
            pl.BlockSpec((tq, NAT_LANES), lambda b, hp, i: (b * nblk + i, hp)),
            pl.BlockSpec((seq_len, NAT_LANES), lambda b, hp, i: (b, hp)),
            pl.BlockSpec((seq_len, NAT_LANES), lambda b, hp, i: (b, hp)),
            pl.BlockSpec((kh, 2, GRID_W, kh * GRID_W), lambda b, hp, i: (0, hp, 0, 0)),
        ],
        out_specs=pl.BlockSpec((tq, NAT_LANES), lambda b, hp, i: (b * nblk + i, hp)),
        out_shape=jax.ShapeDtypeStruct((m, NAT_W), BF16),
        compiler_params=_cparams("parallel", "parallel", "arbitrary"),
        name="nat",
    )(nq, nk, nv, bias)


def _mix_out_kernel(h_ref, a_ref, b_ref, wa_ref, wb_ref, g_ref, o_ref):
    mix = _dot(a_ref[...], wa_ref[...]) + _dot(b_ref[...], wb_ref[...])
    o_ref[...] = h_ref[...] + _rms(mix, g_ref[...])


def _ev_out(h, a, b, wa, wb, g, *, tm=512):
    m = h.shape[0]
    tok = lambda w: pl.BlockSpec((tm, w), lambda i: (i, 0))
    return pl.pallas_call(
        _mix_out_kernel,
        grid=(m // tm,),
        in_specs=[tok(D_MODEL), tok(a.shape[1]), tok(b.shape[1]),
                  _full(wa.shape), _full(wb.shape), _full((1, D_MODEL))],
        out_specs=tok(D_MODEL),
        out_shape=jax.ShapeDtypeStruct((m, D_MODEL), F32),
        compiler_params=_cparams("parallel"),
        name="ev_out",
    )(h, a, b, wa, wb, g)


def _od_in_kernel(h_ref, g_ref, win_ref, u_ref, su_ref):
    m = _rms(h_ref[...], g_ref[...]).astype(BF16)
    z = _dot(m, win_ref[...])
    ca = z[:, :CONV_CH]
    cg = z[:, CONV_CH:2 * CONV_CH]
    u_ref[...] = ca * jax.nn.sigmoid(cg)
    su_ref[...] = z[:, 2 * CONV_CH:]


def _od_in(h, g, w_in, *, tm=512):
    m = h.shape[0]
    tok = lambda w: pl.BlockSpec((tm, w), lambda i: (i, 0))
    return pl.pallas_call(
        _od_in_kernel,
        grid=(m // tm,),
        in_specs=[tok(D_MODEL), _full((1, D_MODEL)), _full(w_in.shape)],
        out_specs=[tok(CONV_CH), tok(S5_CH)],
        out_shape=[jax.ShapeDtypeStruct((m, CONV_CH), F32), jax.ShapeDtypeStruct((m, S5_CH), F32)],
        compiler_params=_cparams("parallel"),
        name="od_in",
    )(h, g, w_in)


CONV_SUB = 32


def _conv_kernel(prev_ref, cur_ref, next_ref, w_ref, b_ref, lg_ref, lb_ref, o_ref, scr, *, tm, nblk):
    i = pl.program_id(0)
    first = (i % nblk) == 0
    last = (i % nblk) == nblk - 1
    scr[0:CONV_HALO, :] = jnp.where(first, 0.0, prev_ref[...])
    scr[CONV_HALO:CONV_HALO + tm, :] = cur_ref[...]
    scr[CONV_HALO + tm:, :] = jnp.where(last, 0.0, next_ref[...])
    w = w_ref[...]
    shift = CONV_HALO - CONV_WIDTH // 2

    def sub(c, carry):
        base = pl.multiple_of(c * CONV_SUB, CONV_SUB)
        win = scr[pl.ds(base, CONV_SUB + 2 * CONV_HALO), :]
        acc = jnp.zeros((CONV_SUB, CONV_CH), F32)
        for kk in range(CONV_WIDTH):
            acc = acc + win[shift + kk:shift + kk + CONV_SUB, :] * w[kk:kk + 1, :]
        y = acc + b_ref[...]
        mu = jnp.mean(y, axis=-1, keepdims=True)
        yc = y - mu
        yn = yc * lax.rsqrt(jnp.mean(yc * yc, axis=-1, keepdims=True) + NORM_EPS)
        yn = yn * lg_ref[...] + lb_ref[...]
        o_ref[pl.ds(base, CONV_SUB), :] = (yn * jax.nn.sigmoid(yn)).astype(BF16)
        return carry

    lax.fori_loop(0, tm // CONV_SUB, sub, 0)


def _conv(u, w, b, lg, lb, seq_len, *, tm=512):
    m = u.shape[0]
    nblk = seq_len // tm
    hb = tm // CONV_HALO
    nh = m // CONV_HALO
    return pl.pallas_call(
        functools.partial(_conv_kernel, tm=tm, nblk=nblk),
        grid=(m // tm,),
        in_specs=[
            pl.BlockSpec((CONV_HALO, CONV_CH), lambda i: (jnp.maximum(i * hb - 1, 0), 0)),
            pl.BlockSpec((tm, CONV_CH), lambda i: (i, 0)),
            pl.BlockSpec((CONV_HALO, CONV_CH), lambda i: (jnp.minimum((i + 1) * hb, nh - 1), 0)),
            _full((CONV_WIDTH, CONV_CH)), _full((1, CONV_CH)), _full((1, CONV_CH)), _full((1, CONV_CH)),
        ],
        out_specs=pl.BlockSpec((tm, CONV_CH), lambda i: (i, 0)),
        out_shape=jax.ShapeDtypeStruct((m, CONV_CH), BF16),
        scratch_shapes=[pltpu.VMEM((tm + 2 * CONV_HALO, CONV_CH), F32)],
        compiler_params=_cparams("parallel"),
        name="conv",
    )(u, u, u, w, b, lg, lb)


S5_W = S5_GROUP * S5_CHUNK
S5_X = 2 * S5_STATE


def _s5_kernel(u_ref, m_ref, bmat_ref, cmat_ref, pw_ref, y_ref, *, nchunk, nseq):
    u = u_ref[0]
    nc = nseq * nchunk
    s = _dot(u, bmat_ref[0])
    row = lax.broadcasted_iota(jnp.int32, (nc, S5_X), 0) & (nchunk - 1)
    nstep = nchunk.bit_length() - 1

    def cmul(k, d, x):
        a1 = pw_ref[0, d, k, 0:1, :]
        a2 = pw_ref[0, d, k, 1:2, :]
        return a1 * x + a2 * pltpu.roll(x, S5_STATE, 1)

    xf = s[:, :S5_X]
    xr = s[:, S5_X:]
    for k in range(nstep):
        sh = 1 << k
        xf = xf + cmul(k, 0, jnp.where(row >= sh, pltpu.roll(xf, sh, 0), 0.0))
        xr = xr + cmul(k, 1, jnp.where(row < nchunk - sh, pltpu.roll(xr, nc - sh, 0), 0.0))
    x_prev = jnp.where(row >= 1, pltpu.roll(xf, 1, 0), 0.0)
    x_next = jnp.where(row < nchunk - 1, pltpu.roll(xr, nc - 1, 0), 0.0)
    x = jnp.concatenate([x_prev, x_next], axis=1).astype(BF16)
    y_ref[0] = _dot(u, m_ref[0]) + _dot(x, cmat_ref[0])


def _s5_scan(u_g, ops, nseq, nchunk):
    g, nc, w = u_g.shape
    assert nchunk & (nchunk - 1) == 0 and nc == nseq * nchunk
    nstep = max(nchunk.bit_length() - 1, 1)
    grp = lambda *shape: pl.BlockSpec((1,) + shape, lambda i: (i,) + (0,) * len(shape))
    return pl.pallas_call(
        functools.partial(_s5_kernel, nchunk=nchunk, nseq=nseq),
        grid=(g,),
        in_specs=[grp(nc, w), grp(w, w), grp(w, 2 * S5_X), grp(2 * S5_X, w),
                  grp(2, nstep, 2, S5_X)],
        out_specs=grp(nc, w),
        out_shape=jax.ShapeDtypeStruct((g, nc, w), F32),
        compiler_params=_cparams("parallel"),
        name="s5",
    )(u_g, ops["m"], ops["b"], ops["c"], ops["pw"][nchunk])


def _od_out_kernel(h_ref, c_ref, su_ref, ys_ref, d_ref, wglu_ref, wc_ref, ws_ref, g_ref, o_ref):
    y = d_ref[...] * su_ref[...] + ys_ref[...]
    z = jax.nn.gelu(y, approximate=True)
    sg = (z * jax.nn.sigmoid(_dot(z.astype(BF16), wglu_ref[...]))).astype(BF16)
    mix = _dot(c_ref[...], wc_ref[...]) + _dot(sg, ws_ref[...])
    o_ref[...] = h_ref[...] + _rms(mix, g_ref[...])


def _od_out(h, c, su, ys, d, wglu, wc, ws, g, *, tm=512):
    m = h.shape[0]
    tok = lambda w: pl.BlockSpec((tm, w), lambda i: (i, 0))
    return pl.pallas_call(
        _od_out_kernel,
        grid=(m // tm,),
        in_specs=[tok(D_MODEL), tok(CONV_CH), tok(S5_CH), tok(S5_CH), _full((1, S5_CH)),
                  _full(wglu.shape), _full(wc.shape), _full(ws.shape), _full((1, D_MODEL))],
        out_specs=tok(D_MODEL),
        out_shape=jax.ShapeDtypeStruct((m, D_MODEL), F32),
        compiler_params=_cparams("parallel"),
        name="od_out",
    )(h, c, su, ys, d, wglu, wc, ws, g)


def _head_pad_cols(w, widths, offsets):
    k = w.shape[0]
    per = w.shape[1] // MLA_HEADS
    w = w.reshape(k, MLA_HEADS, per)
    out = jnp.zeros((k, MLA_HEADS, HEAD_PAD), w.dtype)
    src = 0
    for wd, off in zip(widths, offsets):
        if off is not None:
            out = out.at[:, :, off:off + wd].set(w[:, :, src:src + wd])
        src += wd
    return out.reshape(k, HP_ALL)


def _rot_cols(w):
    half = w.shape[-1] // 2
    return jnp.concatenate([-w[..., half:], w[..., :half]], axis=-1)


def _rope_table(seq_len):
    half = MLA_ROPE // 2
    inv = ROPE_THETA ** (-jnp.arange(half, dtype=F32) / half)
    ang = jnp.arange(seq_len, dtype=F32)[:, None] * inv[None, :]
    cos = jnp.concatenate([jnp.cos(ang)] * 2, axis=1)
    sin = jnp.concatenate([jnp.sin(ang)] * 2, axis=1)
    scale = (MLA_NOPE + MLA_ROPE) ** -0.5
    z64 = jnp.zeros((seq_len, MLA_NOPE), F32)
    z32 = jnp.zeros((seq_len, HEAD_PAD - MLA_NOPE - MLA_ROPE), F32)
    cq = jnp.concatenate([z64 + scale, cos * scale, z32], axis=1)
    sq = jnp.concatenate([z64, sin * scale, z32], axis=1)
    ck = jnp.concatenate([z64, cos, z32], axis=1)
    sk = jnp.concatenate([z64, sin, z32], axis=1)
    return jnp.concatenate([cq, sq, ck, sk], axis=1)


def _nat_bias(rpb, kh):
    c = jnp.arange(GRID_W)
    col_start = jnp.clip(c - NAT_KW // 2, 0, GRID_W - NAT_KW)
    col_ok = (c[None, :] >= col_start[:, None]) & (c[None, :] < col_start[:, None] + NAT_KW)
    col_off = jnp.clip(c[None, :] - c[:, None], -(NAT_KW - 1), NAT_KW - 1) + (NAT_KW - 1)
    delta = jnp.arange(kh)
    row_off = jnp.arange(kh)[None, :] - delta[:, None] + (NAT_KH_MAX - 1)
    bias = rpb[:, row_off[:, None, :, None], col_off[None, :, None, :]]
    bias = jnp.where(col_ok[None, None, :, None, :], bias.astype(F32), NEG_INF)
    bias = bias.transpose(1, 0, 2, 3, 4)
    return bias.reshape(kh, NAT_HEADS, GRID_W, kh * GRID_W)


def _s5_operators(lam_re, lam_im, log_step, b_re, b_im, c_re, c_im, nchunks):
    t = S5_CHUNK
    dt = jnp.exp(log_step)[:, :, None]
    ar, ai = lam_re * dt, lam_im * dt
    er = jnp.exp(ar)
    lbr, lbi = er * jnp.cos(ai), er * jnp.sin(ai)
    den = lam_re * lam_re + lam_im * lam_im
    fr = ((lbr - 1.0) * lam_re + lbi * lam_im) / den
    fi = (lbi * lam_re - (lbr - 1.0) * lam_im) / den
    bbr = fr[..., None] * b_re - fi[..., None] * b_im
    bbi = fr[..., None] * b_im + fi[..., None] * b_re

    def power(d):
        d = d.astype(F32)[None, None, :, None]
        mag = jnp.exp(ar[:, :, None, :] * d)
        return mag * jnp.cos(ai[:, :, None, :] * d), mag * jnp.sin(ai[:, :, None, :] * d)

    hi = lax.Precision.HIGHEST
    pr, pi = power(jnp.arange(t + 1))
    wr = c_re[:, :, None] * pr[:, :, :, None, :] - c_im[:, :, None] * pi[:, :, :, None, :]
    wi = c_re[:, :, None] * pi[:, :, :, None, :] + c_im[:, :, None] * pr[:, :, :, None, :]
    kmat = (jnp.einsum("xgdcp,xgpk->xgdck", wr[:, :, :t], bbr, precision=hi)
            - jnp.einsum("xgdcp,xgpk->xgdck", wi[:, :, :t], bbi, precision=hi))
    kf, kr = kmat[0], kmat[1]
    zero = jnp.zeros_like(kf[:, :1])
    kcat = jnp.concatenate([kr[:, :0:-1], kf[:, :1] + kr[:, :1], kf[:, 1:]], axis=1)
    del zero
    idx = jnp.arange(t)[None, :] - jnp.arange(t)[:, None] + (t - 1)
    mm = kcat[:, idx]
    mm = mm.transpose(0, 1, 4, 2, 3).reshape(S5_GROUPS, S5_W, S5_W)

    def contrib(pr_, pi_, br_, bi_):
        re = pr_[:, :, None, :] * br_.transpose(0, 2, 1)[:, None] - pi_[:, :, None, :] * bi_.transpose(0, 2, 1)[:, None]
        im = pr_[:, :, None, :] * bi_.transpose(0, 2, 1)[:, None] + pi_[:, :, None, :] * br_.transpose(0, 2, 1)[:, None]
        return jnp.concatenate([re, im], axis=-1).reshape(S5_GROUPS, S5_W, S5_X)

    bf = contrib(pr[0, :, t - 1::-1][:, :t], pi[0, :, t - 1::-1][:, :t], bbr[0], bbi[0])
    br = contrib(pr[1, :, :t], pi[1, :, :t], bbr[1], bbi[1])
    bmat = jnp.concatenate([bf, br], axis=-1)

    def readout(wr_, wi_):
        m_ = jnp.concatenate([wr_, -wi_], axis=-1)
        return m_.transpose(0, 3, 1, 2).reshape(S5_GROUPS, S5_X, S5_W)

    cf = readout(wr[0, :, 1:t + 1], wi[0, :, 1:t + 1])
    cr = readout(wr[1, :, t:0:-1], wi[1, :, t:0:-1])
    cmat = jnp.concatenate([cf, cr], axis=1)

    pws = {}
    for nchunk in nchunks:
        nstep = max(nchunk.bit_length() - 1, 1)
        qr, qi = power(t * (2 ** jnp.arange(nstep)))
        a1 = jnp.concatenate([qr, qr], axis=-1)
        a2 = jnp.concatenate([-qi, qi], axis=-1)
        pws[nchunk] = jnp.stack([a1, a2], axis=3).transpose(1, 0, 2, 3, 4)
    return {"m": mm.astype(BF16), "b": bmat.astype(BF16), "c": cmat.astype(BF16), "pw": pws}


def _even_params(ev_w_in, q_norm, kv_norm, w_uq, w_ukv, rpb, ev_w_out, seq_lens):
    c0 = MLA_Q_RANK
    c1 = c0 + MLA_KV_RANK
    c2 = c1 + MLA_ROPE
    w_kr = ev_w_in[:, c1:c2]
    pad_l = jnp.zeros((D_MODEL, MLA_NOPE), F32)
    pad_r = jnp.zeros((D_MODEL, HEAD_PAD - MLA_NOPE - MLA_ROPE), F32)
    w_in = jnp.concatenate([
        ev_w_in[:, :c1],
        pad_l, w_kr, pad_r,
        pad_l, _rot_cols(w_kr), pad_r,
        ev_w_in[:, c2:],
    ], axis=1).astype(BF16)
    uq = w_uq.reshape(MLA_Q_RANK, MLA_HEADS, MLA_NOPE + MLA_ROPE)
    uq_rot = jnp.concatenate([jnp.zeros_like(uq[..., :MLA_NOPE]), _rot_cols(uq[..., MLA_NOPE:])], axis=-1)
    nope_rope = ((MLA_NOPE, MLA_ROPE), (0, MLA_NOPE))
    v_one = jnp.zeros((MLA_HEADS, HEAD_PAD), F32).at[:, MLA_V].set(1.0).reshape(1, HP_ALL)
    wa = jnp.zeros((MLA_HEADS, HEAD_PAD, D_MODEL), F32)
    wa = wa.at[:, :MLA_V].set(ev_w_out[:MLA_HEADS * MLA_V].reshape(MLA_HEADS, MLA_V, D_MODEL))
    return {
        "w_in": w_in,
        "q_norm": q_norm[None], "kv_norm": kv_norm[None],
        "w_uq": _head_pad_cols(w_uq, *nope_rope).astype(BF16),
        "w_uq_rot": _head_pad_cols(uq_rot.reshape(MLA_Q_RANK, -1), *nope_rope).astype(BF16),
        "w_uk": _head_pad_cols(w_ukv, (MLA_NOPE, MLA_V), (0, None)).astype(BF16),
        "w_uv": _head_pad_cols(w_ukv, (MLA_NOPE, MLA_V), (None, 0)).astype(BF16),
        "v_one": v_one,
        "rope_tab": {sl: _rope_table(sl) for sl in seq_lens},
        "nat_bias": {sl: _nat_bias(rpb, min(NAT_KH_MAX, sl // GRID_W)) for sl in seq_lens},
        "wa": wa.reshape(HP_ALL, D_MODEL).astype(BF16),
        "wb": ev_w_out[MLA_HEADS * MLA_V:].astype(BF16),
    }


def _even_mixer(h, g_pre, g_post, p, batch, seq_len):
    q, k, v, nq, nk, nv = _ev_in(h, g_pre, p, seq_len)
    a = _mla(q, k, v, batch, seq_len)
    b = _nat(nq, nk, nv, p["nat_bias"][seq_len], batch, seq_len)
    return _ev_out(h, a, b, p["wa"], p["wb"], g_post)


def _odd_mixer(h, g_pre, g_post, p, batch, seq_len):
    u, su = _od_in(h, g_pre, p["w_in"])
    c = _conv(u, p["dw_w"], p["dw_b"], p["ln_g"], p["ln_b"], seq_len)
    nchunk = seq_len // S5_CHUNK
    u_g = su.astype(BF16).reshape(batch * nchunk, S5_CHUNK, S5_GROUPS, S5_GROUP)
    u_g = u_g.transpose(2, 0, 1, 3).reshape(S5_GROUPS, batch * nchunk, S5_W)
    y_g = _s5_scan(u_g, p["s5"], batch, nchunk)
    ys = y_g.reshape(S5_GROUPS, batch * nchunk, S5_CHUNK, S5_GROUP).transpose(1, 2, 0, 3)
    ys = ys.reshape(batch * seq_len, S5_CH)
    return _od_out(h, c, su, ys, p["d"], p["w_glu"], p["wc"], p["ws"], g_post)


def kernel(x_prompt, x_sample, norm_g, ffn_w_gate, ffn_w_up, ffn_w_down, ev_w_in, mla_q_norm, mla_kv_norm, mla_w_uq, mla_w_ukv, nat_rpb, ev_w_out, od_w_in, conv_dw_w, conv_dw_b, conv_ln_g, conv_ln_b, s5_lambda_re, s5_lambda_im, s5_log_step, s5_b_re, s5_b_im, s5_c_re, s5_c_im, s5_d, s5_w_glu, od_w_out):
    depth = norm_g.shape[0]
    seq_lens = sorted({x_prompt.shape[1], x_sample.shape[1]})
    wg = ffn_w_gate.astype(BF16)
    wu = ffn_w_up.astype(BF16)
    wd = ffn_w_down.astype(BF16)
    mixers = []
    for layer in range(depth):
        i = layer // 2
        if layer % 2 == 0:
            mixers.append(_even_params(ev_w_in[i], mla_q_norm[i], mla_kv_norm[i], mla_w_uq[i], mla_w_ukv[i],
                                       nat_rpb[i], ev_w_out[i], seq_lens))
        else:
            mixers.append({
                "w_in": od_w_in[i].astype(BF16),
                "dw_w": conv_dw_w[i], "dw_b": conv_dw_b[i][None],
                "ln_g": conv_ln_g[i][None], "ln_b": conv_ln_b[i][None],
                "s5": _s5_operators(s5_lambda_re[i], s5_lambda_im[i], s5_log_step[i], s5_b_re[i], s5_b_im[i],
                                    s5_c_re[i], s5_c_im[i], [sl // S5_CHUNK for sl in seq_lens]),
                "d": s5_d[i][None],
                "w_glu": s5_w_glu[i].astype(BF16),
                "wc": od_w_out[i][:CONV_CH].astype(BF16),
                "ws": od_w_out[i][CONV_CH:].astype(BF16),
            })

    def trunk(x):
        batch, seq_len, _ = x.shape
        h = x.reshape(batch * seq_len, D_MODEL)
        for layer in range(depth):
            g = norm_g[layer][:, None, :]
            h = _ffn(h, g[0], wg[layer, 0], wu[layer, 0], wd[layer, 0], g[1])
            mixer = _even_mixer if layer % 2 == 0 else _odd_mixer
            h = mixer(h, g[2], g[3], mixers[layer], batch, seq_len)
            h = _ffn(h, g[4], wg[layer, 1], wu[layer, 1], wd[layer, 1], g[5])
        return h.reshape(batch, seq_len, D_MODEL)

    return (trunk(x_prompt), trunk(x_sample))
```

```python
import functools
import math

import jax
import jax.numpy as jnp
from jax import lax
from jax.experimental import pallas as pl
from jax.experimental.pallas import tpu as pltpu

F32 = jnp.float32
BF16 = jnp.bfloat16

D_MODEL = 1024
D_FF = 2816
GRID_W = 64

MLA_HEADS = 8
MLA_Q_RANK = 256
MLA_KV_RANK = 128
MLA_NOPE = 64
MLA_ROPE = 32
MLA_V = 64
ROPE_THETA = 10000.0
HEAD_PAD = 128
V_ROWS = 80

NAT_HEADS = 8
NAT_HEAD_DIM = 64
NAT_W = NAT_HEADS * NAT_HEAD_DIM
NAT_KH_MAX = 8
NAT_KW = 16

CONV_CH = 512
CONV_WIDTH = 31
CONV_HALO = 16

S5_CH = 512
S5_GROUP = 16
S5_GROUPS = S5_CH // S5_GROUP
S5_STATE = 64
S5_CHUNK = 16
S5_PAIRS = S5_GROUPS // 2

FFN_RES_SCALE = 0.5
NORM_EPS = 1e-6
NEG_INF = -1e30

VMEM_LIMIT = 48 * 1024 * 1024


def _cparams(*sem):
    return pltpu.CompilerParams(dimension_semantics=sem, vmem_limit_bytes=VMEM_LIMIT)


def _rms(x, g):
    return x * lax.rsqrt(jnp.mean(x * x, axis=-1, keepdims=True) + NORM_EPS) * g


def _dot(a, b):
    return jnp.dot(a, b, preferred_element_type=F32)


def _dot_nt(a, b):
    return lax.dot_general(a, b, (((1,), (1,)), ((), ())), preferred_element_type=F32)


def _full(shape):
    n = len(shape)
    return pl.BlockSpec(shape, lambda *_: (0,) * n)


def _ffn_kernel(x_ref, gpre_ref, wg_ref, wu_ref, wd_ref, gpost_ref, o_ref, xn_scr, acc_scr, *, nf):
    j = pl.program_id(1)

    @pl.when(j == 0)
    def _():
        xn_scr[...] = _rms(x_ref[...], gpre_ref[...]).astype(BF16)
        acc_scr[...] = jnp.zeros_like(acc_scr)

    xn = xn_scr[...]
    gate = _dot(xn, wg_ref[...])
    up = _dot(xn, wu_ref[...])
    hdn = (gate * jax.nn.sigmoid(gate) * up).astype(BF16)
    acc_scr[...] += _dot(hdn, wd_ref[...])

    @pl.when(j == nf - 1)
    def _():
        o_ref[...] = x_ref[...] + FFN_RES_SCALE * _rms(acc_scr[...], gpost_ref[...])


def _ffn(h, gpre, wg, wu, wd, gpost, *, tm=512, nf=2):
    m = h.shape[0]
    tf = D_FF // nf
    return pl.pallas_call(
        functools.partial(_ffn_kernel, nf=nf),
        grid=(m // tm, nf),
        in_specs=[
            pl.BlockSpec((tm, D_MODEL), lambda i, j: (i, 0)),
            pl.BlockSpec((1, D_MODEL), lambda i, j: (0, 0)),
            pl.BlockSpec((D_MODEL, tf), lambda i, j: (0, j)),
            pl.BlockSpec((D_MODEL, tf), lambda i, j: (0, j)),
            pl.BlockSpec((tf, D_MODEL), lambda i, j: (j, 0)),
            pl.BlockSpec((1, D_MODEL), lambda i, j: (0, 0)),
        ],
        out_specs=pl.BlockSpec((tm, D_MODEL), lambda i, j: (i, 0)),
        out_shape=jax.ShapeDtypeStruct((m, D_MODEL), F32),
        scratch_shapes=[pltpu.VMEM((tm, D_MODEL), BF16), pltpu.VMEM((tm, D_MODEL), F32)],
        compiler_params=_cparams("parallel", "arbitrary"),
        name="ffn",
    )(h, gpre, wg, wu, wd, gpost)


EV_Z = MLA_Q_RANK + MLA_KV_RANK + 2 * HEAD_PAD + 3 * NAT_W
HP_ALL = MLA_HEADS * HEAD_PAD


def _ev_in_kernel(h_ref, g_ref, win_ref, qn_ref, kvn_ref, wuq_ref, wuqr_ref, wuk_ref, wuv_ref,
                  tab_ref, tabt_ref, vone_ref, qt_ref, k_ref, vt_ref, nq_ref, nk_ref, nv_ref):
    m = _rms(h_ref[...], g_ref[...]).astype(BF16)
    z = _dot(m, win_ref[...])
    c0 = MLA_Q_RANK
    c1 = c0 + MLA_KV_RANK
    c2 = c1 + HEAD_PAD
    c3 = c2 + HEAD_PAD
    q_lat = z[:, :c0]
    kv_lat = z[:, c0:c1]
    kr = z[:, c1:c2]
    kr_rot = z[:, c2:c3]
    nq_ref[...] = (z[:, c3:c3 + NAT_W] * (NAT_HEAD_DIM ** -0.5)).astype(BF16)
    nk_ref[...] = z[:, c3 + NAT_W:c3 + 2 * NAT_W].astype(BF16)
    nv_ref[...] = z[:, c3 + 2 * NAT_W:c3 + 3 * NAT_W].astype(BF16)

    qn = _rms(q_lat, qn_ref[...]).astype(BF16)
    kvn = _rms(kv_lat, kvn_ref[...]).astype(BF16)
    q_raw_t = _dot_nt(wuq_ref[...], qn)
    q_rot_t = _dot_nt(wuqr_ref[...], qn)
    v_t = _dot_nt(wuv_ref[...], kvn) + vone_ref[...]
    k_nope = _dot(kvn, wuk_ref[...])

    tab = tab_ref[...]
    k_rope = kr * tab[:, :HEAD_PAD] + kr_rot * tab[:, HEAD_PAD:]
    cq_t = tabt_ref[:HEAD_PAD, :]
    sq_t = tabt_ref[HEAD_PAD:, :]
    for hd in range(MLA_HEADS):
        sl = slice(hd * HEAD_PAD, (hd + 1) * HEAD_PAD)
        qt_ref[hd] = (q_raw_t[sl] * cq_t + q_rot_t[sl] * sq_t).astype(BF16)
        k_ref[:, sl] = (k_nope[:, sl] + k_rope).astype(BF16)
        vt_ref[hd] = v_t[hd * V_ROWS:(hd + 1) * V_ROWS].astype(BF16)


def _ev_in(h, g, p, seq_len, *, tm=512):
    m = h.shape[0]
    nblk = seq_len // tm
    tok = lambda w: pl.BlockSpec((tm, w), lambda i: (i, 0))
    feat = lambda r: pl.BlockSpec((MLA_HEADS, r, tm), lambda i: (0, 0, i))
    outs = [jax.ShapeDtypeStruct((MLA_HEADS, HEAD_PAD, m), BF16), jax.ShapeDtypeStruct((m, HP_ALL), BF16),
            jax.ShapeDtypeStruct((MLA_HEADS, V_ROWS, m), BF16)] + [jax.ShapeDtypeStruct((m, NAT_W), BF16)] * 3
    tab, tab_t = p["rope_tab"][seq_len]
    return pl.pallas_call(
        _ev_in_kernel,
        grid=(m // tm,),
        in_specs=[
            tok(D_MODEL), _full((1, D_MODEL)), _full((D_MODEL, EV_Z)),
            _full((1, MLA_Q_RANK)), _full((1, MLA_KV_RANK)),
            _full((HP_ALL, MLA_Q_RANK)), _full((HP_ALL, MLA_Q_RANK)),
            _full((MLA_KV_RANK, HP_ALL)), _full((MLA_HEADS * V_ROWS, MLA_KV_RANK)),
            pl.BlockSpec((tm, 2 * HEAD_PAD), lambda i: (i % nblk, 0)),
            pl.BlockSpec((2 * HEAD_PAD, tm), lambda i: (0, i % nblk)),
            _full((MLA_HEADS * V_ROWS, 1)),
        ],
        out_specs=[feat(HEAD_PAD), tok(HP_ALL), feat(V_ROWS)] + [tok(NAT_W)] * 3,
        out_shape=outs,
        compiler_params=_cparams("parallel"),
        name="ev_in",
    )(h, g, p["w_in"], p["q_norm"], p["kv_norm"], p["w_uq_t"], p["w_uq_rot_t"], p["w_uk"], p["w_uv_t"],
      tab, tab_t, p["v_one"])


def _mla_kernel(qt_ref, k_ref, vt_ref, o_ref, s_scr, *, tk, nk, unroll):
    qt = qt_ref[0]
    tq = qt.shape[1]

    def scores(j):
        off = pl.multiple_of(j * tk, tk)
        return _dot(k_ref[pl.ds(off, tk), :], qt)

    s_scr[0] = scores(0)

    def body(jj, carry):
        m_prev, acc = carry
        for i in range(unroll):
            j = jj * unroll + i
            s_scr[(i + 1) % 2] = scores(jnp.minimum(j + 1, nk - 1))
            st = s_scr[i % 2]
            m_new = jnp.maximum(m_prev, jnp.max(st, axis=0, keepdims=True))
            alpha = jnp.exp2(m_prev - m_new)
            pt = jnp.exp2(st - m_new).astype(BF16)
            off = pl.multiple_of(j * tk, tk)
            acc = alpha * acc + _dot(vt_ref[0, :, pl.ds(off, tk)], pt)
            m_prev = m_new
        return m_prev, acc

    m_init = jnp.full((1, tq), jnp.finfo(F32).min, F32)
    acc = jnp.zeros((V_ROWS, tq), F32)
    _, acc = lax.fori_loop(0, nk // unroll, body, (m_init, acc))
    out_t = acc[:MLA_V] / acc[MLA_V:MLA_V + 1]
    out_t = jnp.concatenate([out_t, jnp.zeros((HEAD_PAD - MLA_V, tq), F32)], axis=0)
    o_ref[...] = out_t.T.astype(BF16)


def _mla(qt, k, vt, batch, seq_len, *, tq=512, tk=512, unroll=4):
    m = k.shape[0]
    tk = min(tk, seq_len // unroll)
    nq = seq_len // tq
    nk = seq_len // tk
    assert unroll % 2 == 0 and nk % unroll == 0
    return pl.pallas_call(
        functools.partial(_mla_kernel, tk=tk, nk=nk, unroll=unroll),
        grid=(batch, MLA_HEADS, nq),
        in_specs=[
            pl.BlockSpec((1, HEAD_PAD, tq), lambda b, h, i: (h, 0, b * nq + i)),
            pl.BlockSpec((seq_len, HEAD_PAD), lambda b, h, i: (b, h)),
            pl.BlockSpec((1, V_ROWS, seq_len), lambda b, h, i: (h, 0, b)),
        ],
        out_specs=pl.BlockSpec((tq, HEAD_PAD), lambda b, h, i: (b * nq + i, h)),
        out_shape=jax.ShapeDtypeStruct((m, HP_ALL), BF16),
        scratch_shapes=[pltpu.VMEM((2, tk, tq), F32)],
        compiler_params=_cparams("parallel", "parallel", "arbitrary"),
        name="mla",
    )(qt, k, vt)


NAT_LANES = 2 * NAT_HEAD_DIM


def _nat_kernel(q_ref, k_ref, v_ref, bias_ref, o_ref, *, rows, kh, rblk):
    i = pl.program_id(2)
    lane = lax.broadcasted_iota(jnp.int32, (GRID_W, NAT_LANES), 1)
    head0 = lane < NAT_HEAD_DIM

    def row_body(rr, carry):
        r = i * rblk + rr
        start = jnp.clip(r - kh // 2, 0, rows - kh)
        delta = r - start
        koff = pl.multiple_of(start * GRID_W, GRID_W)
        qoff = pl.multiple_of(rr * GRID_W, GRID_W)
        qrow = q_ref[pl.ds(qoff, GRID_W), :]
        kwin = k_ref[pl.ds(koff, kh * GRID_W), :]
        vwin = v_ref[pl.ds(koff, kh * GRID_W), :]
        zero = jnp.zeros_like(qrow)
        q2 = jnp.concatenate([jnp.where(head0, qrow, zero), jnp.where(head0, zero, qrow)], axis=0)
        s = _dot_nt(q2, kwin) + bias_ref[delta, 0]
        e = jnp.exp(s - jnp.max(s, axis=1, keepdims=True))
        den = jnp.sum(e, axis=1, keepdims=True)
        o2 = _dot(e.astype(BF16), vwin) / den
        o_ref[pl.ds(qoff, GRID_W), :] = jnp.where(head0, o2[:GRID_W], o2[GRID_W:]).astype(BF16)
        return carry

    lax.fori_loop(0, rblk, row_body, 0, unroll=True)


def _nat(nq, nk, nv, bias, batch, seq_len, *, rblk=8):
    m = nq.shape[0]
    rows = seq_len // GRID_W
    kh = min(NAT_KH_MAX, rows)
    nblk = rows // rblk
    tq = rblk * GRID_W
    return pl.pallas_call(
        functools.partial(_nat_kernel, rows=rows, kh=kh, rblk=rblk),
        grid=(batch, NAT_HEADS // 2, nblk),
        in_specs=[
            pl.BlockSpec((tq, NAT_LANES), lambda b, hp, i: (b * nblk + i, hp)),
            pl.BlockSpec((seq_len, NAT_LANES), lambda b, hp, i: (b, hp)),
            pl.BlockSpec((seq_len, NAT_LANES), lambda b, hp, i: (b, hp)),
            pl.BlockSpec((kh, 1, 2 * GRID_W, kh * GRID_W), lambda b, hp, i: (0, hp, 0, 0)),
        ],
        out_specs=pl.BlockSpec((tq, NAT_LANES), lambda b, hp, i: (b * nblk + i, hp)),
        out_shape=jax.ShapeDtypeStruct((m, NAT_W), BF16),
        compiler_params=_cparams("parallel", "parallel", "arbitrary"),
        name="nat",
    )(nq, nk, nv, bias)


def _mix_out_kernel(h_ref, a_ref, b_ref, wa_ref, wb_ref, g_ref, o_ref):
    mix = _dot(a_ref[...], wa_ref[...]) + _dot(b_ref[...], wb_ref[...])
    o_ref[...] = h_ref[...] + _rms(mix, g_ref[...])


def _ev_out(h, a, b, wa, wb, g, *, tm=512):
    m = h.shape[0]
    tok = lambda w: pl.BlockSpec((tm, w), lambda i: (i, 0))
    return pl.pallas_call(
        _mix_out_kernel,
        grid=(m // tm,),
        in_specs=[tok(D_MODEL), tok(a.shape[1]), tok(b.shape[1]),
                  _full(wa.shape), _full(wb.shape), _full((1, D_MODEL))],
        out_specs=tok(D_MODEL),
        out_shape=jax.ShapeDtypeStruct((m, D_MODEL), F32),
        compiler_params=_cparams("parallel"),
        name="ev_out",
    )(h, a, b, wa, wb, g)


def _od_in_kernel(h_ref, g_ref, win_ref, u_ref, su_ref):
    m = _rms(h_ref[...], g_ref[...]).astype(BF16)
    z = _dot(m, win_ref[...])
    ca = z[:, :CONV_CH]
    cg = z[:, CONV_CH:2 * CONV_CH]
    u_ref[...] = ca * jax.nn.sigmoid(cg)
    su_ref[...] = z[:, 2 * CONV_CH:]


def _od_in(h, g, w_in, *, tm=512):
    m = h.shape[0]
    tok = lambda w: pl.BlockSpec((tm, w), lambda i: (i, 0))
    return pl.pallas_call(
        _od_in_kernel,
        grid=(m // tm,),
        in_specs=[tok(D_MODEL), _full((1, D_MODEL)), _full(w_in.shape)],
        out_specs=[tok(CONV_CH), tok(S5_CH)],
        out_shape=[jax.ShapeDtypeStruct((m, CONV_CH), F32), jax.ShapeDtypeStruct((m, S5_CH), F32)],
        compiler_params=_cparams("parallel"),
        name="od_in",
    )(h, g, w_in)


CONV_SUB = 32


def _conv_kernel(prev_ref, cur_ref, next_ref, w_ref, b_ref, lg_ref, lb_ref, o_ref, scr, *, tm, nblk):
    i = pl.program_id(0)
    first = (i % nblk) == 0
    last = (i % nblk) == nblk - 1
    scr[0:CONV_HALO, :] = jnp.where(first, 0.0, prev_ref[...])
    scr[CONV_HALO:CONV_HALO + tm, :] = cur_ref[...]
    scr[CONV_HALO + tm:, :] = jnp.where(last, 0.0, next_ref[...])
    w = w_ref[...]
    shift = CONV_HALO - CONV_WIDTH // 2

    def sub(c, carry):
        base = pl.multiple_of(c * CONV_SUB, CONV_SUB)
        win = scr[pl.ds(base, CONV_SUB + 2 * CONV_HALO), :]
        acc = jnp.zeros((CONV_SUB, CONV_CH), F32)
        for kk in range(CONV_WIDTH):
            acc = acc + win[shift + kk:shift + kk + CONV_SUB, :] * w[kk:kk + 1, :]
        y = acc + b_ref[...]
        mu = jnp.mean(y, axis=-1, keepdims=True)
        yc = y - mu
        yn = yc * lax.rsqrt(jnp.mean(yc * yc, axis=-1, keepdims=True) + NORM_EPS)
        yn = yn * lg_ref[...] + lb_ref[...]
        o_ref[pl.ds(base, CONV_SUB), :] = (yn * jax.nn.sigmoid(yn)).astype(BF16)
        return carry

    lax.fori_loop(0, tm // CONV_SUB, sub, 0)


def _conv(u, w, b, lg, lb, seq_len, *, tm=512):
    m = u.shape[0]
    nblk = seq_len // tm
    hb = tm // CONV_HALO
    nh = m // CONV_HALO
    return pl.pallas_call(
        functools.partial(_conv_kernel, tm=tm, nblk=nblk),
        grid=(m // tm,),
        in_specs=[
            pl.BlockSpec((CONV_HALO, CONV_CH), lambda i: (jnp.maximum(i * hb - 1, 0), 0)),
            pl.BlockSpec((tm, CONV_CH), lambda i: (i, 0)),
            pl.BlockSpec((CONV_HALO, CONV_CH), lambda i: (jnp.minimum((i + 1) * hb, nh - 1), 0)),
            _full((CONV_WIDTH, CONV_CH)), _full((1, CONV_CH)), _full((1, CONV_CH)), _full((1, CONV_CH)),
        ],
        out_specs=pl.BlockSpec((tm, CONV_CH), lambda i: (i, 0)),
        out_shape=jax.ShapeDtypeStruct((m, CONV_CH), BF16),
        scratch_shapes=[pltpu.VMEM((tm + 2 * CONV_HALO, CONV_CH), F32)],
        compiler_params=_cparams("parallel"),
        name="conv",
    )(u, u, u, w, b, lg, lb)


S5_W = 2 * S5_GROUP * S5_CHUNK
S5_X = 2 * S5_STATE


def _s5_kernel(u_ref, m_ref, b_ref, c_ref, pw_ref, y_ref, *, nchunk, nseq):
    u = u_ref[0]
    nc = nseq * nchunk
    s = _dot(u, b_ref[0])
    row = lax.broadcasted_iota(jnp.int32, (nc, S5_X), 0) & (nchunk - 1)
    nstep = nchunk.bit_length() - 1
    states = []
    for d in range(2):
        xr = s[:, (2 * d) * S5_X:(2 * d + 1) * S5_X]
        xi = s[:, (2 * d + 1) * S5_X:(2 * d + 2) * S5_X]

        def shifted(x, sh):
            if d == 0:
                return jnp.where(row >= sh, pltpu.roll(x, sh, 0), 0.0)
            return jnp.where(row < nchunk - sh, pltpu.roll(x, nc - sh, 0), 0.0)

        for k in range(nstep):
            ar = pw_ref[0, d, k, 0:1, :]
            ai = pw_ref[0, d, k, 1:2, :]
            sr = shifted(xr, 1 << k)
            si = shifted(xi, 1 << k)
            xr, xi = xr + ar * sr - ai * si, xi + ar * si + ai * sr
        states += [shifted(xr, 1), shifted(xi, 1)]
    x = jnp.concatenate(states, axis=1).astype(BF16)
    y_ref[0] = _dot(u, m_ref[0]) + _dot(x, c_ref[0])


def _s5_scan(u_p, ops, nseq, nchunk):
    npair, nc, w = u_p.shape
    assert nchunk & (nchunk - 1) == 0 and nc == nseq * nchunk
    nstep = max(nchunk.bit_length() - 1, 1)
    grp = lambda *shape: pl.BlockSpec((1,) + shape, lambda i: (i,) + (0,) * len(shape))
    return pl.pallas_call(
        functools.partial(_s5_kernel, nchunk=nchunk, nseq=nseq),
        grid=(npair,),
        in_specs=[grp(nc, w), grp(w, w), grp(w, 4 * S5_X), grp(4 * S5_X, w), grp(2, nstep, 2, S5_X)],
        out_specs=grp(nc, w),
        out_shape=jax.ShapeDtypeStruct((npair, nc, w), F32),
        compiler_params=_cparams("parallel"),
        name="s5",
    )(u_p, ops["m"], ops["b"], ops["c"], ops["pw"][nchunk])


def _od_out_kernel(h_ref, c_ref, su_ref, ys_ref, d_ref, wglu_ref, wc_ref, ws_ref, g_ref, o_ref):
    y = d_ref[...] * su_ref[...] + ys_ref[...]
    z = jax.nn.gelu(y, approximate=True)
    sg = (z * jax.nn.sigmoid(_dot(z.astype(BF16), wglu_ref[...]))).astype(BF16)
    mix = _dot(c_ref[...], wc_ref[...]) + _dot(sg, ws_ref[...])
    o_ref[...] = h_ref[...] + _rms(mix, g_ref[...])


def _od_out(h, c, su, ys, d, wglu, wc, ws, g, *, tm=512):
    m = h.shape[0]
    tok = lambda w: pl.BlockSpec((tm, w), lambda i: (i, 0))
    return pl.pallas_call(
        _od_out_kernel,
        grid=(m // tm,),
        in_specs=[tok(D_MODEL), tok(CONV_CH), tok(S5_CH), tok(S5_CH), _full((1, S5_CH)),
                  _full(wglu.shape), _full(wc.shape), _full(ws.shape), _full((1, D_MODEL))],
        out_specs=tok(D_MODEL),
        out_shape=jax.ShapeDtypeStruct((m, D_MODEL), F32),
        compiler_params=_cparams("parallel"),
        name="od_out",
    )(h, c, su, ys, d, wglu, wc, ws, g)


def _head_rows(w, widths, offsets, rows_per_head):
    k = w.shape[0]
    per = w.shape[1] // MLA_HEADS
    w = w.reshape(k, MLA_HEADS, per)
    out = jnp.zeros((k, MLA_HEADS, rows_per_head), w.dtype)
    src = 0
    for wd, off in zip(widths, offsets):
        if off is not None:
            out = out.at[:, :, off:off + wd].set(w[:, :, src:src + wd])
        src += wd
    return out.reshape(k, MLA_HEADS * rows_per_head).T


def _rot_cols(w):
    half = w.shape[-1] // 2
    return jnp.concatenate([-w[..., half:], w[..., :half]], axis=-1)


def _rope_tables(seq_len):
    half = MLA_ROPE // 2
    inv = ROPE_THETA ** (-jnp.arange(half, dtype=F32) / half)
    ang = jnp.arange(seq_len, dtype=F32)[:, None] * inv[None, :]
    cos = jnp.concatenate([jnp.cos(ang)] * 2, axis=1)
    sin = jnp.concatenate([jnp.sin(ang)] * 2, axis=1)
    scale = (MLA_NOPE + MLA_ROPE) ** -0.5 * math.log2(math.e)
    z64 = jnp.zeros((seq_len, MLA_NOPE), F32)
    z32 = jnp.zeros((seq_len, HEAD_PAD - MLA_NOPE - MLA_ROPE), F32)
    cq = jnp.concatenate([z64 + scale, cos * scale, z32], axis=1)
    sq = jnp.concatenate([z64, sin * scale, z32], axis=1)
    ck = jnp.concatenate([z64, cos, z32], axis=1)
    sk = jnp.concatenate([z64, sin, z32], axis=1)
    return jnp.concatenate([ck, sk], axis=1), jnp.concatenate([cq, sq], axis=1).T


def _nat_bias(rpb, kh):
    c = jnp.arange(GRID_W)
    col_start = jnp.clip(c - NAT_KW // 2, 0, GRID_W - NAT_KW)
    col_ok = (c[None, :] >= col_start[:, None]) & (c[None, :] < col_start[:, None] + NAT_KW)
    col_off = jnp.clip(c[None, :] - c[:, None], -(NAT_KW - 1), NAT_KW - 1) + (NAT_KW - 1)
    delta = jnp.arange(kh)
    row_off = jnp.arange(kh)[None, :] - delta[:, None] + (NAT_KH_MAX - 1)
    bias = rpb[:, row_off[:, None, :, None], col_off[None, :, None, :]]
    bias = jnp.where(col_ok[None, None, :, None, :], bias.astype(F32), NEG_INF)
    bias = bias.transpose(1, 0, 2, 3, 4)
    return bias.reshape(kh, NAT_HEADS // 2, 2 * GRID_W, kh * GRID_W)


def _pair_diag(x):
    g, r, c = x.shape
    x = x.reshape(g // 2, 2, r, c)
    eye = jnp.eye(2, dtype=x.dtype)
    return jnp.einsum("pars,ab->parbs", x, eye).reshape(g // 2, 2 * r, 2 * c)


def _s5_operators(lam_re, lam_im, log_step, b_re, b_im, c_re, c_im, nchunks):
    t = S5_CHUNK
    w = S5_GROUP * t
    dt = jnp.exp(log_step)[:, :, None]
    ar, ai = lam_re * dt, lam_im * dt
    er = jnp.exp(ar)
    lbr, lbi = er * jnp.cos(ai), er * jnp.sin(ai)
    den = lam_re * lam_re + lam_im * lam_im
    fr = ((lbr - 1.0) * lam_re + lbi * lam_im) / den
    fi = (lbi * lam_re - (lbr - 1.0) * lam_im) / den
    bbr = fr[..., None] * b_re - fi[..., None] * b_im
    bbi = fr[..., None] * b_im + fi[..., None] * b_re

    def power(d):
        d = d.astype(F32)[None, None, :, None]
        mag = jnp.exp(ar[:, :, None, :] * d)
        return mag * jnp.cos(ai[:, :, None, :] * d), mag * jnp.sin(ai[:, :, None, :] * d)

    hi = lax.Precision.HIGHEST
    pr, pi = power(jnp.arange(t + 1))
    wr = c_re[:, :, None] * pr[:, :, :, None, :] - c_im[:, :, None] * pi[:, :, :, None, :]
    wi = c_re[:, :, None] * pi[:, :, :, None, :] + c_im[:, :, None] * pr[:, :, :, None, :]
    kmat = (jnp.einsum("xgdcp,xgpk->xgdck", wr[:, :, :t], bbr, precision=hi)
            - jnp.einsum("xgdcp,xgpk->xgdck", wi[:, :, :t], bbi, precision=hi))
    kf, kr = kmat[0], kmat[1]
    kcat = jnp.concatenate([kr[:, :0:-1], kf[:, :1] + kr[:, :1], kf[:, 1:]], axis=1)
    idx = jnp.arange(t)[None, :] - jnp.arange(t)[:, None] + (t - 1)
    mm = kcat[:, idx]
    mm = mm.transpose(0, 1, 4, 2, 3).reshape(S5_GROUPS, w, w)

    def contrib(pr_, pi_, br_, bi_):
        brt, bit = br_.transpose(0, 2, 1)[:, None], bi_.transpose(0, 2, 1)[:, None]
        re = pr_[:, :, None, :] * brt - pi_[:, :, None, :] * bit
        im = pr_[:, :, None, :] * bit + pi_[:, :, None, :] * brt
        return re.reshape(S5_GROUPS, w, S5_STATE), im.reshape(S5_GROUPS, w, S5_STATE)

    bf_re, bf_im = contrib(pr[0, :, t - 1::-1], pi[0, :, t - 1::-1], bbr[0], bbi[0])
    br_re, br_im = contrib(pr[1, :, :t], pi[1, :, :t], bbr[1], bbi[1])
    bmat = jnp.concatenate([_pair_diag(x) for x in (bf_re, bf_im, br_re, br_im)], axis=2)

    def readout(w_):
        return w_.transpose(0, 3, 1, 2).reshape(S5_GROUPS, S5_STATE, w)

    cf_re, cf_im = readout(wr[0, :, 1:t + 1]), readout(-wi[0, :, 1:t + 1])
    cr_re, cr_im = readout(wr[1, :, t:0:-1]), readout(-wi[1, :, t:0:-1])
    cmat = jnp.concatenate([_pair_diag(x) for x in (cf_re, cf_im, cr_re, cr_im)], axis=1)

    pws = {}
    for nchunk in nchunks:
        nstep = max(nchunk.bit_length() - 1, 1)
        qr, qi = power(t * (2 ** jnp.arange(nstep)))

        def lanes(q):
            q = q.reshape(2, S5_PAIRS, 2, nstep, S5_STATE).transpose(1, 0, 3, 2, 4)
            return q.reshape(S5_PAIRS, 2, nstep, S5_X)

        pws[nchunk] = jnp.stack([lanes(qr), lanes(qi)], axis=3)
    return {"m": _pair_diag(mm).astype(BF16), "b": bmat.astype(BF16), "c": cmat.astype(BF16), "pw": pws}


def _even_params(ev_w_in, q_norm, kv_norm, w_uq, w_ukv, rpb, ev_w_out, seq_lens):
    c0 = MLA_Q_RANK
    c1 = c0 + MLA_KV_RANK
    c2 = c1 + MLA_ROPE
    w_kr = ev_w_in[:, c1:c2]
    pad_l = jnp.zeros((D_MODEL, MLA_NOPE), F32)
    pad_r = jnp.zeros((D_MODEL, HEAD_PAD - MLA_NOPE - MLA_ROPE), F32)
    w_in = jnp.concatenate([
        ev_w_in[:, :c1],
        pad_l, w_kr, pad_r,
        pad_l, _rot_cols(w_kr), pad_r,
        ev_w_in[:, c2:],
    ], axis=1).astype(BF16)
    uq = w_uq.reshape(MLA_Q_RANK, MLA_HEADS, MLA_NOPE + MLA_ROPE)
    uq_rot = jnp.concatenate([jnp.zeros_like(uq[..., :MLA_NOPE]), _rot_cols(uq[..., MLA_NOPE:])], axis=-1)
    nope_rope = ((MLA_NOPE, MLA_ROPE), (0, MLA_NOPE))
    v_one = jnp.zeros((MLA_HEADS, V_ROWS), F32).at[:, MLA_V].set(1.0).reshape(MLA_HEADS * V_ROWS, 1)
    wa = jnp.zeros((MLA_HEADS, HEAD_PAD, D_MODEL), F32)
    wa = wa.at[:, :MLA_V].set(ev_w_out[:MLA_HEADS * MLA_V].reshape(MLA_HEADS, MLA_V, D_MODEL))
    return {
        "w_in": w_in,
        "q_norm": q_norm[None], "kv_norm": kv_norm[None],
        "w_uq_t": _head_rows(w_uq, *nope_rope, HEAD_PAD).astype(BF16),
        "w_uq_rot_t": _head_rows(uq_rot.reshape(MLA_Q_RANK, -1), *nope_rope, HEAD_PAD).astype(BF16),
        "w_uk": _head_rows(w_ukv, (MLA_NOPE, MLA_V), (0, None), HEAD_PAD).T.astype(BF16),
        "w_uv_t": _head_rows(w_ukv, (MLA_NOPE, MLA_V), (None, 0), V_ROWS).astype(BF16),
        "v_one": v_one,
        "rope_tab": {sl: _rope_tables(sl) for sl in seq_lens},
        "nat_bias": {sl: _nat_bias(rpb, min(NAT_KH_MAX, sl // GRID_W)) for sl in seq_lens},
        "wa": wa.reshape(HP_ALL, D_MODEL).astype(BF16),
        "wb": ev_w_out[MLA_HEADS * MLA_V:].astype(BF16),
    }


def _even_mixer(h, g_pre, g_post, p, batch, seq_len):
    qt, k, vt, nq, nk, nv = _ev_in(h, g_pre, p, seq_len)
    a = _mla(qt, k, vt, batch, seq_len)
    b = _nat(nq, nk, nv, p["nat_bias"][seq_len], batch, seq_len)
    return _ev_out(h, a, b, p["wa"], p["wb"], g_post)


def _odd_mixer(h, g_pre, g_post, p, batch, seq_len):
    u, su = _od_in(h, g_pre, p["w_in"])
    c = _conv(u, p["dw_w"], p["dw_b"], p["ln_g"], p["ln_b"], seq_len)
    nchunk = seq_len // S5_CHUNK
    nc = batch * nchunk
    u_p = su.astype(BF16).reshape(nc, S5_CHUNK, S5_PAIRS, 2, S5_GROUP)
    u_p = u_p.transpose(2, 0, 3, 1, 4).reshape(S5_PAIRS, nc, S5_W)
    y_p = _s5_scan(u_p, p["s5"], batch, nchunk)
    ys = y_p.reshape(S5_PAIRS, nc, 2, S5_CHUNK, S5_GROUP).transpose(1, 3, 0, 2, 4)
    ys = ys.reshape(batch * seq_len, S5_CH)
    return _od_out(h, c, su, ys, p["d"], p["w_glu"], p["wc"], p["ws"], g_post)


def kernel(x_prompt, x_sample, norm_g, ffn_w_gate, ffn_w_up, ffn_w_down, ev_w_in, mla_q_norm, mla_kv_norm, mla_w_uq, mla_w_ukv, nat_rpb, ev_w_out, od_w_in, conv_dw_w, conv_dw_b, conv_ln_g, conv_ln_b, s5_lambda_re, s5_lambda_im, s5_log_step, s5_b_re, s5_b_im, s5_c_re, s5_c_im, s5_d, s5_w_glu, od_w_out):
    depth = norm_g.shape[0]
    seq_lens = sorted({x_prompt.shape[1], x_sample.shape[1]})
    wg = ffn_w_gate.astype(BF16)
    wu = ffn_w_up.astype(BF16)
    wd = ffn_w_down.astype(BF16)
    mixers = []
    for layer in range(depth):
        i = layer // 2
        if layer % 2 == 0:
            mixers.append(_even_params(ev_w_in[i], mla_q_norm[i], mla_kv_norm[i], mla_w_uq[i], mla_w_ukv[i],
                                       nat_rpb[i], ev_w_out[i], seq_lens))
        else:
            mixers.append({
                "w_in": od_w_in[i].astype(BF16),
                "dw_w": conv_dw_w[i], "dw_b": conv_dw_b[i][None],
                "ln_g": conv_ln_g[i][None], "ln_b": conv_ln_b[i][None],
                "s5": _s5_operators(s5_lambda_re[i], s5_lambda_im[i], s5_log_step[i], s5_b_re[i], s5_b_im[i],
                                    s5_c_re[i], s5_c_im[i], [sl // S5_CHUNK for sl in seq_lens]),
                "d": s5_d[i][None],
                "w_glu": s5_w_glu[i].astype(BF16),
                "wc": od_w_out[i][:CONV_CH].astype(BF16),
                "ws": od_w_out[i][CONV_CH:].astype(BF16),
            })

    def trunk(x):
        batch, seq_len, _ = x.shape
        h = x.reshape(batch * seq_len, D_MODEL)
        for layer in range(depth):
            g = norm_g[layer][:, None, :]
            h = _ffn(h, g[0], wg[layer, 0], wu[layer, 0], wd[layer, 0], g[1])
            mixer = _even_mixer if layer % 2 == 0 else _odd_mixer
            h = mixer(h, g[2], g[3], mixers[layer], batch, seq_len)
            h = _ffn(h, g[4], wg[layer, 1], wu[layer, 1], wd[layer, 1], g[5])
        return h.reshape(batch, seq_len, D_MODEL)

    return (trunk(x_prompt), trunk(x_sample))
```

```python
import functools
import math

import jax
import jax.numpy as jnp
from jax import lax
from jax.experimental import pallas as pl
from jax.experimental.pallas import tpu as pltpu

F32 = jnp.float32
BF16 = jnp.bfloat16

D_MODEL = 1024
D_FF = 2816
GRID_W = 64

MLA_HEADS = 8
MLA_Q_RANK = 256
MLA_KV_RANK = 128
MLA_NOPE = 64
MLA_ROPE = 32
MLA_V = 64
ROPE_THETA = 10000.0
HEAD_PAD = 128
V_ROWS = 80

NAT_HEADS = 8
NAT_HEAD_DIM = 64
NAT_W = NAT_HEADS * NAT_HEAD_DIM
NAT_KH_MAX = 8
NAT_KW = 16

CONV_CH = 512
CONV_WIDTH = 31
CONV_HALO = 16

S5_CH = 512
S5_GROUP = 16
S5_GROUPS = S5_CH // S5_GROUP
S5_STATE = 64
S5_CHUNK = 16
S5_PAIRS = S5_GROUPS // 2

FFN_RES_SCALE = 0.5
NORM_EPS = 1e-6
NEG_INF = -1e30

VMEM_LIMIT = 48 * 1024 * 1024


def _cparams(*sem):
    return pltpu.CompilerParams(dimension_semantics=sem, vmem_limit_bytes=VMEM_LIMIT)


def _rms(x, g):
    return x * lax.rsqrt(jnp.mean(x * x, axis=-1, keepdims=True) + NORM_EPS) * g


def _dot(a, b):
    return jnp.dot(a, b, preferred_element_type=F32)


def _dot_nt(a, b):
    return lax.dot_general(a, b, (((1,), (1,)), ((), ())), preferred_element_type=F32)


def _full(shape):
    n = len(shape)
    return pl.BlockSpec(shape, lambda *_: (0,) * n)


def _ffn_kernel(x_ref, gpre_ref, wg_ref, wu_ref, wd_ref, gpost_ref, o_ref, xn_scr, acc_scr, *, nf):
    j = pl.program_id(1)

    @pl.when(j == 0)
    def _():
        xn_scr[...] = _rms(x_ref[...], gpre_ref[...]).astype(BF16)
        acc_scr[...] = jnp.zeros_like(acc_scr)

    xn = xn_scr[...]
    gate = _dot(xn, wg_ref[...])
    up = _dot(xn, wu_ref[...])
    hdn = (gate * jax.nn.sigmoid(gate) * up).astype(BF16)
    acc_scr[...] += _dot(hdn, wd_ref[...])

    @pl.when(j == nf - 1)
    def _():
        o_ref[...] = x_ref[...] + FFN_RES_SCALE * _rms(acc_scr[...], gpost_ref[...])


def _ffn(h, gpre, wg, wu, wd, gpost, *, tm=512, nf=2):
    m = h.shape[0]
    tf = D_FF // nf
    return pl.pallas_call(
        functools.partial(_ffn_kernel, nf=nf),
        grid=(m // tm, nf),
        in_specs=[
            pl.BlockSpec((tm, D_MODEL), lambda i, j: (i, 0)),
            pl.BlockSpec((1, D_MODEL), lambda i, j: (0, 0)),
            pl.BlockSpec((D_MODEL, tf), lambda i, j: (0, j)),
            pl.BlockSpec((D_MODEL, tf), lambda i, j: (0, j)),
            pl.BlockSpec((tf, D_MODEL), lambda i, j: (j, 0)),
            pl.BlockSpec((1, D_MODEL), lambda i, j: (0, 0)),
        ],
        out_specs=pl.BlockSpec((tm, D_MODEL), lambda i, j: (i, 0)),
        out_shape=jax.ShapeDtypeStruct((m, D_MODEL), F32),
        scratch_shapes=[pltpu.VMEM((tm, D_MODEL), BF16), pltpu.VMEM((tm, D_MODEL), F32)],
        compiler_params=_cparams("parallel", "arbitrary"),
        name="ffn",
    )(h, gpre, wg, wu, wd, gpost)


EV_Z = MLA_Q_RANK + MLA_KV_RANK + 2 * HEAD_PAD + 3 * NAT_W
HP_ALL = MLA_HEADS * HEAD_PAD


def _ev_in_kernel(h_ref, g_ref, win_ref, qn_ref, kvn_ref, wuq_ref, wuqr_ref, wuk_ref, wuv_ref,
                  tab_ref, tabt_ref, vone_ref, qt_ref, k_ref, vt_ref, nq_ref, nk_ref, nv_ref):
    m = _rms(h_ref[...], g_ref[...]).astype(BF16)
    z = _dot(m, win_ref[...])
    c0 = MLA_Q_RANK
    c1 = c0 + MLA_KV_RANK
    c2 = c1 + HEAD_PAD
    c3 = c2 + HEAD_PAD
    q_lat = z[:, :c0]
    kv_lat = z[:, c0:c1]
    kr = z[:, c1:c2]
    kr_rot = z[:, c2:c3]
    nq_ref[...] = (z[:, c3:c3 + NAT_W] * (NAT_HEAD_DIM ** -0.5)).astype(BF16)
    nk_ref[...] = z[:, c3 + NAT_W:c3 + 2 * NAT_W].astype(BF16)
    nv_ref[...] = z[:, c3 + 2 * NAT_W:c3 + 3 * NAT_W].astype(BF16)

    qn = _rms(q_lat, qn_ref[...]).astype(BF16)
    kvn = _rms(kv_lat, kvn_ref[...]).astype(BF16)
    q_raw_t = _dot_nt(wuq_ref[...], qn)
    q_rot_t = _dot_nt(wuqr_ref[...], qn)
    v_t = _dot_nt(wuv_ref[...], kvn) + vone_ref[...]
    k_nope = _dot(kvn, wuk_ref[...])

    tab = tab_ref[...]
    k_rope = kr * tab[:, :HEAD_PAD] + kr_rot * tab[:, HEAD_PAD:]
    cq_t = tabt_ref[:HEAD_PAD, :]
    sq_t = tabt_ref[HEAD_PAD:, :]
    for hd in range(MLA_HEADS):
        sl = slice(hd * HEAD_PAD, (hd + 1) * HEAD_PAD)
        qt_ref[hd] = (q_raw_t[sl] * cq_t + q_rot_t[sl] * sq_t).astype(BF16)
        k_ref[:, sl] = (k_nope[:, sl] + k_rope).astype(BF16)
        vt_ref[hd] = v_t[hd * V_ROWS:(hd + 1) * V_ROWS].astype(BF16)


def _ev_in(h, g, p, seq_len, *, tm=512):
    m = h.shape[0]
    nblk = seq_len // tm
    tok = lambda w: pl.BlockSpec((tm, w), lambda i: (i, 0))
    feat = lambda r: pl.BlockSpec((MLA_HEADS, r, tm), lambda i: (0, 0, i))
    outs = [jax.ShapeDtypeStruct((MLA_HEADS, HEAD_PAD, m), BF16), jax.ShapeDtypeStruct((m, HP_ALL), BF16),
            jax.ShapeDtypeStruct((MLA_HEADS, V_ROWS, m), BF16)] + [jax.ShapeDtypeStruct((m, NAT_W), BF16)] * 3
    tab, tab_t = p["rope_tab"][seq_len]
    return pl.pallas_call(
        _ev_in_kernel,
        grid=(m // tm,),
        in_specs=[
            tok(D_MODEL), _full((1, D_MODEL)), _full((D_MODEL, EV_Z)),
            _full((1, MLA_Q_RANK)), _full((1, MLA_KV_RANK)),
            _full((HP_ALL, MLA_Q_RANK)), _full((HP_ALL, MLA_Q_RANK)),
            _full((MLA_KV_RANK, HP_ALL)), _full((MLA_HEADS * V_ROWS, MLA_KV_RANK)),
            pl.BlockSpec((tm, 2 * HEAD_PAD), lambda i: (i % nblk, 0)),
            pl.BlockSpec((2 * HEAD_PAD, tm), lambda i: (0, i % nblk)),
            _full((MLA_HEADS * V_ROWS, 1)),
        ],
        out_specs=[feat(HEAD_PAD), tok(HP_ALL), feat(V_ROWS)] + [tok(NAT_W)] * 3,
        out_shape=outs,
        compiler_params=_cparams("parallel"),
        name="ev_in",
    )(h, g, p["w_in"], p["q_norm"], p["kv_norm"], p["w_uq_t"], p["w_uq_rot_t"], p["w_uk"], p["w_uv_t"],
      tab, tab_t, p["v_one"])


def _mla_kernel(qt_ref, k_ref, vt_ref, o_ref, s_scr, *, tk, nk, unroll):
    qt = qt_ref[0]
    tq = qt.shape[1]

    def scores(j):
        off = pl.multiple_of(j * tk, tk)
        return _dot(k_ref[pl.ds(off, tk), :], qt)

    s_scr[0] = scores(0)

    def body(jj, carry):
        m_prev, acc = carry
        for i in range(unroll):
            j = jj * unroll + i
            s_scr[(i + 1) % 2] = scores(jnp.minimum(j + 1, nk - 1))
            st = s_scr[i % 2]
            m_new = jnp.maximum(m_prev, jnp.max(st, axis=0, keepdims=True))
            alpha = jnp.exp2(m_prev - m_new)
            pt = jnp.exp2(st - m_new).astype(BF16)
            off = pl.multiple_of(j * tk, tk)
            acc = alpha * acc + _dot(vt_ref[0, :, pl.ds(off, tk)], pt)
            m_prev = m_new
        return m_prev, acc

    m_init = jnp.full((1, tq), jnp.finfo(F32).min, F32)
    acc = jnp.zeros((V_ROWS, tq), F32)
    _, acc = lax.fori_loop(0, nk // unroll, body, (m_init, acc))
    out_t = acc[:MLA_V] / acc[MLA_V:MLA_V + 1]
    out_t = jnp.concatenate([out_t, jnp.zeros((HEAD_PAD - MLA_V, tq), F32)], axis=0)
    o_ref[...] = out_t.T.astype(BF16)


def _mla(qt, k, vt, batch, seq_len, *, tq=512, tk=512, unroll=8):
    m = k.shape[0]
    tk = min(tk, seq_len // unroll)
    nq = seq_len // tq
    nk = seq_len // tk
    assert unroll % 2 == 0 and nk % unroll == 0
    return pl.pallas_call(
        functools.partial(_mla_kernel, tk=tk, nk=nk, unroll=unroll),
        grid=(batch, MLA_HEADS, nq),
        in_specs=[
            pl.BlockSpec((1, HEAD_PAD, tq), lambda b, h, i: (h, 0, b * nq + i)),
            pl.BlockSpec((seq_len, HEAD_PAD), lambda b, h, i: (b, h)),
            pl.BlockSpec((1, V_ROWS, seq_len), lambda b, h, i: (h, 0, b)),
        ],
        out_specs=pl.BlockSpec((tq, HEAD_PAD), lambda b, h, i: (b * nq + i, h)),
        out_shape=jax.ShapeDtypeStruct((m, HP_ALL), BF16),
        scratch_shapes=[pltpu.VMEM((2, tk, tq), F32)],
        compiler_params=_cparams("parallel", "parallel", "arbitrary"),
        name="mla",
    )(qt, k, vt)


NAT_LANES = 2 * NAT_HEAD_DIM


def _nat_kernel(q_ref, k_ref, v_ref, bias_ref, o_ref, *, rows, kh, rblk):
    i = pl.program_id(2)
    lane = lax.broadcasted_iota(jnp.int32, (GRID_W, NAT_LANES), 1)
    head0 = lane < NAT_HEAD_DIM

    def row_body(rr, carry):
        r = i * rblk + rr
        start = jnp.clip(r - kh // 2, 0, rows - kh)
        delta = r - start
        koff = pl.multiple_of(start * GRID_W, GRID_W)
        qoff = pl.multiple_of(rr * GRID_W, GRID_W)
        qrow = q_ref[pl.ds(qoff, GRID_W), :]
        kwin = k_ref[pl.ds(koff, kh * GRID_W), :]
        vwin = v_ref[pl.ds(koff, kh * GRID_W), :]
        zero = jnp.zeros_like(qrow)
        q2 = jnp.concatenate([jnp.where(head0, qrow, zero), jnp.where(head0, zero, qrow)], axis=0)
        s = _dot_nt(q2, kwin) + bias_ref[delta, 0]
        e = jnp.exp(s - jnp.max(s, axis=1, keepdims=True))
        den = jnp.sum(e, axis=1, keepdims=True)
        o2 = _dot(e.astype(BF16), vwin) / den
        o_ref[pl.ds(qoff, GRID_W), :] = jnp.where(head0, o2[:GRID_W], o2[GRID_W:]).astype(BF16)
        return carry

    lax.fori_loop(0, rblk, row_body, 0, unroll=True)


def _nat(nq, nk, nv, bias, batch, seq_len, *, rblk=8):
    m = nq.shape[0]
    rows = seq_len // GRID_W
    kh = min(NAT_KH_MAX, rows)
    nblk = rows // rblk
    tq = rblk * GRID_W
    return pl.pallas_call(
        functools.partial(_nat_kernel, rows=rows, kh=kh, rblk=rblk),
        grid=(batch, NAT_HEADS // 2, nblk),
        in_specs=[
            pl.BlockSpec((tq, NAT_LANES), lambda b, hp, i: (b * nblk + i, hp)),
            pl.BlockSpec((seq_len, NAT_LANES), lambda b, hp, i: (b, hp)),
            pl.BlockSpec((seq_len, NAT_LANES), lambda b, hp, i: (b, hp)),
            pl.BlockSpec((kh, 1, 2 * GRID_W, kh * GRID_W), lambda b, hp, i: (0, hp, 0, 0)),
        ],
        out_specs=pl.BlockSpec((tq, NAT_LANES), lambda b, hp, i: (b * nblk + i, hp)),
        out_shape=jax.ShapeDtypeStruct((m, NAT_W), BF16),
        compiler_params=_cparams("parallel", "parallel", "arbitrary"),
        name="nat",
    )(nq, nk, nv, bias)


def _mix_out_kernel(h_ref, a_ref, b_ref, wa_ref, wb_ref, g_ref, o_ref):
    mix = _dot(a_ref[...], wa_ref[...]) + _dot(b_ref[...], wb_ref[...])
    o_ref[...] = h_ref[...] + _rms(mix, g_ref[...])


def _ev_out(h, a, b, wa, wb, g, *, tm=512):
    m = h.shape[0]
    tok = lambda w: pl.BlockSpec((tm, w), lambda i: (i, 0))
    return pl.pallas_call(
        _mix_out_kernel,
        grid=(m // tm,),
        in_specs=[tok(D_MODEL), tok(a.shape[1]), tok(b.shape[1]),
                  _full(wa.shape), _full(wb.shape), _full((1, D_MODEL))],
        out_specs=tok(D_MODEL),
        out_shape=jax.ShapeDtypeStruct((m, D_MODEL), F32),
        compiler_params=_cparams("parallel"),
        name="ev_out",
    )(h, a, b, wa, wb, g)


def _od_in_kernel(h_ref, g_ref, win_ref, u_ref, su_ref):
    m = _rms(h_ref[...], g_ref[...]).astype(BF16)
    z = _dot(m, win_ref[...])
    ca = z[:, :CONV_CH]
    cg = z[:, CONV_CH:2 * CONV_CH]
    u_ref[...] = ca * jax.nn.sigmoid(cg)
    su_ref[...] = z[:, 2 * CONV_CH:]


def _od_in(h, g, w_in, *, tm=512):
    m = h.shape[0]
    tok = lambda w: pl.BlockSpec((tm, w), lambda i: (i, 0))
    return pl.pallas_call(
        _od_in_kernel,
        grid=(m // tm,),
        in_specs=[tok(D_MODEL), _full((1, D_MODEL)), _full(w_in.shape)],
        out_specs=[tok(CONV_CH), tok(S5_CH)],
        out_shape=[jax.ShapeDtypeStruct((m, CONV_CH), F32), jax.ShapeDtypeStruct((m, S5_CH), F32)],
        compiler_params=_cparams("parallel"),
        name="od_in",
    )(h, g, w_in)


CONV_SUB = 32


def _conv_kernel(prev_ref, cur_ref, next_ref, w_ref, b_ref, lg_ref, lb_ref, o_ref, scr, *, tm, nblk):
    i = pl.program_id(0)
    first = (i % nblk) == 0
    last = (i % nblk) == nblk - 1
    scr[0:CONV_HALO, :] = jnp.where(first, 0.0, prev_ref[...])
    scr[CONV_HALO:CONV_HALO + tm, :] = cur_ref[...]
    scr[CONV_HALO + tm:, :] = jnp.where(last, 0.0, next_ref[...])
    w = w_ref[...]
    shift = CONV_HALO - CONV_WIDTH // 2

    def sub(c, carry):
        base = pl.multiple_of(c * CONV_SUB, CONV_SUB)
        win = scr[pl.ds(base, CONV_SUB + 2 * CONV_HALO), :]
        acc = jnp.zeros((CONV_SUB, CONV_CH), F32)
        for kk in range(CONV_WIDTH):
            acc = acc + win[shift + kk:shift + kk + CONV_SUB, :] * w[kk:kk + 1, :]
        y = acc + b_ref[...]
        mu = jnp.mean(y, axis=-1, keepdims=True)
        yc = y - mu
        yn = yc * lax.rsqrt(jnp.mean(yc * yc, axis=-1, keepdims=True) + NORM_EPS)
        yn = yn * lg_ref[...] + lb_ref[...]
        o_ref[pl.ds(base, CONV_SUB), :] = (yn * jax.nn.sigmoid(yn)).astype(BF16)
        return carry

    lax.fori_loop(0, tm // CONV_SUB, sub, 0)


def _conv(u, w, b, lg, lb, seq_len, *, tm=512):
    m = u.shape[0]
    nblk = seq_len // tm
    hb = tm // CONV_HALO
    nh = m // CONV_HALO
    return pl.pallas_call(
        functools.partial(_conv_kernel, tm=tm, nblk=nblk),
        grid=(m // tm,),
        in_specs=[
            pl.BlockSpec((CONV_HALO, CONV_CH), lambda i: (jnp.maximum(i * hb - 1, 0), 0)),
            pl.BlockSpec((tm, CONV_CH), lambda i: (i, 0)),
            pl.BlockSpec((CONV_HALO, CONV_CH), lambda i: (jnp.minimum((i + 1) * hb, nh - 1), 0)),
            _full((CONV_WIDTH, CONV_CH)), _full((1, CONV_CH)), _full((1, CONV_CH)), _full((1, CONV_CH)),
        ],
        out_specs=pl.BlockSpec((tm, CONV_CH), lambda i: (i, 0)),
        out_shape=jax.ShapeDtypeStruct((m, CONV_CH), BF16),
        scratch_shapes=[pltpu.VMEM((tm + 2 * CONV_HALO, CONV_CH), F32)],
        compiler_params=_cparams("parallel"),
        name="conv",
    )(u, u, u, w, b, lg, lb)


S5_W = 2 * S5_GROUP * S5_CHUNK
S5_X = 2 * S5_STATE


def _s5_kernel(u_ref, m_ref, b_ref, c_ref, pw_ref, y_ref, *, nchunk, nseq):
    u = u_ref[0]
    nc = nseq * nchunk
    s = _dot(u, b_ref[0])
    row = lax.broadcasted_iota(jnp.int32, (nc, S5_X), 0) & (nchunk - 1)
    nstep = nchunk.bit_length() - 1
    states = []
    for d in range(2):
        xr = s[:, (2 * d) * S5_X:(2 * d + 1) * S5_X]
        xi = s[:, (2 * d + 1) * S5_X:(2 * d + 2) * S5_X]

        def shifted(x, sh):
            if d == 0:
                return jnp.where(row >= sh, pltpu.roll(x, sh, 0), 0.0)
            return jnp.where(row < nchunk - sh, pltpu.roll(x, nc - sh, 0), 0.0)

        for k in range(nstep):
            ar = pw_ref[0, d, k, 0:1, :]
            ai = pw_ref[0, d, k, 1:2, :]
            sr = shifted(xr, 1 << k)
            si = shifted(xi, 1 << k)
            xr, xi = xr + ar * sr - ai * si, xi + ar * si + ai * sr
        states += [shifted(xr, 1), shifted(xi, 1)]
    x = jnp.concatenate(states, axis=1).astype(BF16)
    y_ref[0] = _dot(u, m_ref[0]) + _dot(x, c_ref[0])


def _s5_scan(u_p, ops, nseq, nchunk):
    npair, nc, w = u_p.shape
    assert nchunk & (nchunk - 1) == 0 and nc == nseq * nchunk
    nstep = max(nchunk.bit_length() - 1, 1)
    grp = lambda *shape: pl.BlockSpec((1,) + shape, lambda i: (i,) + (0,) * len(shape))
    return pl.pallas_call(
        functools.partial(_s5_kernel, nchunk=nchunk, nseq=nseq),
        grid=(npair,),
        in_specs=[grp(nc, w), grp(w, w), grp(w, 4 * S5_X), grp(4 * S5_X, w), grp(2, nstep, 2, S5_X)],
        out_specs=grp(nc, w),
        out_shape=jax.ShapeDtypeStruct((npair, nc, w), F32),
        compiler_params=_cparams("parallel"),
        name="s5",
    )(u_p, ops["m"], ops["b"], ops["c"], ops["pw"][nchunk])


def _od_out_kernel(h_ref, c_ref, su_ref, ys_ref, d_ref, wglu_ref, wc_ref, ws_ref, g_ref, o_ref):
    y = d_ref[...] * su_ref[...] + ys_ref[...]
    z = jax.nn.gelu(y, approximate=True)
    sg = (z * jax.nn.sigmoid(_dot(z.astype(BF16), wglu_ref[...]))).astype(BF16)
    mix = _dot(c_ref[...], wc_ref[...]) + _dot(sg, ws_ref[...])
    o_ref[...] = h_ref[...] + _rms(mix, g_ref[...])


def _od_out(h, c, su, ys, d, wglu, wc, ws, g, *, tm=512):
    m = h.shape[0]
    tok = lambda w: pl.BlockSpec((tm, w), lambda i: (i, 0))
    return pl.pallas_call(
        _od_out_kernel,
        grid=(m // tm,),
        in_specs=[tok(D_MODEL), tok(CONV_CH), tok(S5_CH), tok(S5_CH), _full((1, S5_CH)),
                  _full(wglu.shape), _full(wc.shape), _full(ws.shape), _full((1, D_MODEL))],
        out_specs=tok(D_MODEL),
        out_shape=jax.ShapeDtypeStruct((m, D_MODEL), F32),
        compiler_params=_cparams("parallel"),
        name="od_out",
    )(h, c, su, ys, d, wglu, wc, ws, g)


def _head_rows(w, widths, offsets, rows_per_head):
    k = w.shape[0]
    per = w.shape[1] // MLA_HEADS
    w = w.reshape(k, MLA_HEADS, per)
    out = jnp.zeros((k, MLA_HEADS, rows_per_head), w.dtype)
    src = 0
    for wd, off in zip(widths, offsets):
        if off is not None:
            out = out.at[:, :, off:off + wd].set(w[:, :, src:src + wd])
        src += wd
    return out.reshape(k, MLA_HEADS * rows_per_head).T


def _rot_cols(w):
    half = w.shape[-1] // 2
    return jnp.concatenate([-w[..., half:], w[..., :half]], axis=-1)


def _rope_tables(seq_len):
    half = MLA_ROPE // 2
    inv = ROPE_THETA ** (-jnp.arange(half, dtype=F32) / half)
    ang = jnp.arange(seq_len, dtype=F32)[:, None] * inv[None, :]
    cos = jnp.concatenate([jnp.cos(ang)] * 2, axis=1)
    sin = jnp.concatenate([jnp.sin(ang)] * 2, axis=1)
    scale = (MLA_NOPE + MLA_ROPE) ** -0.5 * math.log2(math.e)
    z64 = jnp.zeros((seq_len, MLA_NOPE), F32)
    z32 = jnp.zeros((seq_len, HEAD_PAD - MLA_NOPE - MLA_ROPE), F32)
    cq = jnp.concatenate([z64 + scale, cos * scale, z32], axis=1)
    sq = jnp.concatenate([z64, sin * scale, z32], axis=1)
    ck = jnp.concatenate([z64, cos, z32], axis=1)
    sk = jnp.concatenate([z64, sin, z32], axis=1)
    return jnp.concatenate([ck, sk], axis=1), jnp.concatenate([cq, sq], axis=1).T


def _nat_bias(rpb, kh):
    c = jnp.arange(GRID_W)
    col_start = jnp.clip(c - NAT_KW // 2, 0, GRID_W - NAT_KW)
    col_ok = (c[None, :] >= col_start[:, None]) & (c[None, :] < col_start[:, None] + NAT_KW)
    col_off = jnp.clip(c[None, :] - c[:, None], -(NAT_KW - 1), NAT_KW - 1) + (NAT_KW - 1)
    delta = jnp.arange(kh)
    row_off = jnp.arange(kh)[None, :] - delta[:, None] + (NAT_KH_MAX - 1)
    row_sel = (row_off[:, :, None] == jnp.arange(2 * NAT_KH_MAX - 1)).astype(F32)
    col_sel = (col_off[:, :, None] == jnp.arange(2 * NAT_KW - 1)).astype(F32)
    bias = jnp.einsum("hrc,djr,qkc->dhqjk", rpb.astype(F32), row_sel, col_sel, precision=lax.Precision.HIGHEST)
    bias = jnp.where(col_ok[None, None, :, None, :], bias, NEG_INF)
    return bias.reshape(kh, NAT_HEADS // 2, 2 * GRID_W, kh * GRID_W)


def _pair_diag(x, spec, rows, cols):
    x = x.reshape((S5_PAIRS, 2) + x.shape[1:])
    return jnp.einsum(spec, x, jnp.eye(2, dtype=x.dtype)).reshape(S5_PAIRS, rows, cols)


def _s5_operators(lam_re, lam_im, log_step, b_re, b_im, c_re, c_im, nchunks):
    t = S5_CHUNK
    w = S5_GROUP * t
    dt = jnp.exp(log_step)[:, :, None]
    ar, ai = lam_re * dt, lam_im * dt
    er = jnp.exp(ar)
    lbr, lbi = er * jnp.cos(ai), er * jnp.sin(ai)
    den = lam_re * lam_re + lam_im * lam_im
    fr = ((lbr - 1.0) * lam_re + lbi * lam_im) / den
    fi = (lbi * lam_re - (lbr - 1.0) * lam_im) / den
    bbr = fr[..., None] * b_re - fi[..., None] * b_im
    bbi = fr[..., None] * b_im + fi[..., None] * b_re

    def power(d):
        d = d.astype(F32)[None, None, :, None]
        mag = jnp.exp(ar[:, :, None, :] * d)
        return mag * jnp.cos(ai[:, :, None, :] * d), mag * jnp.sin(ai[:, :, None, :] * d)

    hi = lax.Precision.HIGHEST
    pr, pi = power(jnp.arange(t + 1))
    wr = c_re[:, :, None] * pr[:, :, :, None, :] - c_im[:, :, None] * pi[:, :, :, None, :]
    wi = c_re[:, :, None] * pi[:, :, :, None, :] + c_im[:, :, None] * pr[:, :, :, None, :]
    kmat = (jnp.einsum("xgdcp,xgpk->xgdck", wr[:, :, :t], bbr, precision=hi)
            - jnp.einsum("xgdcp,xgpk->xgdck", wi[:, :, :t], bbi, precision=hi))
    kf, kr = kmat[0], kmat[1]
    kcat = jnp.concatenate([kr[:, :0:-1], kf[:, :1] + kr[:, :1], kf[:, 1:]], axis=1)
    idx = jnp.arange(t)[None, :] - jnp.arange(t)[:, None] + (t - 1)
    mm = kcat[:, idx]
    mmat = _pair_diag(mm, "pastck,ab->psaktbc", 2 * w, 2 * w)

    def contrib(pr_, pi_, br_, bi_):
        brt, bit = br_.transpose(0, 2, 1)[:, None], bi_.transpose(0, 2, 1)[:, None]
        re = pr_[:, :, None, :] * brt - pi_[:, :, None, :] * bit
        im = pr_[:, :, None, :] * bit + pi_[:, :, None, :] * brt
        return [_pair_diag(x, "paxys,ab->pxaybs", 2 * w, S5_X) for x in (re, im)]

    bmat = jnp.concatenate(
        contrib(pr[0, :, t - 1::-1], pi[0, :, t - 1::-1], bbr[0], bbi[0])
        + contrib(pr[1, :, :t], pi[1, :, :t], bbr[1], bbi[1]), axis=2)

    def readout(w_):
        return _pair_diag(w_, "patcs,ab->pastbc", S5_X, 2 * w)

    cmat = jnp.concatenate([
        readout(wr[0, :, 1:t + 1]), readout(-wi[0, :, 1:t + 1]),
        readout(wr[1, :, t:0:-1]), readout(-wi[1, :, t:0:-1]),
    ], axis=1)

    pws = {}
    for nchunk in nchunks:
        nstep = max(nchunk.bit_length() - 1, 1)
        qr, qi = power(t * (2 ** jnp.arange(nstep)))

        def lanes(q):
            q = q.reshape(2, S5_PAIRS, 2, nstep, S5_STATE).transpose(1, 0, 3, 2, 4)
            return q.reshape(S5_PAIRS, 2, nstep, S5_X)

        pws[nchunk] = jnp.stack([lanes(qr), lanes(qi)], axis=3)
    return {"m": mmat.astype(BF16), "b": bmat.astype(BF16), "c": cmat.astype(BF16), "pw": pws}


def _even_params(ev_w_in, q_norm, kv_norm, w_uq, w_ukv, rpb, ev_w_out, seq_lens):
    c0 = MLA_Q_RANK
    c1 = c0 + MLA_KV_RANK
    c2 = c1 + MLA_ROPE
    w_kr = ev_w_in[:, c1:c2]
    pad_l = jnp.zeros((D_MODEL, MLA_NOPE), F32)
    pad_r = jnp.zeros((D_MODEL, HEAD_PAD - MLA_NOPE - MLA_ROPE), F32)
    w_in = jnp.concatenate([
        ev_w_in[:, :c1],
        pad_l, w_kr, pad_r,
        pad_l, _rot_cols(w_kr), pad_r,
        ev_w_in[:, c2:],
    ], axis=1).astype(BF16)
    uq = w_uq.reshape(MLA_Q_RANK, MLA_HEADS, MLA_NOPE + MLA_ROPE)
    uq_rot = jnp.concatenate([jnp.zeros_like(uq[..., :MLA_NOPE]), _rot_cols(uq[..., MLA_NOPE:])], axis=-1)
    nope_rope = ((MLA_NOPE, MLA_ROPE), (0, MLA_NOPE))
    v_one = jnp.zeros((MLA_HEADS, V_ROWS), F32).at[:, MLA_V].set(1.0).reshape(MLA_HEADS * V_ROWS, 1)
    nat_bias = {kh: _nat_bias(rpb, kh) for kh in {min(NAT_KH_MAX, sl // GRID_W) for sl in seq_lens}}
    wa = jnp.zeros((MLA_HEADS, HEAD_PAD, D_MODEL), F32)
    wa = wa.at[:, :MLA_V].set(ev_w_out[:MLA_HEADS * MLA_V].reshape(MLA_HEADS, MLA_V, D_MODEL))
    return {
        "w_in": w_in,
        "q_norm": q_norm[None], "kv_norm": kv_norm[None],
        "w_uq_t": _head_rows(w_uq, *nope_rope, HEAD_PAD).astype(BF16),
        "w_uq_rot_t": _head_rows(uq_rot.reshape(MLA_Q_RANK, -1), *nope_rope, HEAD_PAD).astype(BF16),
        "w_uk": _head_rows(w_ukv, (MLA_NOPE, MLA_V), (0, None), HEAD_PAD).T.astype(BF16),
        "w_uv_t": _head_rows(w_ukv, (MLA_NOPE, MLA_V), (None, 0), V_ROWS).astype(BF16),
        "v_one": v_one,
        "rope_tab": {sl: _rope_tables(sl) for sl in seq_lens},
        "nat_bias": {sl: nat_bias[min(NAT_KH_MAX, sl // GRID_W)] for sl in seq_lens},
        "wa": wa.reshape(HP_ALL, D_MODEL).astype(BF16),
        "wb": ev_w_out[MLA_HEADS * MLA_V:].astype(BF16),
    }


def _even_mixer(h, g_pre, g_post, p, batch, seq_len):
    qt, k, vt, nq, nk, nv = _ev_in(h, g_pre, p, seq_len)
    a = _mla(qt, k, vt, batch, seq_len)
    b = _nat(nq, nk, nv, p["nat_bias"][seq_len], batch, seq_len)
    return _ev_out(h, a, b, p["wa"], p["wb"], g_post)


def _odd_mixer(h, g_pre, g_post, p, batch, seq_len):
    u, su = _od_in(h, g_pre, p["w_in"])
    c = _conv(u, p["dw_w"], p["dw_b"], p["ln_g"], p["ln_b"], seq_len)
    nchunk = seq_len // S5_CHUNK
    nc = batch * nchunk
    u_p = su.astype(BF16).reshape(nc, S5_CHUNK, S5_PAIRS, 2 * S5_GROUP)
    u_p = u_p.transpose(2, 0, 1, 3).reshape(S5_PAIRS, nc, S5_W)
    y_p = _s5_scan(u_p, p["s5"], batch, nchunk)
    ys = y_p.reshape(S5_PAIRS, nc, S5_CHUNK, 2 * S5_GROUP).transpose(1, 2, 0, 3)
    ys = ys.reshape(batch * seq_len, S5_CH)
    return _od_out(h, c, su, ys, p["d"], p["w_glu"], p["wc"], p["ws"], g_post)


def kernel(x_prompt, x_sample, norm_g, ffn_w_gate, ffn_w_up, ffn_w_down, ev_w_in, mla_q_norm, mla_kv_norm, mla_w_uq, mla_w_ukv, nat_rpb, ev_w_out, od_w_in, conv_dw_w, conv_dw_b, conv_ln_g, conv_ln_b, s5_lambda_re, s5_lambda_im, s5_log_step, s5_b_re, s5_b_im, s5_c_re, s5_c_im, s5_d, s5_w_glu, od_w_out):
    depth = norm_g.shape[0]
    seq_lens = sorted({x_prompt.shape[1], x_sample.shape[1]})
    wg = ffn_w_gate.astype(BF16)
    wu = ffn_w_up.astype(BF16)
    wd = ffn_w_down.astype(BF16)
    mixers = []
    for layer in range(depth):
        i = layer // 2
        if layer % 2 == 0:
            mixers.append(_even_params(ev_w_in[i], mla_q_norm[i], mla_kv_norm[i], mla_w_uq[i], mla_w_ukv[i],
                                       nat_rpb[i], ev_w_out[i], seq_lens))
        else:
            mixers.append({
                "w_in": od_w_in[i].astype(BF16),
                "dw_w": conv_dw_w[i], "dw_b": conv_dw_b[i][None],
                "ln_g": conv_ln_g[i][None], "ln_b": conv_ln_b[i][None],
                "s5": _s5_operators(s5_lambda_re[i], s5_lambda_im[i], s5_log_step[i], s5_b_re[i], s5_b_im[i],
                                    s5_c_re[i], s5_c_im[i], [sl // S5_CHUNK for sl in seq_lens]),
                "d": s5_d[i][None],
                "w_glu": s5_w_glu[i].astype(BF16),
                "wc": od_w_out[i][:CONV_CH].astype(BF16),
                "ws": od_w_out[i][CONV_CH:].astype(BF16),
            })

    def trunk(x):
        batch, seq_len, _ = x.shape
        h = x.reshape(batch * seq_len, D_MODEL)
        for layer in range(depth):
            g = norm_g[layer][:, None, :]
            h = _ffn(h, g[0], wg[layer, 0], wu[layer, 0], wd[layer, 0], g[1])
            mixer = _even_mixer if layer % 2 == 0 else _odd_mixer
            h = mixer(h, g[2], g[3], mixers[layer], batch, seq_len)
            h = _ffn(h, g[4], wg[layer, 1], wu[layer, 1], wd[layer, 1], g[5])
        return h.reshape(batch, seq_len, D_MODEL)

    return (trunk(x_prompt), trunk(x_sample))
```

```python
import functools
import math

import jax
import jax.numpy as jnp
from jax import lax
from jax.experimental import pallas as pl
from jax.experimental.pallas import tpu as pltpu

F32 = jnp.float32
BF16 = jnp.bfloat16

LANES = 128
SUBLANES = 8

D_MODEL = 1024
D_FF = 2816
GRID_W = 64

MLA_HEADS = 8
MLA_Q_RANK = 256
MLA_KV_RANK = 128
MLA_NOPE = 64
MLA_ROPE = 32
MLA_V = 64
ROPE_THETA = 10000.0
HEAD_PAD = 128
V_ROWS = 80

NAT_HEADS = 8
NAT_HEAD_DIM = 64
NAT_W = NAT_HEADS * NAT_HEAD_DIM
NAT_KH_MAX = 8
NAT_KW = 16

CONV_CH = 512
CONV_WIDTH = 31
CONV_HALO = 16

S5_CH = 512
S5_GROUP = 16
S5_GROUPS = S5_CH // S5_GROUP
S5_STATE = 64
S5_CHUNK = 16
S5_PAIRS = S5_GROUPS // 2

FFN_RES_SCALE = 0.5
NORM_EPS = 1e-6
NEG_INF = -1e30

VMEM_LIMIT = 48 * 1024 * 1024
FFN_VMEM_LIMIT = 56 * 1024 * 1024


def _cparams(*sem):
    return pltpu.CompilerParams(dimension_semantics=sem, vmem_limit_bytes=VMEM_LIMIT)


def _rms(x, g):
    return x * lax.rsqrt(jnp.mean(x * x, axis=-1, keepdims=True) + NORM_EPS) * g


def _dot(a, b):
    return jnp.dot(a, b, preferred_element_type=F32)


def _dot_nt(a, b):
    return lax.dot_general(a, b, (((1,), (1,)), ((), ())), preferred_element_type=F32)


def _full(shape):
    n = len(shape)
    return pl.BlockSpec(shape, lambda *_: (0,) * n)


def _ffn_kernel(x_ref, gpre_ref, wg_ref, wu_ref, wd_ref, gpost_ref, o_ref):
    x = x_ref[...]
    xn = _rms(x, gpre_ref[...]).astype(BF16)
    gate = _dot(xn, wg_ref[...])
    up = _dot(xn, wu_ref[...])
    hdn = (gate * jax.nn.sigmoid(gate) * up).astype(BF16)
    o_ref[...] = x + FFN_RES_SCALE * _rms(_dot(hdn, wd_ref[...]), gpost_ref[...])


def _ffn(h, gpre, wg, wu, wd, gpost, *, tm=512):
    m = h.shape[0]
    resident = lambda shape: pl.BlockSpec(shape, lambda i: (0, 0), pipeline_mode=pl.Buffered(1))
    return pl.pallas_call(
        _ffn_kernel,
        grid=(m // tm,),
        in_specs=[
            pl.BlockSpec((tm, D_MODEL), lambda i: (i, 0)),
            _full((1, D_MODEL)),
            resident((D_MODEL, D_FF)), resident((D_MODEL, D_FF)), resident((D_FF, D_MODEL)),
            _full((1, D_MODEL)),
        ],
        out_specs=pl.BlockSpec((tm, D_MODEL), lambda i: (i, 0)),
        out_shape=jax.ShapeDtypeStruct((m, D_MODEL), F32),
        compiler_params=pltpu.CompilerParams(dimension_semantics=("parallel",), vmem_limit_bytes=FFN_VMEM_LIMIT),
        name="ffn",
    )(h, gpre, wg, wu, wd, gpost)


EV_Z = MLA_Q_RANK + MLA_KV_RANK + 2 * HEAD_PAD + 3 * NAT_W
HP_ALL = MLA_HEADS * HEAD_PAD


def _ev_in_kernel(h_ref, g_ref, win_ref, qn_ref, kvn_ref, wuq_ref, wuqr_ref, wuk_ref, wuv_ref,
                  tab_ref, tabt_ref, vone_ref, qt_ref, k_ref, vt_ref, nq_ref, nk_ref, nv_ref):
    m = _rms(h_ref[...], g_ref[...]).astype(BF16)
    z = _dot(m, win_ref[...])
    c0 = MLA_Q_RANK
    c1 = c0 + MLA_KV_RANK
    c2 = c1 + HEAD_PAD
    c3 = c2 + HEAD_PAD
    q_lat = z[:, :c0]
    kv_lat = z[:, c0:c1]
    kr = z[:, c1:c2]
    kr_rot = z[:, c2:c3]
    nq_ref[...] = (z[:, c3:c3 + NAT_W] * (NAT_HEAD_DIM ** -0.5)).astype(BF16)
    nk_ref[...] = z[:, c3 + NAT_W:c3 + 2 * NAT_W].astype(BF16)
    nv_ref[...] = z[:, c3 + 2 * NAT_W:c3 + 3 * NAT_W].astype(BF16)

    qn = _rms(q_lat, qn_ref[...]).astype(BF16)
    kvn = _rms(kv_lat, kvn_ref[...]).astype(BF16)
    q_raw_t = _dot_nt(wuq_ref[...], qn)
    q_rot_t = _dot_nt(wuqr_ref[...], qn)
    v_t = _dot_nt(wuv_ref[...], kvn) + vone_ref[...]
    k_nope = _dot(kvn, wuk_ref[...])

    tab = tab_ref[...]
    k_rope = kr * tab[:, :HEAD_PAD] + kr_rot * tab[:, HEAD_PAD:]
    cq_t = tabt_ref[:HEAD_PAD, :]
    sq_t = tabt_ref[HEAD_PAD:, :]
    for hd in range(MLA_HEADS):
        sl = slice(hd * HEAD_PAD, (hd + 1) * HEAD_PAD)
        qt_ref[hd] = (q_raw_t[sl] * cq_t + q_rot_t[sl] * sq_t).astype(BF16)
        k_ref[:, sl] = (k_nope[:, sl] + k_rope).astype(BF16)
        vt_ref[hd] = v_t[hd * V_ROWS:(hd + 1) * V_ROWS].astype(BF16)


def _ev_in(h, g, p, seq_len, *, tm=512):
    m = h.shape[0]
    nblk = seq_len // tm
    tok = lambda w: pl.BlockSpec((tm, w), lambda i: (i, 0))
    feat = lambda r: pl.BlockSpec((MLA_HEADS, r, tm), lambda i: (0, 0, i))
    outs = [jax.ShapeDtypeStruct((MLA_HEADS, HEAD_PAD, m), BF16), jax.ShapeDtypeStruct((m, HP_ALL), BF16),
            jax.ShapeDtypeStruct((MLA_HEADS, V_ROWS, m), BF16)] + [jax.ShapeDtypeStruct((m, NAT_W), BF16)] * 3
    tab, tab_t = p["rope_tab"][seq_len]
    return pl.pallas_call(
        _ev_in_kernel,
        grid=(m // tm,),
        in_specs=[
            tok(D_MODEL), _full((1, D_MODEL)), _full((D_MODEL, EV_Z)),
            _full((1, MLA_Q_RANK)), _full((1, MLA_KV_RANK)),
            _full((HP_ALL, MLA_Q_RANK)), _full((HP_ALL, MLA_Q_RANK)),
            _full((MLA_KV_RANK, HP_ALL)), _full((MLA_HEADS * V_ROWS, MLA_KV_RANK)),
            pl.BlockSpec((tm, 2 * HEAD_PAD), lambda i: (i % nblk, 0)),
            pl.BlockSpec((2 * HEAD_PAD, tm), lambda i: (0, i % nblk)),
            _full((MLA_HEADS * V_ROWS, 1)),
        ],
        out_specs=[feat(HEAD_PAD), tok(HP_ALL), feat(V_ROWS)] + [tok(NAT_W)] * 3,
        out_shape=outs,
        compiler_params=_cparams("parallel"),
        name="ev_in",
    )(h, g, p["w_in"], p["q_norm"], p["kv_norm"], p["w_uq_t"], p["w_uq_rot_t"], p["w_uk"], p["w_uv_t"],
      tab, tab_t, p["v_one"])


def _mla_kernel(qt_ref, k_ref, vt_ref, o_ref, s_scr, *, tk, nk, unroll):
    qt = qt_ref[0]
    tq = qt.shape[1]

    def scores(j):
        off = pl.multiple_of(j * tk, tk)
        return _dot(k_ref[pl.ds(off, tk), :], qt)

    s_scr[0] = scores(0)

    def body(jj, carry):
        m_prev, acc = carry
        for i in range(unroll):
            j = jj * unroll + i
            s_scr[(i + 1) % 2] = scores(jnp.minimum(j + 1, nk - 1))
            st = s_scr[i % 2]
            m_new = jnp.maximum(m_prev, jnp.max(st, axis=0, keepdims=True))
            alpha = jnp.exp2(m_prev - m_new)
            pt = jnp.exp2(st - m_new).astype(BF16)
            off = pl.multiple_of(j * tk, tk)
            acc = alpha * acc + _dot(vt_ref[0, :, pl.ds(off, tk)], pt)
            m_prev = m_new
        return m_prev, acc

    m_init = jnp.full((1, tq), jnp.finfo(F32).min, F32)
    acc = jnp.zeros((V_ROWS, tq), F32)
    _, acc = lax.fori_loop(0, nk // unroll, body, (m_init, acc))
    out_t = acc[:MLA_V] / acc[MLA_V:MLA_V + 1]
    out_t = jnp.concatenate([out_t, jnp.zeros((HEAD_PAD - MLA_V, tq), F32)], axis=0)
    o_ref[...] = out_t.T.astype(BF16)


def _mla(qt, k, vt, batch, seq_len, *, tq=512, tk=512, unroll=8):
    m = k.shape[0]
    tk = min(tk, seq_len // unroll)
    nq = seq_len // tq
    nk = seq_len // tk
    assert unroll % 2 == 0 and nk % unroll == 0
    return pl.pallas_call(
        functools.partial(_mla_kernel, tk=tk, nk=nk, unroll=unroll),
        grid=(batch, MLA_HEADS, nq),
        in_specs=[
            pl.BlockSpec((1, HEAD_PAD, tq), lambda b, h, i: (h, 0, b * nq + i)),
            pl.BlockSpec((seq_len, HEAD_PAD), lambda b, h, i: (b, h)),
            pl.BlockSpec((1, V_ROWS, seq_len), lambda b, h, i: (h, 0, b)),
        ],
        out_specs=pl.BlockSpec((tq, HEAD_PAD), lambda b, h, i: (b * nq + i, h)),
        out_shape=jax.ShapeDtypeStruct((m, HP_ALL), BF16),
        scratch_shapes=[pltpu.VMEM((2, tk, tq), F32)],
        compiler_params=_cparams("parallel", "parallel", "arbitrary"),
        name="mla",
    )(qt, k, vt)


NAT_LANES = 2 * NAT_HEAD_DIM


def _nat_kernel(q_ref, k_ref, v_ref, bias_ref, o_ref, *, rows, kh, rblk):
    i = pl.program_id(2)
    lane = lax.broadcasted_iota(jnp.int32, (GRID_W, NAT_LANES), 1)
    head0 = lane < NAT_HEAD_DIM

    def row_body(rr, carry):
        r = i * rblk + rr
        start = jnp.clip(r - kh // 2, 0, rows - kh)
        delta = r - start
        koff = pl.multiple_of(start * GRID_W, GRID_W)
        qoff = pl.multiple_of(rr * GRID_W, GRID_W)
        qrow = q_ref[pl.ds(qoff, GRID_W), :]
        kwin = k_ref[pl.ds(koff, kh * GRID_W), :]
        vwin = v_ref[pl.ds(koff, kh * GRID_W), :]
        zero = jnp.zeros_like(qrow)
        q2 = jnp.concatenate([jnp.where(head0, qrow, zero), jnp.where(head0, zero, qrow)], axis=0)
        st = _dot_nt(kwin, q2) + bias_ref[delta, 0]
        e = jnp.exp(st - jnp.max(st, axis=0, keepdims=True))
        pt = (e / jnp.sum(e, axis=0, keepdims=True)).astype(BF16)
        o2 = lax.dot_general(pt, vwin, (((0,), (0,)), ((), ())), preferred_element_type=F32)
        o_ref[pl.ds(qoff, GRID_W), :] = jnp.where(head0, o2[:GRID_W], o2[GRID_W:]).astype(BF16)
        return carry

    lax.fori_loop(0, rblk, row_body, 0, unroll=True)


def _nat(nq, nk, nv, bias, batch, seq_len, *, rblk=8):
    m = nq.shape[0]
    rows = seq_len // GRID_W
    kh = min(NAT_KH_MAX, rows)
    nblk = rows // rblk
    tq = rblk * GRID_W
    return pl.pallas_call(
        functools.partial(_nat_kernel, rows=rows, kh=kh, rblk=rblk),
        grid=(batch, NAT_HEADS // 2, nblk),
        in_specs=[
            pl.BlockSpec((tq, NAT_LANES), lambda b, hp, i: (b * nblk + i, hp)),
            pl.BlockSpec((seq_len, NAT_LANES), lambda b, hp, i: (b, hp)),
            pl.BlockSpec((seq_len, NAT_LANES), lambda b, hp, i: (b, hp)),
            pl.BlockSpec((kh, 1, kh * GRID_W, 2 * GRID_W), lambda b, hp, i: (0, hp, 0, 0)),
        ],
        out_specs=pl.BlockSpec((tq, NAT_LANES), lambda b, hp, i: (b * nblk + i, hp)),
        out_shape=jax.ShapeDtypeStruct((m, NAT_W), BF16),
        compiler_params=_cparams("parallel", "parallel", "arbitrary"),
        name="nat",
    )(nq, nk, nv, bias)


def _mix_out_kernel(h_ref, a_ref, b_ref, wa_ref, wb_ref, g_ref, o_ref):
    mix = _dot(a_ref[...], wa_ref[...]) + _dot(b_ref[...], wb_ref[...])
    o_ref[...] = h_ref[...] + _rms(mix, g_ref[...])


def _ev_out(h, a, b, wa, wb, g, *, tm=512):
    m = h.shape[0]
    tok = lambda w: pl.BlockSpec((tm, w), lambda i: (i, 0))
    return pl.pallas_call(
        _mix_out_kernel,
        grid=(m // tm,),
        in_specs=[tok(D_MODEL), tok(a.shape[1]), tok(b.shape[1]),
                  _full(wa.shape), _full(wb.shape), _full((1, D_MODEL))],
        out_specs=tok(D_MODEL),
        out_shape=jax.ShapeDtypeStruct((m, D_MODEL), F32),
        compiler_params=_cparams("parallel"),
        name="ev_out",
    )(h, a, b, wa, wb, g)


def _od_in_kernel(h_ref, g_ref, win_ref, u_ref, su_ref):
    m = _rms(h_ref[...], g_ref[...]).astype(BF16)
    z = _dot(m, win_ref[...])
    ca = z[:, :CONV_CH]
    cg = z[:, CONV_CH:2 * CONV_CH]
    u_ref[...] = ca * jax.nn.sigmoid(cg)
    su_ref[...] = z[:, 2 * CONV_CH:]


def _od_in(h, g, w_in, *, tm=512):
    m = h.shape[0]
    tok = lambda w: pl.BlockSpec((tm, w), lambda i: (i, 0))
    return pl.pallas_call(
        _od_in_kernel,
        grid=(m // tm,),
        in_specs=[tok(D_MODEL), _full((1, D_MODEL)), _full(w_in.shape)],
        out_specs=[tok(CONV_CH), tok(S5_CH)],
        out_shape=[jax.ShapeDtypeStruct((m, CONV_CH), F32), jax.ShapeDtypeStruct((m, S5_CH), F32)],
        compiler_params=_cparams("parallel"),
        name="od_in",
    )(h, g, w_in)


CONV_SUB = 64


def _conv_kernel(prev_ref, cur_ref, next_ref, w_ref, b_ref, lg_ref, lb_ref, o_ref, scr, *, tm, nblk):
    i = pl.program_id(0)
    first = (i % nblk) == 0
    last = (i % nblk) == nblk - 1
    scr[0:CONV_HALO, :] = jnp.where(first, 0.0, prev_ref[...])
    scr[CONV_HALO:CONV_HALO + tm, :] = cur_ref[...]
    scr[CONV_HALO + tm:, :] = jnp.where(last, 0.0, next_ref[...])
    w = w_ref[...]
    shift = CONV_HALO - CONV_WIDTH // 2

    def sub(c, carry):
        base = pl.multiple_of(c * CONV_SUB, CONV_SUB)
        cols = []
        for lb in range(CONV_CH // LANES):
            ls = slice(lb * LANES, (lb + 1) * LANES)
            win = scr[pl.ds(base, CONV_SUB + 2 * CONV_HALO), ls]
            acc = jnp.zeros((CONV_SUB, LANES), F32)
            nwin = CONV_SUB + 2 * CONV_HALO
            for b in range(SUBLANES):
                wb = pltpu.roll(win, nwin - b, 0) if b else win
                for a in range(2 * CONV_HALO // SUBLANES):
                    kk = SUBLANES * a + b - shift
                    if 0 <= kk < CONV_WIDTH:
                        acc = acc + wb[SUBLANES * a:SUBLANES * a + CONV_SUB, :] * w[kk:kk + 1, ls]
            cols.append(acc)
        y = jnp.concatenate(cols, axis=1) + b_ref[...]
        mu = jnp.mean(y, axis=-1, keepdims=True)
        yc = y - mu
        yn = yc * lax.rsqrt(jnp.mean(yc * yc, axis=-1, keepdims=True) + NORM_EPS)
        yn = yn * lg_ref[...] + lb_ref[...]
        o_ref[pl.ds(base, CONV_SUB), :] = (yn * jax.nn.sigmoid(yn)).astype(BF16)
        return carry

    lax.fori_loop(0, tm // CONV_SUB, sub, 0)


def _conv(u, w, b, lg, lb, seq_len, *, tm=512):
    m = u.shape[0]
    nblk = seq_len // tm
    hb = tm // CONV_HALO
    nh = m // CONV_HALO
    return pl.pallas_call(
        functools.partial(_conv_kernel, tm=tm, nblk=nblk),
        grid=(m // tm,),
        in_specs=[
            pl.BlockSpec((CONV_HALO, CONV_CH), lambda i: (jnp.maximum(i * hb - 1, 0), 0)),
            pl.BlockSpec((tm, CONV_CH), lambda i: (i, 0)),
            pl.BlockSpec((CONV_HALO, CONV_CH), lambda i: (jnp.minimum((i + 1) * hb, nh - 1), 0)),
            _full((CONV_WIDTH, CONV_CH)), _full((1, CONV_CH)), _full((1, CONV_CH)), _full((1, CONV_CH)),
        ],
        out_specs=pl.BlockSpec((tm, CONV_CH), lambda i: (i, 0)),
        out_shape=jax.ShapeDtypeStruct((m, CONV_CH), BF16),
        scratch_shapes=[pltpu.VMEM((tm + 2 * CONV_HALO, CONV_CH), F32)],
        compiler_params=_cparams("parallel"),
        name="conv",
    )(u, u, u, w, b, lg, lb)


S5_W = 2 * S5_GROUP * S5_CHUNK
S5_X = 2 * S5_STATE


def _s5_kernel(u_ref, m_ref, b_ref, c_ref, pw_ref, y_ref, *, nchunk, nseq):
    u = u_ref[0]
    nc = nseq * nchunk
    s = _dot(u, b_ref[0])
    row = lax.broadcasted_iota(jnp.int32, (nc, S5_X), 0) & (nchunk - 1)
    nstep = nchunk.bit_length() - 1
    states = []
    for d in range(2):
        xr = s[:, (2 * d) * S5_X:(2 * d + 1) * S5_X]
        xi = s[:, (2 * d + 1) * S5_X:(2 * d + 2) * S5_X]

        def shifted(x, sh):
            if d == 0:
                return jnp.where(row >= sh, pltpu.roll(x, sh, 0), 0.0)
            return jnp.where(row < nchunk - sh, pltpu.roll(x, nc - sh, 0), 0.0)

        for k in range(nstep):
            ar = pw_ref[0, d, k, 0:1, :]
            ai = pw_ref[0, d, k, 1:2, :]
            sr = shifted(xr, 1 << k)
            si = shifted(xi, 1 << k)
            xr, xi = xr + ar * sr - ai * si, xi + ar * si + ai * sr
        states += [shifted(xr, 1), shifted(xi, 1)]
    x = jnp.concatenate(states, axis=1).astype(BF16)
    y_ref[0] = _dot(u, m_ref[0]) + _dot(x, c_ref[0])


def _s5_scan(u_p, ops, nseq, nchunk):
    npair, nc, w = u_p.shape
    assert nchunk & (nchunk - 1) == 0 and nc == nseq * nchunk
    nstep = max(nchunk.bit_length() - 1, 1)
    grp = lambda *shape: pl.BlockSpec((1,) + shape, lambda i: (i,) + (0,) * len(shape))
    return pl.pallas_call(
        functools.partial(_s5_kernel, nchunk=nchunk, nseq=nseq),
        grid=(npair,),
        in_specs=[grp(nc, w), grp(w, w), grp(w, 4 * S5_X), grp(4 * S5_X, w), grp(2, nstep, 2, S5_X)],
        out_specs=grp(nc, w),
        out_shape=jax.ShapeDtypeStruct((npair, nc, w), F32),
        compiler_params=_cparams("parallel"),
        name="s5",
    )(u_p, ops["m"], ops["b"], ops["c"], ops["pw"][nchunk])


def _od_out_kernel(h_ref, c_ref, su_ref, ys_ref, d_ref, wglu_ref, wc_ref, ws_ref, g_ref, o_ref):
    y = d_ref[...] * su_ref[...] + ys_ref[...]
    z = jax.nn.gelu(y, approximate=True)
    sg = (z * jax.nn.sigmoid(_dot(z.astype(BF16), wglu_ref[...]))).astype(BF16)
    mix = _dot(c_ref[...], wc_ref[...]) + _dot(sg, ws_ref[...])
    o_ref[...] = h_ref[...] + _rms(mix, g_ref[...])


def _od_out(h, c, su, ys, d, wglu, wc, ws, g, *, tm=512):
    m = h.shape[0]
    tok = lambda w: pl.BlockSpec((tm, w), lambda i: (i, 0))
    return pl.pallas_call(
        _od_out_kernel,
        grid=(m // tm,),
        in_specs=[tok(D_MODEL), tok(CONV_CH), tok(S5_CH), tok(S5_CH), _full((1, S5_CH)),
                  _full(wglu.shape), _full(wc.shape), _full(ws.shape), _full((1, D_MODEL))],
        out_specs=tok(D_MODEL),
        out_shape=jax.ShapeDtypeStruct((m, D_MODEL), F32),
        compiler_params=_cparams("parallel"),
        name="od_out",
    )(h, c, su, ys, d, wglu, wc, ws, g)


def _head_rows(w, widths, offsets, rows_per_head):
    k = w.shape[0]
    per = w.shape[1] // MLA_HEADS
    w = w.reshape(k, MLA_HEADS, per)
    out = jnp.zeros((k, MLA_HEADS, rows_per_head), w.dtype)
    src = 0
    for wd, off in zip(widths, offsets):
        if off is not None:
            out = out.at[:, :, off:off + wd].set(w[:, :, src:src + wd])
        src += wd
    return out.reshape(k, MLA_HEADS * rows_per_head).T


def _rot_cols(w):
    half = w.shape[-1] // 2
    return jnp.concatenate([-w[..., half:], w[..., :half]], axis=-1)


def _rope_tables(seq_len):
    half = MLA_ROPE // 2
    inv = ROPE_THETA ** (-jnp.arange(half, dtype=F32) / half)
    ang = jnp.arange(seq_len, dtype=F32)[:, None] * inv[None, :]
    cos = jnp.concatenate([jnp.cos(ang)] * 2, axis=1)
    sin = jnp.concatenate([jnp.sin(ang)] * 2, axis=1)
    scale = (MLA_NOPE + MLA_ROPE) ** -0.5 * math.log2(math.e)
    z64 = jnp.zeros((seq_len, MLA_NOPE), F32)
    z32 = jnp.zeros((seq_len, HEAD_PAD - MLA_NOPE - MLA_ROPE), F32)
    cq = jnp.concatenate([z64 + scale, cos * scale, z32], axis=1)
    sq = jnp.concatenate([z64, sin * scale, z32], axis=1)
    ck = jnp.concatenate([z64, cos, z32], axis=1)
    sk = jnp.concatenate([z64, sin, z32], axis=1)
    return jnp.concatenate([ck, sk], axis=1), jnp.concatenate([cq, sq], axis=1).T


def _nat_bias(rpb, kh):
    c = jnp.arange(GRID_W)
    col_start = jnp.clip(c - NAT_KW // 2, 0, GRID_W - NAT_KW)
    col_ok = (c[None, :] >= col_start[:, None]) & (c[None, :] < col_start[:, None] + NAT_KW)
    col_off = jnp.clip(c[None, :] - c[:, None], -(NAT_KW - 1), NAT_KW - 1) + (NAT_KW - 1)
    delta = jnp.arange(kh)
    row_off = jnp.arange(kh)[None, :] - delta[:, None] + (NAT_KH_MAX - 1)
    row_sel = (row_off[:, :, None] == jnp.arange(2 * NAT_KH_MAX - 1)).astype(F32)
    col_sel = (col_off[:, :, None] == jnp.arange(2 * NAT_KW - 1)).astype(F32)
    bias = jnp.einsum("hrc,djr,qkc->dhqjk", rpb.astype(F32), row_sel, col_sel, precision=lax.Precision.HIGHEST)
    bias = jnp.where(col_ok[None, None, :, None, :], bias, NEG_INF)
    bias = bias.reshape(kh, NAT_HEADS // 2, 2 * GRID_W, kh * GRID_W)
    return bias.transpose(0, 1, 3, 2)


def _pair_diag(x, spec, rows, cols):
    x = x.reshape((S5_PAIRS, 2) + x.shape[1:])
    return jnp.einsum(spec, x, jnp.eye(2, dtype=x.dtype)).reshape(S5_PAIRS, rows, cols)


def _s5_operators(lam_re, lam_im, log_step, b_re, b_im, c_re, c_im, nchunks):
    t = S5_CHUNK
    w = S5_GROUP * t
    dt = jnp.exp(log_step)[:, :, None]
    ar, ai = lam_re * dt, lam_im * dt
    er = jnp.exp(ar)
    lbr, lbi = er * jnp.cos(ai), er * jnp.sin(ai)
    den = lam_re * lam_re + lam_im * lam_im
    fr = ((lbr - 1.0) * lam_re + lbi * lam_im) / den
    fi = (lbi * lam_re - (lbr - 1.0) * lam_im) / den
    bbr = fr[..., None] * b_re - fi[..., None] * b_im
    bbi = fr[..., None] * b_im + fi[..., None] * b_re

    def power(d):
        d = d.astype(F32)[None, None, :, None]
        mag = jnp.exp(ar[:, :, None, :] * d)
        return mag * jnp.cos(ai[:, :, None, :] * d), mag * jnp.sin(ai[:, :, None, :] * d)

    hi = lax.Precision.HIGHEST
    pr, pi = power(jnp.arange(t + 1))
    wr = c_re[:, :, None] * pr[:, :, :, None, :] - c_im[:, :, None] * pi[:, :, :, None, :]
    wi = c_re[:, :, None] * pi[:, :, :, None, :] + c_im[:, :, None] * pr[:, :, :, None, :]
    kmat = (jnp.einsum("xgdcp,xgpk->xgdck", wr[:, :, :t], bbr, precision=hi)
            - jnp.einsum("xgdcp,xgpk->xgdck", wi[:, :, :t], bbi, precision=hi))
    kf, kr = kmat[0], kmat[1]
    kcat = jnp.concatenate([kr[:, :0:-1], kf[:, :1] + kr[:, :1], kf[:, 1:]], axis=1)
    idx = jnp.arange(t)[None, :] - jnp.arange(t)[:, None] + (t - 1)
    mm = kcat[:, idx]
    mmat = _pair_diag(mm, "pastck,ab->psaktbc", 2 * w, 2 * w)

    def contrib(pr_, pi_, br_, bi_):
        brt, bit = br_.transpose(0, 2, 1)[:, None], bi_.transpose(0, 2, 1)[:, None]
        re = pr_[:, :, None, :] * brt - pi_[:, :, None, :] * bit
        im = pr_[:, :, None, :] * bit + pi_[:, :, None, :] * brt
        return [_pair_diag(x, "paxys,ab->pxaybs", 2 * w, S5_X) for x in (re, im)]

    bmat = jnp.concatenate(
        contrib(pr[0, :, t - 1::-1], pi[0, :, t - 1::-1], bbr[0], bbi[0])
        + contrib(pr[1, :, :t], pi[1, :, :t], bbr[1], bbi[1]), axis=2)

    def readout(w_):
        return _pair_diag(w_, "patcs,ab->pastbc", S5_X, 2 * w)

    cmat = jnp.concatenate([
        readout(wr[0, :, 1:t + 1]), readout(-wi[0, :, 1:t + 1]),
        readout(wr[1, :, t:0:-1]), readout(-wi[1, :, t:0:-1]),
    ], axis=1)

    pws = {}
    for nchunk in nchunks:
        nstep = max(nchunk.bit_length() - 1, 1)
        qr, qi = power(t * (2 ** jnp.arange(nstep)))

        def lanes(q):
            q = q.reshape(2, S5_PAIRS, 2, nstep, S5_STATE).transpose(1, 0, 3, 2, 4)
            return q.reshape(S5_PAIRS, 2, nstep, S5_X)

        pws[nchunk] = jnp.stack([lanes(qr), lanes(qi)], axis=3)
    return {"m": mmat.astype(BF16), "b": bmat.astype(BF16), "c": cmat.astype(BF16), "pw": pws}


def _even_params(ev_w_in, q_norm, kv_norm, w_uq, w_ukv, rpb, ev_w_out, seq_lens):
    c0 = MLA_Q_RANK
    c1 = c0 + MLA_KV_RANK
    c2 = c1 + MLA_ROPE
    w_kr = ev_w_in[:, c1:c2]
    pad_l = jnp.zeros((D_MODEL, MLA_NOPE), F32)
    pad_r = jnp.zeros((D_MODEL, HEAD_PAD - MLA_NOPE - MLA_ROPE), F32)
    w_in = jnp.concatenate([
        ev_w_in[:, :c1],
        pad_l, w_kr, pad_r,
        pad_l, _rot_cols(w_kr), pad_r,
        ev_w_in[:, c2:],
    ], axis=1).astype(BF16)
    uq = w_uq.reshape(MLA_Q_RANK, MLA_HEADS, MLA_NOPE + MLA_ROPE)
    uq_rot = jnp.concatenate([jnp.zeros_like(uq[..., :MLA_NOPE]), _rot_cols(uq[..., MLA_NOPE:])], axis=-1)
    nope_rope = ((MLA_NOPE, MLA_ROPE), (0, MLA_NOPE))
    v_one = jnp.zeros((MLA_HEADS, V_ROWS), F32).at[:, MLA_V].set(1.0).reshape(MLA_HEADS * V_ROWS, 1)
    nat_bias = {kh: _nat_bias(rpb, kh) for kh in {min(NAT_KH_MAX, sl // GRID_W) for sl in seq_lens}}
    wa = jnp.zeros((MLA_HEADS, HEAD_PAD, D_MODEL), F32)
    wa = wa.at[:, :MLA_V].set(ev_w_out[:MLA_HEADS * MLA_V].reshape(MLA_HEADS, MLA_V, D_MODEL))
    return {
        "w_in": w_in,
        "q_norm": q_norm[None], "kv_norm": kv_norm[None],
        "w_uq_t": _head_rows(w_uq, *nope_rope, HEAD_PAD).astype(BF16),
        "w_uq_rot_t": _head_rows(uq_rot.reshape(MLA_Q_RANK, -1), *nope_rope, HEAD_PAD).astype(BF16),
        "w_uk": _head_rows(w_ukv, (MLA_NOPE, MLA_V), (0, None), HEAD_PAD).T.astype(BF16),
        "w_uv_t": _head_rows(w_ukv, (MLA_NOPE, MLA_V), (None, 0), V_ROWS).astype(BF16),
        "v_one": v_one,
        "rope_tab": {sl: _rope_tables(sl) for sl in seq_lens},
        "nat_bias": {sl: nat_bias[min(NAT_KH_MAX, sl // GRID_W)] for sl in seq_lens},
        "wa": wa.reshape(HP_ALL, D_MODEL).astype(BF16),
        "wb": ev_w_out[MLA_HEADS * MLA_V:].astype(BF16),
    }


def _even_mixer(h, g_pre, g_post, p, batch, seq_len):
    qt, k, vt, nq, nk, nv = _ev_in(h, g_pre, p, seq_len)
    a = _mla(qt, k, vt, batch, seq_len)
    b = _nat(nq, nk, nv, p["nat_bias"][seq_len], batch, seq_len)
    return _ev_out(h, a, b, p["wa"], p["wb"], g_post)


def _odd_mixer(h, g_pre, g_post, p, batch, seq_len):
    u, su = _od_in(h, g_pre, p["w_in"])
    c = _conv(u, p["dw_w"], p["dw_b"], p["ln_g"], p["ln_b"], seq_len)
    nchunk = seq_len // S5_CHUNK
    nc = batch * nchunk
    u_p = su.astype(BF16).reshape(nc, S5_CHUNK, S5_PAIRS, 2 * S5_GROUP)
    u_p = u_p.transpose(2, 0, 1, 3).reshape(S5_PAIRS, nc, S5_W)
    y_p = _s5_scan(u_p, p["s5"], batch, nchunk)
    ys = y_p.reshape(S5_PAIRS, nc, S5_CHUNK, 2 * S5_GROUP).transpose(1, 2, 0, 3)
    ys = ys.reshape(batch * seq_len, S5_CH)
    return _od_out(h, c, su, ys, p["d"], p["w_glu"], p["wc"], p["ws"], g_post)


def kernel(x_prompt, x_sample, norm_g, ffn_w_gate, ffn_w_up, ffn_w_down, ev_w_in, mla_q_norm, mla_kv_norm, mla_w_uq, mla_w_ukv, nat_rpb, ev_w_out, od_w_in, conv_dw_w, conv_dw_b, conv_ln_g, conv_ln_b, s5_lambda_re, s5_lambda_im, s5_log_step, s5_b_re, s5_b_im, s5_c_re, s5_c_im, s5_d, s5_w_glu, od_w_out):
    depth = norm_g.shape[0]
    seq_lens = sorted({x_prompt.shape[1], x_sample.shape[1]})
    wg = ffn_w_gate.astype(BF16)
    wu = ffn_w_up.astype(BF16)
    wd = ffn_w_down.astype(BF16)
    mixers = []
    for layer in range(depth):
        i = layer // 2
        if layer % 2 == 0:
            mixers.append(_even_params(ev_w_in[i], mla_q_norm[i], mla_kv_norm[i], mla_w_uq[i], mla_w_ukv[i],
                                       nat_rpb[i], ev_w_out[i], seq_lens))
        else:
            mixers.append({
                "w_in": od_w_in[i].astype(BF16),
                "dw_w": conv_dw_w[i], "dw_b": conv_dw_b[i][None],
                "ln_g": conv_ln_g[i][None], "ln_b": conv_ln_b[i][None],
                "s5": _s5_operators(s5_lambda_re[i], s5_lambda_im[i], s5_log_step[i], s5_b_re[i], s5_b_im[i],
                                    s5_c_re[i], s5_c_im[i], [sl // S5_CHUNK for sl in seq_lens]),
                "d": s5_d[i][None],
                "w_glu": s5_w_glu[i].astype(BF16),
                "wc": od_w_out[i][:CONV_CH].astype(BF16),
                "ws": od_w_out[i][CONV_CH:].astype(BF16),
            })

    def trunk(x):
        batch, seq_len, _ = x.shape
        h = x.reshape(batch * seq_len, D_MODEL)
        for layer in range(depth):
            g = norm_g[layer][:, None, :]
            h = _ffn(h, g[0], wg[layer, 0], wu[layer, 0], wd[layer, 0], g[1])
            mixer = _even_mixer if layer % 2 == 0 else _odd_mixer
            h = mixer(h, g[2], g[3], mixers[layer], batch, seq_len)
            h = _ffn(h, g[4], wg[layer, 1], wu[layer, 1], wd[layer, 1], g[5])
        return h.reshape(batch, seq_len, D_MODEL)

    return (trunk(x_prompt), trunk(x_sample))
```

```python
import functools
import math

import jax
import jax.numpy as jnp
import numpy as np
from jax import lax
from jax.experimental import pallas as pl
from jax.experimental.pallas import tpu as pltpu

F32 = jnp.float32
BF16 = jnp.bfloat16

LANES = 128
SUBLANES = 8

D_MODEL = 1024
D_FF = 2816
GRID_W = 64

MLA_HEADS = 8
MLA_Q_RANK = 256
MLA_KV_RANK = 128
MLA_NOPE = 64
MLA_ROPE = 32
MLA_V = 64
ROPE_THETA = 10000.0
HEAD_PAD = 128
V_ROWS = 80

NAT_HEADS = 8
NAT_HEAD_DIM = 64
NAT_W = NAT_HEADS * NAT_HEAD_DIM
NAT_KH_MAX = 8
NAT_KW = 16

CONV_CH = 512
CONV_WIDTH = 31
CONV_HALO = 16

S5_CH = 512
S5_GROUP = 16
S5_GROUPS = S5_CH // S5_GROUP
S5_STATE = 64
S5_CHUNK = 16
S5_PAIRS = S5_GROUPS // 2

FFN_RES_SCALE = 0.5
NORM_EPS = 1e-6
NEG_INF = -1e30

VMEM_LIMIT = 48 * 1024 * 1024
FFN_VMEM_LIMIT = 56 * 1024 * 1024


def _cparams(*sem):
    return pltpu.CompilerParams(dimension_semantics=sem, vmem_limit_bytes=VMEM_LIMIT)


def _rms(x, g):
    return x * lax.rsqrt(jnp.mean(x * x, axis=-1, keepdims=True) + NORM_EPS) * g


def _dot(a, b):
    return jnp.dot(a, b, preferred_element_type=F32)


def _dot_nt(a, b):
    return lax.dot_general(a, b, (((1,), (1,)), ((), ())), preferred_element_type=F32)


def _full(shape):
    n = len(shape)
    return pl.BlockSpec(shape, lambda *_: (0,) * n)


def _ffn_kernel(x_ref, gpre_ref, wg_ref, wu_ref, wd_ref, gpost_ref, o_ref):
    x = x_ref[...]
    xn = _rms(x, gpre_ref[...]).astype(BF16)
    gate = _dot(xn, wg_ref[...])
    up = _dot(xn, wu_ref[...])
    hdn = (gate * jax.nn.sigmoid(gate) * up).astype(BF16)
    o_ref[...] = x + FFN_RES_SCALE * _rms(_dot(hdn, wd_ref[...]), gpost_ref[...])


def _ffn(h, gpre, wg, wu, wd, gpost, *, tm=512):
    m = h.shape[0]
    resident = lambda shape: pl.BlockSpec(shape, lambda i: (0, 0), pipeline_mode=pl.Buffered(1))
    return pl.pallas_call(
        _ffn_kernel,
        grid=(m // tm,),
        in_specs=[
            pl.BlockSpec((tm, D_MODEL), lambda i: (i, 0)),
            _full((1, D_MODEL)),
            resident((D_MODEL, D_FF)), resident((D_MODEL, D_FF)), resident((D_FF, D_MODEL)),
            _full((1, D_MODEL)),
        ],
        out_specs=pl.BlockSpec((tm, D_MODEL), lambda i: (i, 0)),
        out_shape=jax.ShapeDtypeStruct((m, D_MODEL), F32),
        compiler_params=pltpu.CompilerParams(dimension_semantics=("parallel",), vmem_limit_bytes=FFN_VMEM_LIMIT),
        name="ffn",
    )(h, gpre, wg, wu, wd, gpost)


EV_Z = MLA_Q_RANK + MLA_KV_RANK + 2 * HEAD_PAD + 3 * NAT_W
HP_ALL = MLA_HEADS * HEAD_PAD


def _ev_in_kernel(h_ref, g_ref, win_ref, qn_ref, kvn_ref, wuq_ref, wuqr_ref, wuk_ref, wuv_ref,
                  tab_ref, tabt_ref, vone_ref, qt_ref, k_ref, vt_ref, nq_ref, nk_ref, nv_ref):
    m = _rms(h_ref[...], g_ref[...]).astype(BF16)
    z = _dot(m, win_ref[...])
    c0 = MLA_Q_RANK
    c1 = c0 + MLA_KV_RANK
    c2 = c1 + HEAD_PAD
    c3 = c2 + HEAD_PAD
    q_lat = z[:, :c0]
    kv_lat = z[:, c0:c1]
    kr = z[:, c1:c2]
    kr_rot = z[:, c2:c3]
    nq_ref[...] = (z[:, c3:c3 + NAT_W] * (NAT_HEAD_DIM ** -0.5)).astype(BF16)
    nk_ref[...] = z[:, c3 + NAT_W:c3 + 2 * NAT_W].astype(BF16)
    nv_ref[...] = z[:, c3 + 2 * NAT_W:c3 + 3 * NAT_W].astype(BF16)

    qn = _rms(q_lat, qn_ref[...]).astype(BF16)
    kvn = _rms(kv_lat, kvn_ref[...]).astype(BF16)
    q_raw_t = _dot_nt(wuq_ref[...], qn)
    q_rot_t = _dot_nt(wuqr_ref[...], qn)
    v_t = _dot_nt(wuv_ref[...], kvn) + vone_ref[...]
    k_nope = _dot(kvn, wuk_ref[...])

    tab = tab_ref[...]
    k_rope = kr * tab[:, :HEAD_PAD] + kr_rot * tab[:, HEAD_PAD:]
    cq_t = tabt_ref[:HEAD_PAD, :]
    sq_t = tabt_ref[HEAD_PAD:, :]
    for hd in range(MLA_HEADS):
        sl = slice(hd * HEAD_PAD, (hd + 1) * HEAD_PAD)
        qt_ref[hd] = (q_raw_t[sl] * cq_t + q_rot_t[sl] * sq_t).astype(BF16)
        k_ref[:, sl] = (k_nope[:, sl] + k_rope).astype(BF16)
        vt_ref[hd] = v_t[hd * V_ROWS:(hd + 1) * V_ROWS].astype(BF16)


def _ev_in(h, g, p, seq_len, *, tm=512):
    m = h.shape[0]
    nblk = seq_len // tm
    tok = lambda w: pl.BlockSpec((tm, w), lambda i: (i, 0))
    feat = lambda r: pl.BlockSpec((MLA_HEADS, r, tm), lambda i: (0, 0, i))
    outs = [jax.ShapeDtypeStruct((MLA_HEADS, HEAD_PAD, m), BF16), jax.ShapeDtypeStruct((m, HP_ALL), BF16),
            jax.ShapeDtypeStruct((MLA_HEADS, V_ROWS, m), BF16)] + [jax.ShapeDtypeStruct((m, NAT_W), BF16)] * 3
    tab, tab_t = p["rope_tab"][seq_len]
    return pl.pallas_call(
        _ev_in_kernel,
        grid=(m // tm,),
        in_specs=[
            tok(D_MODEL), _full((1, D_MODEL)), _full((D_MODEL, EV_Z)),
            _full((1, MLA_Q_RANK)), _full((1, MLA_KV_RANK)),
            _full((HP_ALL, MLA_Q_RANK)), _full((HP_ALL, MLA_Q_RANK)),
            _full((MLA_KV_RANK, HP_ALL)), _full((MLA_HEADS * V_ROWS, MLA_KV_RANK)),
            pl.BlockSpec((tm, 2 * HEAD_PAD), lambda i: (i % nblk, 0)),
            pl.BlockSpec((2 * HEAD_PAD, tm), lambda i: (0, i % nblk)),
            _full((MLA_HEADS * V_ROWS, 1)),
        ],
        out_specs=[feat(HEAD_PAD), tok(HP_ALL), feat(V_ROWS)] + [tok(NAT_W)] * 3,
        out_shape=outs,
        compiler_params=_cparams("parallel"),
        name="ev_in",
    )(h, g, p["w_in"], p["q_norm"], p["kv_norm"], p["w_uq_t"], p["w_uq_rot_t"], p["w_uk"], p["w_uv_t"],
      tab, tab_t, p["v_one"])


def _mla_kernel(qt_ref, k_ref, vt_ref, o_ref, s_scr, *, tk, nk, unroll):
    qt = qt_ref[0]
    tq = qt.shape[1]

    def scores(j):
        off = pl.multiple_of(j * tk, tk)
        return _dot(k_ref[pl.ds(off, tk), :], qt)

    s_scr[0] = scores(0)

    def body(jj, carry):
        m_prev, acc = carry
        for i in range(unroll):
            j = jj * unroll + i
            s_scr[(i + 1) % 2] = scores(jnp.minimum(j + 1, nk - 1))
            st = s_scr[i % 2]
            m_new = jnp.maximum(m_prev, jnp.max(st, axis=0, keepdims=True))
            alpha = jnp.exp2(m_prev - m_new)
            pt = jnp.exp2(st - m_new).astype(BF16)
            off = pl.multiple_of(j * tk, tk)
            acc = alpha * acc + _dot(vt_ref[0, :, pl.ds(off, tk)], pt)
            m_prev = m_new
        return m_prev, acc

    m_init = jnp.full((1, tq), jnp.finfo(F32).min, F32)
    acc = jnp.zeros((V_ROWS, tq), F32)
    _, acc = lax.fori_loop(0, nk // unroll, body, (m_init, acc))
    out_t = acc[:MLA_V] / acc[MLA_V:MLA_V + 1]
    out_t = jnp.concatenate([out_t, jnp.zeros((HEAD_PAD - MLA_V, tq), F32)], axis=0)
    o_ref[...] = out_t.T.astype(BF16)


def _mla(qt, k, vt, batch, seq_len, *, tq=512, tk=512, unroll=8):
    m = k.shape[0]
    tk = min(tk, seq_len // unroll)
    nq = seq_len // tq
    nk = seq_len // tk
    assert unroll % 2 == 0 and nk % unroll == 0
    return pl.pallas_call(
        functools.partial(_mla_kernel, tk=tk, nk=nk, unroll=unroll),
        grid=(batch, MLA_HEADS, nq),
        in_specs=[
            pl.BlockSpec((1, HEAD_PAD, tq), lambda b, h, i: (h, 0, b * nq + i)),
            pl.BlockSpec((seq_len, HEAD_PAD), lambda b, h, i: (b, h)),
            pl.BlockSpec((1, V_ROWS, seq_len), lambda b, h, i: (h, 0, b)),
        ],
        out_specs=pl.BlockSpec((tq, HEAD_PAD), lambda b, h, i: (b * nq + i, h)),
        out_shape=jax.ShapeDtypeStruct((m, HP_ALL), BF16),
        scratch_shapes=[pltpu.VMEM((2, tk, tq), F32)],
        compiler_params=_cparams("parallel", "parallel", "arbitrary"),
        name="mla",
    )(qt, k, vt)


NAT_LANES = 2 * NAT_HEAD_DIM


def _nat_kernel(q_ref, k_ref, v_ref, bias_ref, o_ref, *, rows, kh, rblk):
    i = pl.program_id(2)
    lane = lax.broadcasted_iota(jnp.int32, (GRID_W, NAT_LANES), 1)
    head0 = lane < NAT_HEAD_DIM

    def row_body(rr, carry):
        r = i * rblk + rr
        start = jnp.clip(r - kh // 2, 0, rows - kh)
        delta = r - start
        koff = pl.multiple_of(start * GRID_W, GRID_W)
        qoff = pl.multiple_of(rr * GRID_W, GRID_W)
        qrow = q_ref[pl.ds(qoff, GRID_W), :]
        kwin = k_ref[pl.ds(koff, kh * GRID_W), :]
        vwin = v_ref[pl.ds(koff, kh * GRID_W), :]
        zero = jnp.zeros_like(qrow)
        q2 = jnp.concatenate([jnp.where(head0, qrow, zero), jnp.where(head0, zero, qrow)], axis=0)
        st = _dot_nt(kwin, q2) + bias_ref[delta, 0]
        e = jnp.exp(st - jnp.max(st, axis=0, keepdims=True))
        pt = (e / jnp.sum(e, axis=0, keepdims=True)).astype(BF16)
        o2 = lax.dot_general(pt, vwin, (((0,), (0,)), ((), ())), preferred_element_type=F32)
        o_ref[pl.ds(qoff, GRID_W), :] = jnp.where(head0, o2[:GRID_W], o2[GRID_W:]).astype(BF16)
        return carry

    lax.fori_loop(0, rblk, row_body, 0, unroll=True)


def _nat(nq, nk, nv, bias, batch, seq_len, *, rblk=8):
    m = nq.shape[0]
    rows = seq_len // GRID_W
    kh = min(NAT_KH_MAX, rows)
    nblk = rows // rblk
    tq = rblk * GRID_W
    return pl.pallas_call(
        functools.partial(_nat_kernel, rows=rows, kh=kh, rblk=rblk),
        grid=(batch, NAT_HEADS // 2, nblk),
        in_specs=[
            pl.BlockSpec((tq, NAT_LANES), lambda b, hp, i: (b * nblk + i, hp)),
            pl.BlockSpec((seq_len, NAT_LANES), lambda b, hp, i: (b, hp)),
            pl.BlockSpec((seq_len, NAT_LANES), lambda b, hp, i: (b, hp)),
            pl.BlockSpec((kh, 1, kh * GRID_W, 2 * GRID_W), lambda b, hp, i: (0, hp, 0, 0)),
        ],
        out_specs=pl.BlockSpec((tq, NAT_LANES), lambda b, hp, i: (b * nblk + i, hp)),
        out_shape=jax.ShapeDtypeStruct((m, NAT_W), BF16),
        compiler_params=_cparams("parallel", "parallel", "arbitrary"),
        name="nat",
    )(nq, nk, nv, bias)


def _mix_out_kernel(h_ref, a_ref, b_ref, wa_ref, wb_ref, g_ref, o_ref):
    mix = _dot(a_ref[...], wa_ref[...]) + _dot(b_ref[...], wb_ref[...])
    o_ref[...] = h_ref[...] + _rms(mix, g_ref[...])


def _ev_out(h, a, b, wa, wb, g, *, tm=512):
    m = h.shape[0]
    tok = lambda w: pl.BlockSpec((tm, w), lambda i: (i, 0))
    return pl.pallas_call(
        _mix_out_kernel,
        grid=(m // tm,),
        in_specs=[tok(D_MODEL), tok(a.shape[1]), tok(b.shape[1]),
                  _full(wa.shape), _full(wb.shape), _full((1, D_MODEL))],
        out_specs=tok(D_MODEL),
        out_shape=jax.ShapeDtypeStruct((m, D_MODEL), F32),
        compiler_params=_cparams("parallel"),
        name="ev_out",
    )(h, a, b, wa, wb, g)


def _od_in_kernel(h_ref, g_ref, win_ref, u_ref, su_ref):
    m = _rms(h_ref[...], g_ref[...]).astype(BF16)
    z = _dot(m, win_ref[...])
    ca = z[:, :CONV_CH]
    cg = z[:, CONV_CH:2 * CONV_CH]
    u_ref[...] = ca * jax.nn.sigmoid(cg)
    su_ref[...] = z[:, 2 * CONV_CH:]


def _od_in(h, g, w_in, *, tm=512):
    m = h.shape[0]
    tok = lambda w: pl.BlockSpec((tm, w), lambda i: (i, 0))
    return pl.pallas_call(
        _od_in_kernel,
        grid=(m // tm,),
        in_specs=[tok(D_MODEL), _full((1, D_MODEL)), _full(w_in.shape)],
        out_specs=[tok(CONV_CH), tok(S5_CH)],
        out_shape=[jax.ShapeDtypeStruct((m, CONV_CH), F32), jax.ShapeDtypeStruct((m, S5_CH), F32)],
        compiler_params=_cparams("parallel"),
        name="od_in",
    )(h, g, w_in)


CONV_SUB = 64


def _conv_kernel(prev_ref, cur_ref, next_ref, w_ref, b_ref, lg_ref, lb_ref, o_ref, scr, *, tm, nblk):
    i = pl.program_id(0)
    first = (i % nblk) == 0
    last = (i % nblk) == nblk - 1
    scr[0:CONV_HALO, :] = jnp.where(first, 0.0, prev_ref[...])
    scr[CONV_HALO:CONV_HALO + tm, :] = cur_ref[...]
    scr[CONV_HALO + tm:, :] = jnp.where(last, 0.0, next_ref[...])
    w = w_ref[...]
    shift = CONV_HALO - CONV_WIDTH // 2

    def sub(c, carry):
        base = pl.multiple_of(c * CONV_SUB, CONV_SUB)
        cols = []
        for lb in range(CONV_CH // LANES):
            ls = slice(lb * LANES, (lb + 1) * LANES)
            win = scr[pl.ds(base, CONV_SUB + 2 * CONV_HALO), ls]
            acc = jnp.zeros((CONV_SUB, LANES), F32)
            nwin = CONV_SUB + 2 * CONV_HALO
            for b in range(SUBLANES):
                wb = pltpu.roll(win, nwin - b, 0) if b else win
                for a in range(2 * CONV_HALO // SUBLANES):
                    kk = SUBLANES * a + b - shift
                    if 0 <= kk < CONV_WIDTH:
                        acc = acc + wb[SUBLANES * a:SUBLANES * a + CONV_SUB, :] * w[kk:kk + 1, ls]
            cols.append(acc)
        y = jnp.concatenate(cols, axis=1) + b_ref[...]
        mu = jnp.mean(y, axis=-1, keepdims=True)
        yc = y - mu
        yn = yc * lax.rsqrt(jnp.mean(yc * yc, axis=-1, keepdims=True) + NORM_EPS)
        yn = yn * lg_ref[...] + lb_ref[...]
        o_ref[pl.ds(base, CONV_SUB), :] = (yn * jax.nn.sigmoid(yn)).astype(BF16)
        return carry

    lax.fori_loop(0, tm // CONV_SUB, sub, 0)


def _conv(u, w, b, lg, lb, seq_len, *, tm=512):
    m = u.shape[0]
    nblk = seq_len // tm
    hb = tm // CONV_HALO
    nh = m // CONV_HALO
    return pl.pallas_call(
        functools.partial(_conv_kernel, tm=tm, nblk=nblk),
        grid=(m // tm,),
        in_specs=[
            pl.BlockSpec((CONV_HALO, CONV_CH), lambda i: (jnp.maximum(i * hb - 1, 0), 0)),
            pl.BlockSpec((tm, CONV_CH), lambda i: (i, 0)),
            pl.BlockSpec((CONV_HALO, CONV_CH), lambda i: (jnp.minimum((i + 1) * hb, nh - 1), 0)),
            _full((CONV_WIDTH, CONV_CH)), _full((1, CONV_CH)), _full((1, CONV_CH)), _full((1, CONV_CH)),
        ],
        out_specs=pl.BlockSpec((tm, CONV_CH), lambda i: (i, 0)),
        out_shape=jax.ShapeDtypeStruct((m, CONV_CH), BF16),
        scratch_shapes=[pltpu.VMEM((tm + 2 * CONV_HALO, CONV_CH), F32)],
        compiler_params=_cparams("parallel"),
        name="conv",
    )(u, u, u, w, b, lg, lb)


S5_W = 2 * S5_GROUP * S5_CHUNK
S5_X = 2 * S5_STATE
S5_PIECE = 2 * S5_GROUP
S5_NPP = LANES // S5_PIECE
S5_VMEM_LIMIT = 56 * 1024 * 1024


def _s5_kernel(su_ref, m_ref, b_ref, c_ref, pw_ref, y_ref, u_scr, y_scr, *, nchunk):
    nstep = nchunk.bit_length() - 1
    slot = lax.broadcasted_iota(jnp.int32, (nchunk, LANES), 1) // S5_PIECE
    row = lax.broadcasted_iota(jnp.int32, (nchunk, S5_X), 0)

    def token_rows(ref, sigma):
        return ref.at[pl.ds(sigma, nchunk, stride=S5_CHUNK), :]

    def move(x, src, dst):
        return x if src == dst else pltpu.roll(x, ((dst - src) % S5_NPP) * S5_PIECE, 1)

    for col in range(S5_W // LANES):
        acc = [None] * S5_NPP
        for s in range(S5_NPP):
            x = token_rows(su_ref, S5_NPP * col + s)[...]
            for pp in range(S5_NPP):
                r = move(x, pp, s)
                acc[pp] = r if s == 0 else jnp.where(slot == s, r, acc[pp])
        for pp in range(S5_NPP):
            u_scr[pp, :, col * LANES:(col + 1) * LANES] = acc[pp].astype(BF16)

    for pp in range(S5_NPP):
        u = u_scr[pp]
        contrib = _dot(u, b_ref[pp])
        states = []
        for d in range(2):
            xr = contrib[:, (2 * d) * S5_X:(2 * d + 1) * S5_X]
            xi = contrib[:, (2 * d + 1) * S5_X:(2 * d + 2) * S5_X]

            def shifted(x, sh):
                if d == 0:
                    return jnp.where(row >= sh, pltpu.roll(x, sh, 0), 0.0)
                return jnp.where(row < nchunk - sh, pltpu.roll(x, nchunk - sh, 0), 0.0)

            for k in range(nstep):
                ar = pw_ref[pp, d, k, 0:1, :]
                ai = pw_ref[pp, d, k, 1:2, :]
                sr = shifted(xr, 1 << k)
                si = shifted(xi, 1 << k)
                xr, xi = xr + ar * sr - ai * si, xi + ar * si + ai * sr
            states += [shifted(xr, 1), shifted(xi, 1)]
        x = jnp.concatenate(states, axis=1).astype(BF16)
        y_scr[pp] = _dot(u, m_ref[pp]) + _dot(x, c_ref[pp])

    for col in range(S5_W // LANES):
        ys = [y_scr[pp, :, col * LANES:(col + 1) * LANES] for pp in range(S5_NPP)]
        for s in range(S5_NPP):
            out = move(ys[0], s, 0)
            for pp in range(1, S5_NPP):
                out = jnp.where(slot == pp, move(ys[pp], s, pp), out)
            token_rows(y_ref, S5_NPP * col + s)[...] = out


def _s5(su, ops, batch, seq_len):
    m = su.shape[0]
    nchunk = seq_len // S5_CHUNK
    assert nchunk & (nchunk - 1) == 0
    nstep = max(nchunk.bit_length() - 1, 1)
    seq_blk = pl.BlockSpec((seq_len, LANES), lambda q, b: (b, q), pipeline_mode=pl.Buffered(1))
    per_q = lambda *shape: pl.BlockSpec((S5_NPP,) + shape, lambda q, b: (q,) + (0,) * len(shape))
    return pl.pallas_call(
        functools.partial(_s5_kernel, nchunk=nchunk),
        grid=(S5_CH // LANES, batch),
        in_specs=[seq_blk, per_q(S5_W, S5_W), per_q(S5_W, 4 * S5_X), per_q(4 * S5_X, S5_W),
                  per_q(2, nstep, 2, S5_X)],
        out_specs=seq_blk,
        out_shape=jax.ShapeDtypeStruct((m, S5_CH), F32),
        scratch_shapes=[pltpu.VMEM((S5_NPP, nchunk, S5_W), BF16), pltpu.VMEM((S5_NPP, nchunk, S5_W), F32)],
        compiler_params=pltpu.CompilerParams(dimension_semantics=("parallel", "arbitrary"),
                                             vmem_limit_bytes=S5_VMEM_LIMIT),
        name="s5",
    )(su, ops["m"], ops["b"], ops["c"], ops["pw"][nchunk])


def _od_out_kernel(h_ref, c_ref, su_ref, ys_ref, d_ref, wglu_ref, wc_ref, ws_ref, g_ref, o_ref):
    y = d_ref[...] * su_ref[...] + ys_ref[...]
    z = jax.nn.gelu(y, approximate=True)
    sg = (z * jax.nn.sigmoid(_dot(z.astype(BF16), wglu_ref[...]))).astype(BF16)
    mix = _dot(c_ref[...], wc_ref[...]) + _dot(sg, ws_ref[...])
    o_ref[...] = h_ref[...] + _rms(mix, g_ref[...])


def _od_out(h, c, su, ys, d, wglu, wc, ws, g, *, tm=512):
    m = h.shape[0]
    tok = lambda w: pl.BlockSpec((tm, w), lambda i: (i, 0))
    return pl.pallas_call(
        _od_out_kernel,
        grid=(m // tm,),
        in_specs=[tok(D_MODEL), tok(CONV_CH), tok(S5_CH), tok(S5_CH), _full((1, S5_CH)),
                  _full(wglu.shape), _full(wc.shape), _full(ws.shape), _full((1, D_MODEL))],
        out_specs=tok(D_MODEL),
        out_shape=jax.ShapeDtypeStruct((m, D_MODEL), F32),
        compiler_params=_cparams("parallel"),
        name="od_out",
    )(h, c, su, ys, d, wglu, wc, ws, g)


def _pad_heads(w, lo, hi, width):
    k = w.shape[0]
    w = w.reshape(k, MLA_HEADS, -1)[:, :, lo:hi]
    return jnp.pad(w, ((0, 0), (0, 0), (0, width - (hi - lo)))).reshape(k, MLA_HEADS * width)


def _rot_cols(w):
    half = w.shape[-1] // 2
    return jnp.concatenate([-w[..., half:], w[..., :half]], axis=-1)


def _rope_tables(seq_len):
    half = MLA_ROPE // 2
    inv = ROPE_THETA ** (-jnp.arange(half, dtype=F32) / half)
    ang = jnp.arange(seq_len, dtype=F32)[:, None] * inv[None, :]
    cos = jnp.concatenate([jnp.cos(ang)] * 2, axis=1)
    sin = jnp.concatenate([jnp.sin(ang)] * 2, axis=1)
    scale = (MLA_NOPE + MLA_ROPE) ** -0.5 * math.log2(math.e)
    z64 = jnp.zeros((seq_len, MLA_NOPE), F32)
    z32 = jnp.zeros((seq_len, HEAD_PAD - MLA_NOPE - MLA_ROPE), F32)
    cq = jnp.concatenate([z64 + scale, cos * scale, z32], axis=1)
    sq = jnp.concatenate([z64, sin * scale, z32], axis=1)
    ck = jnp.concatenate([z64, cos, z32], axis=1)
    sk = jnp.concatenate([z64, sin, z32], axis=1)
    return jnp.concatenate([ck, sk], axis=1), jnp.concatenate([cq, sq], axis=1).T


def _nat_bias(rpb, kh):
    c = np.arange(GRID_W)
    col_start = np.clip(c - NAT_KW // 2, 0, GRID_W - NAT_KW)
    col_ok = (c[None, :] >= col_start[:, None]) & (c[None, :] < col_start[:, None] + NAT_KW)
    col_off = np.clip(c[None, :] - c[:, None], -(NAT_KW - 1), NAT_KW - 1) + (NAT_KW - 1)
    delta = np.arange(kh)
    row_off = np.arange(kh)[None, :] - delta[:, None] + (NAT_KH_MAX - 1)
    row_sel = (row_off[:, :, None] == np.arange(2 * NAT_KH_MAX - 1)).astype(np.float32)
    col_sel = (col_off[:, :, None] == np.arange(2 * NAT_KW - 1)).astype(np.float32)
    bias = jnp.einsum("hrc,djr,qkc->dhqjk", rpb.astype(F32), row_sel, col_sel, precision=lax.Precision.HIGHEST)
    bias = jnp.where(col_ok[None, None, :, None, :], bias, NEG_INF)
    bias = bias.reshape(kh, NAT_HEADS // 2, 2 * GRID_W, kh * GRID_W)
    return bias.transpose(0, 1, 3, 2)


def _pair_diag(x, spec, rows, cols):
    x = x.reshape((S5_PAIRS, 2) + x.shape[1:])
    return jnp.einsum(spec, x, jnp.eye(2, dtype=x.dtype)).reshape(S5_PAIRS, rows, cols)


def _s5_operators(lam_re, lam_im, log_step, b_re, b_im, c_re, c_im, nchunks):
    t = S5_CHUNK
    w = S5_GROUP * t
    dt = jnp.exp(log_step)[:, :, None]
    ar, ai = lam_re * dt, lam_im * dt
    er = jnp.exp(ar)
    lbr, lbi = er * jnp.cos(ai), er * jnp.sin(ai)
    den = lam_re * lam_re + lam_im * lam_im
    fr = ((lbr - 1.0) * lam_re + lbi * lam_im) / den
    fi = (lbi * lam_re - (lbr - 1.0) * lam_im) / den
    bbr = fr[..., None] * b_re - fi[..., None] * b_im
    bbi = fr[..., None] * b_im + fi[..., None] * b_re

    def power(d):
        d = d.astype(F32)[None, None, :, None]
        mag = jnp.exp(ar[:, :, None, :] * d)
        return mag * jnp.cos(ai[:, :, None, :] * d), mag * jnp.sin(ai[:, :, None, :] * d)

    hi = lax.Precision.HIGHEST
    pr, pi = power(jnp.arange(t + 1))
    wr = c_re[:, :, None] * pr[:, :, :, None, :] - c_im[:, :, None] * pi[:, :, :, None, :]
    wi = c_re[:, :, None] * pi[:, :, :, None, :] + c_im[:, :, None] * pr[:, :, :, None, :]
    kmat = (jnp.einsum("xgdcp,xgpk->xgdck", wr[:, :, :t], bbr, precision=hi)
            - jnp.einsum("xgdcp,xgpk->xgdck", wi[:, :, :t], bbi, precision=hi))
    kf, kr = kmat[0], kmat[1]
    kcat = jnp.concatenate([kr[:, :0:-1], kf[:, :1] + kr[:, :1], kf[:, 1:]], axis=1)
    idx = jnp.arange(t)[None, :] - jnp.arange(t)[:, None] + (t - 1)
    mm = kcat[:, idx]
    mmat = _pair_diag(mm, "pastck,ab->psaktbc", 2 * w, 2 * w)

    def contrib(pr_, pi_, br_, bi_):
        brt, bit = br_.transpose(0, 2, 1)[:, None], bi_.transpose(0, 2, 1)[:, None]
        re = pr_[:, :, None, :] * brt - pi_[:, :, None, :] * bit
        im = pr_[:, :, None, :] * bit + pi_[:, :, None, :] * brt
        return [_pair_diag(x, "paxys,ab->pxaybs", 2 * w, S5_X) for x in (re, im)]

    bmat = jnp.concatenate(
        contrib(pr[0, :, t - 1::-1], pi[0, :, t - 1::-1], bbr[0], bbi[0])
        + contrib(pr[1, :, :t], pi[1, :, :t], bbr[1], bbi[1]), axis=2)

    def readout(w_):
        return _pair_diag(w_, "patcs,ab->pastbc", S5_X, 2 * w)

    cmat = jnp.concatenate([
        readout(wr[0, :, 1:t + 1]), readout(-wi[0, :, 1:t + 1]),
        readout(wr[1, :, t:0:-1]), readout(-wi[1, :, t:0:-1]),
    ], axis=1)

    pws = {}
    for nchunk in nchunks:
        nstep = max(nchunk.bit_length() - 1, 1)
        qr, qi = power(t * (2 ** jnp.arange(nstep)))

        def lanes(q):
            q = q.reshape(2, S5_PAIRS, 2, nstep, S5_STATE).transpose(1, 0, 3, 2, 4)
            return q.reshape(S5_PAIRS, 2, nstep, S5_X)

        pws[nchunk] = jnp.stack([lanes(qr), lanes(qi)], axis=3)
    return {"m": mmat.astype(BF16), "b": bmat.astype(BF16), "c": cmat.astype(BF16), "pw": pws}


def _even_params(ev_w_in, q_norm, kv_norm, w_uq, w_ukv, rpb, ev_w_out, seq_lens):
    c0 = MLA_Q_RANK
    c1 = c0 + MLA_KV_RANK
    c2 = c1 + MLA_ROPE
    w_kr = ev_w_in[:, c1:c2]
    pad_l = jnp.zeros((D_MODEL, MLA_NOPE), F32)
    pad_r = jnp.zeros((D_MODEL, HEAD_PAD - MLA_NOPE - MLA_ROPE), F32)
    w_in = jnp.concatenate([
        ev_w_in[:, :c1],
        pad_l, w_kr, pad_r,
        pad_l, _rot_cols(w_kr), pad_r,
        ev_w_in[:, c2:],
    ], axis=1).astype(BF16)
    uq = w_uq.reshape(MLA_Q_RANK, MLA_HEADS, MLA_NOPE + MLA_ROPE)
    uq_rot = jnp.concatenate([jnp.zeros_like(uq[..., :MLA_NOPE]), _rot_cols(uq[..., MLA_NOPE:])], axis=-1)
    qk_dim = MLA_NOPE + MLA_ROPE
    v_one = np.zeros((MLA_HEADS, V_ROWS), np.float32)
    v_one[:, MLA_V] = 1.0
    nat_bias = {kh: _nat_bias(rpb, kh) for kh in {min(NAT_KH_MAX, sl // GRID_W) for sl in seq_lens}}
    wa = ev_w_out[:MLA_HEADS * MLA_V].reshape(MLA_HEADS, MLA_V, D_MODEL)
    wa = jnp.pad(wa, ((0, 0), (0, HEAD_PAD - MLA_V), (0, 0)))
    return {
        "w_in": w_in,
        "q_norm": q_norm[None], "kv_norm": kv_norm[None],
        "w_uq_t": _pad_heads(w_uq, 0, qk_dim, HEAD_PAD).T.astype(BF16),
        "w_uq_rot_t": _pad_heads(uq_rot.reshape(MLA_Q_RANK, -1), 0, qk_dim, HEAD_PAD).T.astype(BF16),
        "w_uk": _pad_heads(w_ukv, 0, MLA_NOPE, HEAD_PAD).astype(BF16),
        "w_uv_t": _pad_heads(w_ukv, MLA_NOPE, MLA_NOPE + MLA_V, V_ROWS).T.astype(BF16),
        "v_one": jnp.asarray(v_one.reshape(MLA_HEADS * V_ROWS, 1)),
        "rope_tab": {sl: _rope_tables(sl) for sl in seq_lens},
        "nat_bias": {sl: nat_bias[min(NAT_KH_MAX, sl // GRID_W)] for sl in seq_lens},
        "wa": wa.reshape(HP_ALL, D_MODEL).astype(BF16),
        "wb": ev_w_out[MLA_HEADS * MLA_V:].astype(BF16),
    }


def _even_mixer(h, g_pre, g_post, p, batch, seq_len):
    qt, k, vt, nq, nk, nv = _ev_in(h, g_pre, p, seq_len)
    a = _mla(qt, k, vt, batch, seq_len)
    b = _nat(nq, nk, nv, p["nat_bias"][seq_len], batch, seq_len)
    return _ev_out(h, a, b, p["wa"], p["wb"], g_post)


def _odd_mixer(h, g_pre, g_post, p, batch, seq_len):
    u, su = _od_in(h, g_pre, p["w_in"])
    c = _conv(u, p["dw_w"], p["dw_b"], p["ln_g"], p["ln_b"], seq_len)
    ys = _s5(su, p["s5"], batch, seq_len)
    return _od_out(h, c, su, ys, p["d"], p["w_glu"], p["wc"], p["ws"], g_post)


def kernel(x_prompt, x_sample, norm_g, ffn_w_gate, ffn_w_up, ffn_w_down, ev_w_in, mla_q_norm, mla_kv_norm, mla_w_uq, mla_w_ukv, nat_rpb, ev_w_out, od_w_in, conv_dw_w, conv_dw_b, conv_ln_g, conv_ln_b, s5_lambda_re, s5_lambda_im, s5_log_step, s5_b_re, s5_b_im, s5_c_re, s5_c_im, s5_d, s5_w_glu, od_w_out):
    depth = norm_g.shape[0]
    seq_lens = sorted({x_prompt.shape[1], x_sample.shape[1]})
    ffn_w = [[tuple(w[layer, j].astype(BF16) for w in (ffn_w_gate, ffn_w_up, ffn_w_down)) for j in range(2)]
             for layer in range(depth)]
    mixers = []
    for layer in range(depth):
        i = layer // 2
        if layer % 2 == 0:
            mixers.append(_even_params(ev_w_in[i], mla_q_norm[i], mla_kv_norm[i], mla_w_uq[i], mla_w_ukv[i],
                                       nat_rpb[i], ev_w_out[i], seq_lens))
        else:
            mixers.append({
                "w_in": od_w_in[i].astype(BF16),
                "dw_w": conv_dw_w[i], "dw_b": conv_dw_b[i][None],
                "ln_g": conv_ln_g[i][None], "ln_b": conv_ln_b[i][None],
                "s5": _s5_operators(s5_lambda_re[i], s5_lambda_im[i], s5_log_step[i], s5_b_re[i], s5_b_im[i],
                                    s5_c_re[i], s5_c_im[i], [sl // S5_CHUNK for sl in seq_lens]),
                "d": s5_d[i][None],
                "w_glu": s5_w_glu[i].astype(BF16),
                "wc": od_w_out[i][:CONV_CH].astype(BF16),
                "ws": od_w_out[i][CONV_CH:].astype(BF16),
            })

    gains = [[norm_g[layer, i][None] for i in range(norm_g.shape[1])] for layer in range(depth)]

    def trunk(x):
        batch, seq_len, _ = x.shape
        h = x.reshape(batch * seq_len, D_MODEL)
        for layer in range(depth):
            g = gains[layer]
            h = _ffn(h, g[0], *ffn_w[layer][0], g[1])
            mixer = _even_mixer if layer % 2 == 0 else _odd_mixer
            h = mixer(h, g[2], g[3], mixers[layer], batch, seq_len)
            h = _ffn(h, g[4], *ffn_w[layer][1], g[5])
        return h.reshape(batch, seq_len, D_MODEL)

    return (trunk(x_prompt), trunk(x_sample))
```

```python
import functools
import math

import jax
import jax.numpy as jnp
import numpy as np
from jax import lax
from jax.experimental import pallas as pl
from jax.experimental.pallas import tpu as pltpu

F32 = jnp.float32
BF16 = jnp.bfloat16

LANES = 128
SUBLANES = 8

D_MODEL = 1024
D_FF = 2816
GRID_W = 64

MLA_HEADS = 8
MLA_Q_RANK = 256
MLA_KV_RANK = 128
MLA_NOPE = 64
MLA_ROPE = 32
MLA_V = 64
ROPE_THETA = 10000.0
HEAD_PAD = 128
V_ROWS = 80

NAT_HEADS = 8
NAT_HEAD_DIM = 64
NAT_W = NAT_HEADS * NAT_HEAD_DIM
NAT_KH_MAX = 8
NAT_KW = 16

CONV_CH = 512
CONV_WIDTH = 31
CONV_HALO = 16

S5_CH = 512
S5_GROUP = 16
S5_GROUPS = S5_CH // S5_GROUP
S5_STATE = 64
S5_CHUNK = 16
S5_PAIRS = S5_GROUPS // 2

FFN_RES_SCALE = 0.5
NORM_EPS = 1e-6
NEG_INF = -1e30

VMEM_LIMIT = 48 * 1024 * 1024
FFN_VMEM_LIMIT = 56 * 1024 * 1024


def _cparams(*sem):
    return pltpu.CompilerParams(dimension_semantics=sem, vmem_limit_bytes=VMEM_LIMIT)


def _rms(x, g):
    return x * lax.rsqrt(jnp.mean(x * x, axis=-1, keepdims=True) + NORM_EPS) * g


def _dot(a, b):
    return jnp.dot(a, b, preferred_element_type=F32)


def _dot_nt(a, b):
    return lax.dot_general(a, b, (((1,), (1,)), ((), ())), preferred_element_type=F32)


def _full(shape):
    n = len(shape)
    return pl.BlockSpec(shape, lambda *_: (0,) * n)


def _ffn_kernel(x_ref, gpre_ref, wg_ref, wu_ref, wd_ref, gpost_ref, o_ref):
    x = x_ref[...]
    xn = _rms(x, gpre_ref[...]).astype(BF16)
    gate = _dot(xn, wg_ref[...])
    up = _dot(xn, wu_ref[...])
    hdn = (gate * jax.nn.sigmoid(gate) * up).astype(BF16)
    o_ref[...] = x + FFN_RES_SCALE * _rms(_dot(hdn, wd_ref[...]), gpost_ref[...])


def _ffn(h, gpre, wg, wu, wd, gpost, *, tm=512):
    m = h.shape[0]
    resident = lambda shape: pl.BlockSpec(shape, lambda i: (0, 0), pipeline_mode=pl.Buffered(1))
    return pl.pallas_call(
        _ffn_kernel,
        grid=(m // tm,),
        in_specs=[
            pl.BlockSpec((tm, D_MODEL), lambda i: (i, 0)),
            _full((1, D_MODEL)),
            resident((D_MODEL, D_FF)), resident((D_MODEL, D_FF)), resident((D_FF, D_MODEL)),
            _full((1, D_MODEL)),
        ],
        out_specs=pl.BlockSpec((tm, D_MODEL), lambda i: (i, 0)),
        out_shape=jax.ShapeDtypeStruct((m, D_MODEL), F32),
        compiler_params=pltpu.CompilerParams(dimension_semantics=("parallel",), vmem_limit_bytes=FFN_VMEM_LIMIT),
        name="ffn",
    )(h, gpre, wg, wu, wd, gpost)


EV_Z = MLA_Q_RANK + MLA_KV_RANK + 2 * HEAD_PAD + 3 * NAT_W
HP_ALL = MLA_HEADS * HEAD_PAD


def _ev_in_kernel(h_ref, g_ref, win_ref, qn_ref, kvn_ref, wuq_ref, wuqr_ref, wuk_ref, wuv_ref,
                  tab_ref, tabt_ref, vone_ref, qt_ref, k_ref, vt_ref, nq_ref, nk_ref, nv_ref):
    m = _rms(h_ref[...], g_ref[...]).astype(BF16)
    z = _dot(m, win_ref[...])
    c0 = MLA_Q_RANK
    c1 = c0 + MLA_KV_RANK
    c2 = c1 + HEAD_PAD
    c3 = c2 + HEAD_PAD
    q_lat = z[:, :c0]
    kv_lat = z[:, c0:c1]
    kr = z[:, c1:c2]
    kr_rot = z[:, c2:c3]
    nq_ref[...] = (z[:, c3:c3 + NAT_W] * (NAT_HEAD_DIM ** -0.5)).astype(BF16)
    nk_ref[...] = z[:, c3 + NAT_W:c3 + 2 * NAT_W].astype(BF16)
    nv_ref[...] = z[:, c3 + 2 * NAT_W:c3 + 3 * NAT_W].astype(BF16)

    qn = _rms(q_lat, qn_ref[...]).astype(BF16)
    kvn = _rms(kv_lat, kvn_ref[...]).astype(BF16)
    q_raw_t = _dot_nt(wuq_ref[...], qn)
    q_rot_t = _dot_nt(wuqr_ref[...], qn)
    v_t = _dot_nt(wuv_ref[...], kvn) + vone_ref[...]
    k_nope = _dot(kvn, wuk_ref[...])

    tab = tab_ref[...]
    k_rope = kr * tab[:, :HEAD_PAD] + kr_rot * tab[:, HEAD_PAD:]
    cq_t = tabt_ref[:HEAD_PAD, :]
    sq_t = tabt_ref[HEAD_PAD:, :]
    for hd in range(MLA_HEADS):
        sl = slice(hd * HEAD_PAD, (hd + 1) * HEAD_PAD)
        qt_ref[hd] = (q_raw_t[sl] * cq_t + q_rot_t[sl] * sq_t).astype(BF16)
        k_ref[:, sl] = (k_nope[:, sl] + k_rope).astype(BF16)
        vt_ref[hd] = v_t[hd * V_ROWS:(hd + 1) * V_ROWS].astype(BF16)


def _ev_in(h, g, p, seq_len, *, tm=512):
    m = h.shape[0]
    nblk = seq_len // tm
    tok = lambda w: pl.BlockSpec((tm, w), lambda i: (i, 0))
    feat = lambda r: pl.BlockSpec((MLA_HEADS, r, tm), lambda i: (0, 0, i))
    outs = [jax.ShapeDtypeStruct((MLA_HEADS, HEAD_PAD, m), BF16), jax.ShapeDtypeStruct((m, HP_ALL), BF16),
            jax.ShapeDtypeStruct((MLA_HEADS, V_ROWS, m), BF16)] + [jax.ShapeDtypeStruct((m, NAT_W), BF16)] * 3
    tab, tab_t = p["rope_tab"][seq_len]
    return pl.pallas_call(
        _ev_in_kernel,
        grid=(m // tm,),
        in_specs=[
            tok(D_MODEL), _full((1, D_MODEL)), _full((D_MODEL, EV_Z)),
            _full((1, MLA_Q_RANK)), _full((1, MLA_KV_RANK)),
            _full((HP_ALL, MLA_Q_RANK)), _full((HP_ALL, MLA_Q_RANK)),
            _full((MLA_KV_RANK, HP_ALL)), _full((MLA_HEADS * V_ROWS, MLA_KV_RANK)),
            pl.BlockSpec((tm, 2 * HEAD_PAD), lambda i: (i % nblk, 0)),
            pl.BlockSpec((2 * HEAD_PAD, tm), lambda i: (0, i % nblk)),
            _full((MLA_HEADS * V_ROWS, 1)),
        ],
        out_specs=[feat(HEAD_PAD), tok(HP_ALL), feat(V_ROWS)] + [tok(NAT_W)] * 3,
        out_shape=outs,
        compiler_params=_cparams("parallel"),
        name="ev_in",
    )(h, g, p["w_in"], p["q_norm"], p["kv_norm"], p["w_uq_t"], p["w_uq_rot_t"], p["w_uk"], p["w_uv_t"],
      tab, tab_t, p["v_one"])


def _mla_kernel(qt_ref, k_ref, vt_ref, o_ref, s_scr, *, tk, nk, unroll):
    qt = qt_ref[0]
    tq = qt.shape[1]

    def scores(j):
        off = pl.multiple_of(j * tk, tk)
        return _dot(k_ref[pl.ds(off, tk), :], qt)

    s_scr[0] = scores(0)

    def body(jj, carry):
        m_prev, acc = carry
        for i in range(unroll):
            j = jj * unroll + i
            s_scr[(i + 1) % 2] = scores(jnp.minimum(j + 1, nk - 1))
            st = s_scr[i % 2]
            m_new = jnp.maximum(m_prev, jnp.max(st, axis=0, keepdims=True))
            alpha = jnp.exp2(m_prev - m_new)
            pt = jnp.exp2(st - m_new).astype(BF16)
            off = pl.multiple_of(j * tk, tk)
            acc = alpha * acc + _dot(vt_ref[0, :, pl.ds(off, tk)], pt)
            m_prev = m_new
        return m_prev, acc

    m_init = jnp.full((1, tq), jnp.finfo(F32).min, F32)
    acc = jnp.zeros((V_ROWS, tq), F32)
    _, acc = lax.fori_loop(0, nk // unroll, body, (m_init, acc))
    out_t = acc[:MLA_V] / acc[MLA_V:MLA_V + 1]
    out_t = jnp.concatenate([out_t, jnp.zeros((HEAD_PAD - MLA_V, tq), F32)], axis=0)
    o_ref[...] = out_t.T.astype(BF16)


def _mla(qt, k, vt, batch, seq_len, *, tq=512, tk=512, unroll=8):
    m = k.shape[0]
    tk = min(tk, seq_len // unroll)
    nq = seq_len // tq
    nk = seq_len // tk
    assert unroll % 2 == 0 and nk % unroll == 0
    return pl.pallas_call(
        functools.partial(_mla_kernel, tk=tk, nk=nk, unroll=unroll),
        grid=(batch, MLA_HEADS, nq),
        in_specs=[
            pl.BlockSpec((1, HEAD_PAD, tq), lambda b, h, i: (h, 0, b * nq + i)),
            pl.BlockSpec((seq_len, HEAD_PAD), lambda b, h, i: (b, h)),
            pl.BlockSpec((1, V_ROWS, seq_len), lambda b, h, i: (h, 0, b)),
        ],
        out_specs=pl.BlockSpec((tq, HEAD_PAD), lambda b, h, i: (b * nq + i, h)),
        out_shape=jax.ShapeDtypeStruct((m, HP_ALL), BF16),
        scratch_shapes=[pltpu.VMEM((2, tk, tq), F32)],
        compiler_params=_cparams("parallel", "parallel", "arbitrary"),
        name="mla",
    )(qt, k, vt)


NAT_LANES = 2 * NAT_HEAD_DIM


def _nat_kernel(q_ref, k_ref, v_ref, bias_ref, o_ref, *, rows, kh, rblk):
    i = pl.program_id(2)
    lane = lax.broadcasted_iota(jnp.int32, (GRID_W, NAT_LANES), 1)
    head0 = lane < NAT_HEAD_DIM

    def row_body(rr, carry):
        r = i * rblk + rr
        start = jnp.clip(r - kh // 2, 0, rows - kh)
        delta = r - start
        koff = pl.multiple_of(start * GRID_W, GRID_W)
        qoff = pl.multiple_of(rr * GRID_W, GRID_W)
        qrow = q_ref[pl.ds(qoff, GRID_W), :]
        kwin = k_ref[pl.ds(koff, kh * GRID_W), :]
        vwin = v_ref[pl.ds(koff, kh * GRID_W), :]
        zero = jnp.zeros_like(qrow)
        q2 = jnp.concatenate([jnp.where(head0, qrow, zero), jnp.where(head0, zero, qrow)], axis=0)
        st = _dot_nt(kwin, q2) + bias_ref[delta, 0]
        e = jnp.exp(st - jnp.max(st, axis=0, keepdims=True))
        pt = (e / jnp.sum(e, axis=0, keepdims=True)).astype(BF16)
        o2 = lax.dot_general(pt, vwin, (((0,), (0,)), ((), ())), preferred_element_type=F32)
        o_ref[pl.ds(qoff, GRID_W), :] = jnp.where(head0, o2[:GRID_W], o2[GRID_W:]).astype(BF16)
        return carry

    lax.fori_loop(0, rblk, row_body, 0, unroll=True)


def _nat(nq, nk, nv, bias, batch, seq_len, *, rblk=32):
    m = nq.shape[0]
    rows = seq_len // GRID_W
    kh = min(NAT_KH_MAX, rows)
    rblk = min(rblk, rows)
    nblk = rows // rblk
    tq = rblk * GRID_W
    return pl.pallas_call(
        functools.partial(_nat_kernel, rows=rows, kh=kh, rblk=rblk),
        grid=(batch, NAT_HEADS // 2, nblk),
        in_specs=[
            pl.BlockSpec((tq, NAT_LANES), lambda b, hp, i: (b * nblk + i, hp)),
            pl.BlockSpec((seq_len, NAT_LANES), lambda b, hp, i: (b, hp)),
            pl.BlockSpec((seq_len, NAT_LANES), lambda b, hp, i: (b, hp)),
            pl.BlockSpec((kh, 1, kh * GRID_W, 2 * GRID_W), lambda b, hp, i: (0, hp, 0, 0)),
        ],
        out_specs=pl.BlockSpec((tq, NAT_LANES), lambda b, hp, i: (b * nblk + i, hp)),
        out_shape=jax.ShapeDtypeStruct((m, NAT_W), BF16),
        compiler_params=_cparams("parallel", "parallel", "arbitrary"),
        name="nat",
    )(nq, nk, nv, bias)


def _mix_out_kernel(h_ref, a_ref, b_ref, wa_ref, wb_ref, g_ref, o_ref):
    mix = _dot(a_ref[...], wa_ref[...]) + _dot(b_ref[...], wb_ref[...])
    o_ref[...] = h_ref[...] + _rms(mix, g_ref[...])


def _ev_out(h, a, b, wa, wb, g, *, tm=512):
    m = h.shape[0]
    tok = lambda w: pl.BlockSpec((tm, w), lambda i: (i, 0))
    return pl.pallas_call(
        _mix_out_kernel,
        grid=(m // tm,),
        in_specs=[tok(D_MODEL), tok(a.shape[1]), tok(b.shape[1]),
                  _full(wa.shape), _full(wb.shape), _full((1, D_MODEL))],
        out_specs=tok(D_MODEL),
        out_shape=jax.ShapeDtypeStruct((m, D_MODEL), F32),
        compiler_params=_cparams("parallel"),
        name="ev_out",
    )(h, a, b, wa, wb, g)


def _od_in_kernel(h_ref, g_ref, win_ref, u_ref, su_ref):
    m = _rms(h_ref[...], g_ref[...]).astype(BF16)
    z = _dot(m, win_ref[...])
    ca = z[:, :CONV_CH]
    cg = z[:, CONV_CH:2 * CONV_CH]
    u_ref[...] = ca * jax.nn.sigmoid(cg)
    su_ref[...] = z[:, 2 * CONV_CH:]


def _od_in(h, g, w_in, *, tm=512):
    m = h.shape[0]
    tok = lambda w: pl.BlockSpec((tm, w), lambda i: (i, 0))
    return pl.pallas_call(
        _od_in_kernel,
        grid=(m // tm,),
        in_specs=[tok(D_MODEL), _full((1, D_MODEL)), _full(w_in.shape)],
        out_specs=[tok(CONV_CH), tok(S5_CH)],
        out_shape=[jax.ShapeDtypeStruct((m, CONV_CH), F32), jax.ShapeDtypeStruct((m, S5_CH), F32)],
        compiler_params=_cparams("parallel"),
        name="od_in",
    )(h, g, w_in)


CONV_SUB = 64


def _conv_kernel(prev_ref, cur_ref, next_ref, w_ref, b_ref, lg_ref, lb_ref, o_ref, scr, *, tm, nblk):
    i = pl.program_id(0)
    first = (i % nblk) == 0
    last = (i % nblk) == nblk - 1
    scr[0:CONV_HALO, :] = jnp.where(first, 0.0, prev_ref[...])
    scr[CONV_HALO:CONV_HALO + tm, :] = cur_ref[...]
    scr[CONV_HALO + tm:, :] = jnp.where(last, 0.0, next_ref[...])
    w = w_ref[...]
    shift = CONV_HALO - CONV_WIDTH // 2

    def sub(c, carry):
        base = pl.multiple_of(c * CONV_SUB, CONV_SUB)
        cols = []
        for lb in range(CONV_CH // LANES):
            ls = slice(lb * LANES, (lb + 1) * LANES)
            win = scr[pl.ds(base, CONV_SUB + 2 * CONV_HALO), ls]
            acc = jnp.zeros((CONV_SUB, LANES), F32)
            nwin = CONV_SUB + 2 * CONV_HALO
            for b in range(SUBLANES):
                wb = pltpu.roll(win, nwin - b, 0) if b else win
                for a in range(2 * CONV_HALO // SUBLANES):
                    kk = SUBLANES * a + b - shift
                    if 0 <= kk < CONV_WIDTH:
                        acc = acc + wb[SUBLANES * a:SUBLANES * a + CONV_SUB, :] * w[kk:kk + 1, ls]
            cols.append(acc)
        y = jnp.concatenate(cols, axis=1) + b_ref[...]
        mu = jnp.mean(y, axis=-1, keepdims=True)
        yc = y - mu
        yn = yc * lax.rsqrt(jnp.mean(yc * yc, axis=-1, keepdims=True) + NORM_EPS)
        yn = yn * lg_ref[...] + lb_ref[...]
        o_ref[pl.ds(base, CONV_SUB), :] = (yn * jax.nn.sigmoid(yn)).astype(BF16)
        return carry

    lax.fori_loop(0, tm // CONV_SUB, sub, 0)


def _conv(u, w, b, lg, lb, seq_len, *, tm=512):
    m = u.shape[0]
    nblk = seq_len // tm
    hb = tm // CONV_HALO
    nh = m // CONV_HALO
    return pl.pallas_call(
        functools.partial(_conv_kernel, tm=tm, nblk=nblk),
        grid=(m // tm,),
        in_specs=[
            pl.BlockSpec((CONV_HALO, CONV_CH), lambda i: (jnp.maximum(i * hb - 1, 0), 0)),
            pl.BlockSpec((tm, CONV_CH), lambda i: (i, 0)),
            pl.BlockSpec((CONV_HALO, CONV_CH), lambda i: (jnp.minimum((i + 1) * hb, nh - 1), 0)),
            _full((CONV_WIDTH, CONV_CH)), _full((1, CONV_CH)), _full((1, CONV_CH)), _full((1, CONV_CH)),
        ],
        out_specs=pl.BlockSpec((tm, CONV_CH), lambda i: (i, 0)),
        out_shape=jax.ShapeDtypeStruct((m, CONV_CH), BF16),
        scratch_shapes=[pltpu.VMEM((tm + 2 * CONV_HALO, CONV_CH), F32)],
        compiler_params=_cparams("parallel"),
        name="conv",
    )(u, u, u, w, b, lg, lb)


S5_W = 2 * S5_GROUP * S5_CHUNK
S5_X = 2 * S5_STATE
S5_PIECE = 2 * S5_GROUP
S5_NPP = LANES // S5_PIECE
S5_VMEM_LIMIT = 56 * 1024 * 1024


def _s5_kernel(su_ref, m_ref, b_ref, c_ref, pw_ref, y_ref, u_scr, y_scr, *, nchunk):
    nstep = nchunk.bit_length() - 1
    slot = lax.broadcasted_iota(jnp.int32, (nchunk, LANES), 1) // S5_PIECE
    row = lax.broadcasted_iota(jnp.int32, (nchunk, S5_X), 0)

    def token_rows(ref, sigma):
        return ref.at[pl.ds(sigma, nchunk, stride=S5_CHUNK), :]

    def move(x, src, dst):
        return x if src == dst else pltpu.roll(x, ((dst - src) % S5_NPP) * S5_PIECE, 1)

    for col in range(S5_W // LANES):
        acc = [None] * S5_NPP
        for s in range(S5_NPP):
            x = token_rows(su_ref, S5_NPP * col + s)[...]
            for pp in range(S5_NPP):
                r = move(x, pp, s)
                acc[pp] = r if s == 0 else jnp.where(slot == s, r, acc[pp])
        for pp in range(S5_NPP):
            u_scr[pp, :, col * LANES:(col + 1) * LANES] = acc[pp].astype(BF16)

    for pp in range(S5_NPP):
        u = u_scr[pp]
        contrib = _dot(u, b_ref[pp])
        states = []
        for d in range(2):
            xr = contrib[:, (2 * d) * S5_X:(2 * d + 1) * S5_X]
            xi = contrib[:, (2 * d + 1) * S5_X:(2 * d + 2) * S5_X]

            def shifted(x, sh):
                if d == 0:
                    return jnp.where(row >= sh, pltpu.roll(x, sh, 0), 0.0)
                return jnp.where(row < nchunk - sh, pltpu.roll(x, nchunk - sh, 0), 0.0)

            for k in range(nstep):
                ar = pw_ref[pp, d, k, 0:1, :]
                ai = pw_ref[pp, d, k, 1:2, :]
                sr = shifted(xr, 1 << k)
                si = shifted(xi, 1 << k)
                xr, xi = xr + ar * sr - ai * si, xi + ar * si + ai * sr
            states += [shifted(xr, 1), shifted(xi, 1)]
        x = jnp.concatenate(states, axis=1).astype(BF16)
        y_scr[pp] = _dot(u, m_ref[pp]) + _dot(x, c_ref[pp])

    for col in range(S5_W // LANES):
        ys = [y_scr[pp, :, col * LANES:(col + 1) * LANES] for pp in range(S5_NPP)]
        for s in range(S5_NPP):
            out = move(ys[0], s, 0)
            for pp in range(1, S5_NPP):
                out = jnp.where(slot == pp, move(ys[pp], s, pp), out)
            token_rows(y_ref, S5_NPP * col + s)[...] = out


def _s5(su, ops, batch, seq_len):
    m = su.shape[0]
    nchunk = seq_len // S5_CHUNK
    assert nchunk & (nchunk - 1) == 0
    nstep = max(nchunk.bit_length() - 1, 1)
    seq_blk = pl.BlockSpec((seq_len, LANES), lambda q, b: (b, q), pipeline_mode=pl.Buffered(1))
    per_q = lambda *shape: pl.BlockSpec((S5_NPP,) + shape, lambda q, b: (q,) + (0,) * len(shape))
    return pl.pallas_call(
        functools.partial(_s5_kernel, nchunk=nchunk),
        grid=(S5_CH // LANES, batch),
        in_specs=[seq_blk, per_q(S5_W, S5_W), per_q(S5_W, 4 * S5_X), per_q(4 * S5_X, S5_W),
                  per_q(2, nstep, 2, S5_X)],
        out_specs=seq_blk,
        out_shape=jax.ShapeDtypeStruct((m, S5_CH), F32),
        scratch_shapes=[pltpu.VMEM((S5_NPP, nchunk, S5_W), BF16), pltpu.VMEM((S5_NPP, nchunk, S5_W), F32)],
        compiler_params=pltpu.CompilerParams(dimension_semantics=("parallel", "arbitrary"),
                                             vmem_limit_bytes=S5_VMEM_LIMIT),
        name="s5",
    )(su, ops["m"], ops["b"], ops["c"], ops["pw"][nchunk])


def _od_out_kernel(h_ref, c_ref, su_ref, ys_ref, d_ref, wglu_ref, wc_ref, ws_ref, g_ref, o_ref):
    y = d_ref[...] * su_ref[...] + ys_ref[...]
    z = jax.nn.gelu(y, approximate=True)
    sg = (z * jax.nn.sigmoid(_dot(z.astype(BF16), wglu_ref[...]))).astype(BF16)
    mix = _dot(c_ref[...], wc_ref[...]) + _dot(sg, ws_ref[...])
    o_ref[...] = h_ref[...] + _rms(mix, g_ref[...])


def _od_out(h, c, su, ys, d, wglu, wc, ws, g, *, tm=512):
    m = h.shape[0]
    tok = lambda w: pl.BlockSpec((tm, w), lambda i: (i, 0))
    return pl.pallas_call(
        _od_out_kernel,
        grid=(m // tm,),
        in_specs=[tok(D_MODEL), tok(CONV_CH), tok(S5_CH), tok(S5_CH), _full((1, S5_CH)),
                  _full(wglu.shape), _full(wc.shape), _full(ws.shape), _full((1, D_MODEL))],
        out_specs=tok(D_MODEL),
        out_shape=jax.ShapeDtypeStruct((m, D_MODEL), F32),
        compiler_params=_cparams("parallel"),
        name="od_out",
    )(h, c, su, ys, d, wglu, wc, ws, g)


def _pad_heads(w, lo, hi, width):
    k = w.shape[0]
    w = w.reshape(k, MLA_HEADS, -1)[:, :, lo:hi]
    return jnp.pad(w, ((0, 0), (0, 0), (0, width - (hi - lo)))).reshape(k, MLA_HEADS * width)


def _rot_cols(w):
    half = w.shape[-1] // 2
    return jnp.concatenate([-w[..., half:], w[..., :half]], axis=-1)


def _rope_tables(seq_len):
    half = MLA_ROPE // 2
    inv = ROPE_THETA ** (-jnp.arange(half, dtype=F32) / half)
    ang = jnp.arange(seq_len, dtype=F32)[:, None] * inv[None, :]
    cos = jnp.concatenate([jnp.cos(ang)] * 2, axis=1)
    sin = jnp.concatenate([jnp.sin(ang)] * 2, axis=1)
    scale = (MLA_NOPE + MLA_ROPE) ** -0.5 * math.log2(math.e)
    z64 = jnp.zeros((seq_len, MLA_NOPE), F32)
    z32 = jnp.zeros((seq_len, HEAD_PAD - MLA_NOPE - MLA_ROPE), F32)
    cq = jnp.concatenate([z64 + scale, cos * scale, z32], axis=1)
    sq = jnp.concatenate([z64, sin * scale, z32], axis=1)
    ck = jnp.concatenate([z64, cos, z32], axis=1)
    sk = jnp.concatenate([z64, sin, z32], axis=1)
    return jnp.concatenate([ck, sk], axis=1), jnp.concatenate([cq, sq], axis=1).T


def _nat_bias(rpb, kh):
    c = np.arange(GRID_W)
    col_start = np.clip(c - NAT_KW // 2, 0, GRID_W - NAT_KW)
    col_ok = (c[None, :] >= col_start[:, None]) & (c[None, :] < col_start[:, None] + NAT_KW)
    col_off = np.clip(c[None, :] - c[:, None], -(NAT_KW - 1), NAT_KW - 1) + (NAT_KW - 1)
    delta = np.arange(kh)
    row_off = np.arange(kh)[None, :] - delta[:, None] + (NAT_KH_MAX - 1)
    row_sel = (row_off[:, :, None] == np.arange(2 * NAT_KH_MAX - 1)).astype(np.float32)
    col_sel = (col_off[:, :, None] == np.arange(2 * NAT_KW - 1)).astype(np.float32)
    bias = jnp.einsum("hrc,djr,qkc->dhqjk", rpb.astype(F32), row_sel, col_sel, precision=lax.Precision.HIGHEST)
    bias = jnp.where(col_ok[None, None, :, None, :], bias, NEG_INF)
    bias = bias.reshape(kh, NAT_HEADS // 2, 2 * GRID_W, kh * GRID_W)
    return bias.transpose(0, 1, 3, 2)


def _pair_diag(x, spec, rows, cols):
    x = x.reshape((S5_PAIRS, 2) + x.shape[1:])
    return jnp.einsum(spec, x, jnp.eye(2, dtype=x.dtype)).reshape(S5_PAIRS, rows, cols)


def _s5_operators(lam_re, lam_im, log_step, b_re, b_im, c_re, c_im, nchunks):
    t = S5_CHUNK
    w = S5_GROUP * t
    dt = jnp.exp(log_step)[:, :, None]
    ar, ai = lam_re * dt, lam_im * dt
    er = jnp.exp(ar)
    lbr, lbi = er * jnp.cos(ai), er * jnp.sin(ai)
    den = lam_re * lam_re + lam_im * lam_im
    fr = ((lbr - 1.0) * lam_re + lbi * lam_im) / den
    fi = (lbi * lam_re - (lbr - 1.0) * lam_im) / den
    bbr = fr[..., None] * b_re - fi[..., None] * b_im
    bbi = fr[..., None] * b_im + fi[..., None] * b_re

    def power(d):
        d = d.astype(F32)[None, None, :, None]
        mag = jnp.exp(ar[:, :, None, :] * d)
        return mag * jnp.cos(ai[:, :, None, :] * d), mag * jnp.sin(ai[:, :, None, :] * d)

    hi = lax.Precision.HIGHEST
    pr, pi = power(jnp.arange(t + 1))
    wr = c_re[:, :, None] * pr[:, :, :, None, :] - c_im[:, :, None] * pi[:, :, :, None, :]
    wi = c_re[:, :, None] * pi[:, :, :, None, :] + c_im[:, :, None] * pr[:, :, :, None, :]
    kmat = (jnp.einsum("xgdcp,xgpk->xgdck", wr[:, :, :t], bbr, precision=hi)
            - jnp.einsum("xgdcp,xgpk->xgdck", wi[:, :, :t], bbi, precision=hi))
    kf, kr = kmat[0], kmat[1]
    kcat = jnp.concatenate([kr[:, :0:-1], kf[:, :1] + kr[:, :1], kf[:, 1:]], axis=1)
    piece = 2 * S5_GROUP
    kc = _pair_diag(kcat, "padck,ab->pakdbc", piece, (2 * t - 1) * piece)
    mmat = jnp.concatenate([kc[:, :, (t - 1 - s) * piece:(2 * t - 1 - s) * piece] for s in range(t)], axis=1)

    def contrib(pr_, pi_, br_, bi_):
        brt, bit = br_.transpose(0, 2, 1)[:, None], bi_.transpose(0, 2, 1)[:, None]
        re = pr_[:, :, None, :] * brt - pi_[:, :, None, :] * bit
        im = pr_[:, :, None, :] * bit + pi_[:, :, None, :] * brt
        return [_pair_diag(x, "paxys,ab->pxaybs", 2 * w, S5_X) for x in (re, im)]

    bmat = jnp.concatenate(
        contrib(pr[0, :, t - 1::-1], pi[0, :, t - 1::-1], bbr[0], bbi[0])
        + contrib(pr[1, :, :t], pi[1, :, :t], bbr[1], bbi[1]), axis=2)

    def readout(w_):
        return _pair_diag(w_, "patcs,ab->pastbc", S5_X, 2 * w)

    cmat = jnp.concatenate([
        readout(wr[0, :, 1:t + 1]), readout(-wi[0, :, 1:t + 1]),
        readout(wr[1, :, t:0:-1]), readout(-wi[1, :, t:0:-1]),
    ], axis=1)

    pws = {}
    for nchunk in nchunks:
        nstep = max(nchunk.bit_length() - 1, 1)
        qr, qi = power(t * (2 ** jnp.arange(nstep)))

        def lanes(q):
            q = q.reshape(2, S5_PAIRS, 2, nstep, S5_STATE).transpose(1, 0, 3, 2, 4)
            return q.reshape(S5_PAIRS, 2, nstep, S5_X)

        pws[nchunk] = jnp.stack([lanes(qr), lanes(qi)], axis=3)
    return {"m": mmat.astype(BF16), "b": bmat.astype(BF16), "c": cmat.astype(BF16), "pw": pws}


def _even_params(ev_w_in, q_norm, kv_norm, w_uq, w_ukv, rpb, ev_w_out, seq_lens):
    c0 = MLA_Q_RANK
    c1 = c0 + MLA_KV_RANK
    c2 = c1 + MLA_ROPE
    w_kr = ev_w_in[:, c1:c2]
    pad_l = jnp.zeros((D_MODEL, MLA_NOPE), F32)
    pad_r = jnp.zeros((D_MODEL, HEAD_PAD - MLA_NOPE - MLA_ROPE), F32)
    w_in = jnp.concatenate([
        ev_w_in[:, :c1],
        pad_l, w_kr, pad_r,
        pad_l, _rot_cols(w_kr), pad_r,
        ev_w_in[:, c2:],
    ], axis=1).astype(BF16)
    uq = w_uq.reshape(MLA_Q_RANK, MLA_HEADS, MLA_NOPE + MLA_ROPE)
    uq_rot = jnp.concatenate([jnp.zeros_like(uq[..., :MLA_NOPE]), _rot_cols(uq[..., MLA_NOPE:])], axis=-1)
    qk_dim = MLA_NOPE + MLA_ROPE
    v_one = np.zeros((MLA_HEADS, V_ROWS), np.float32)
    v_one[:, MLA_V] = 1.0
    nat_bias = {kh: _nat_bias(rpb, kh) for kh in {min(NAT_KH_MAX, sl // GRID_W) for sl in seq_lens}}
    wa = ev_w_out[:MLA_HEADS * MLA_V].reshape(MLA_HEADS, MLA_V, D_MODEL)
    wa = jnp.pad(wa, ((0, 0), (0, HEAD_PAD - MLA_V), (0, 0)))
    return {
        "w_in": w_in,
        "q_norm": q_norm[None], "kv_norm": kv_norm[None],
        "w_uq_t": _pad_heads(w_uq, 0, qk_dim, HEAD_PAD).T.astype(BF16),
        "w_uq_rot_t": _pad_heads(uq_rot.reshape(MLA_Q_RANK, -1), 0, qk_dim, HEAD_PAD).T.astype(BF16),
        "w_uk": _pad_heads(w_ukv, 0, MLA_NOPE, HEAD_PAD).astype(BF16),
        "w_uv_t": _pad_heads(w_ukv, MLA_NOPE, MLA_NOPE + MLA_V, V_ROWS).T.astype(BF16),
        "v_one": jnp.asarray(v_one.reshape(MLA_HEADS * V_ROWS, 1)),
        "rope_tab": {sl: _rope_tables(sl) for sl in seq_lens},
        "nat_bias": {sl: nat_bias[min(NAT_KH_MAX, sl // GRID_W)] for sl in seq_lens},
        "wa": wa.reshape(HP_ALL, D_MODEL).astype(BF16),
        "wb": ev_w_out[MLA_HEADS * MLA_V:].astype(BF16),
    }


def _even_mixer(h, g_pre, g_post, p, batch, seq_len):
    qt, k, vt, nq, nk, nv = _ev_in(h, g_pre, p, seq_len)
    a = _mla(qt, k, vt, batch, seq_len)
    b = _nat(nq, nk, nv, p["nat_bias"][seq_len], batch, seq_len)
    return _ev_out(h, a, b, p["wa"], p["wb"], g_post)


def _odd_mixer(h, g_pre, g_post, p, batch, seq_len):
    u, su = _od_in(h, g_pre, p["w_in"])
    c = _conv(u, p["dw_w"], p["dw_b"], p["ln_g"], p["ln_b"], seq_len)
    ys = _s5(su, p["s5"], batch, seq_len)
    return _od_out(h, c, su, ys, p["d"], p["w_glu"], p["wc"], p["ws"], g_post)


def kernel(x_prompt, x_sample, norm_g, ffn_w_gate, ffn_w_up, ffn_w_down, ev_w_in, mla_q_norm, mla_kv_norm, mla_w_uq, mla_w_ukv, nat_rpb, ev_w_out, od_w_in, conv_dw_w, conv_dw_b, conv_ln_g, conv_ln_b, s5_lambda_re, s5_lambda_im, s5_log_step, s5_b_re, s5_b_im, s5_c_re, s5_c_im, s5_d, s5_w_glu, od_w_out):
    depth = norm_g.shape[0]
    seq_lens = sorted({x_prompt.shape[1], x_sample.shape[1]})
    ffn_w = [[tuple(w[layer, j].astype(BF16) for w in (ffn_w_gate, ffn_w_up, ffn_w_down)) for j in range(2)]
             for layer in range(depth)]
    mixers = []
    for layer in range(depth):
        i = layer // 2
        if layer % 2 == 0:
            mixers.append(_even_params(ev_w_in[i], mla_q_norm[i], mla_kv_norm[i], mla_w_uq[i], mla_w_ukv[i],
                                       nat_rpb[i], ev_w_out[i], seq_lens))
        else:
            mixers.append({
                "w_in": od_w_in[i].astype(BF16),
                "dw_w": conv_dw_w[i], "dw_b": conv_dw_b[i][None],
                "ln_g": conv_ln_g[i][None], "ln_b": conv_ln_b[i][None],
                "s5": _s5_operators(s5_lambda_re[i], s5_lambda_im[i], s5_log_step[i], s5_b_re[i], s5_b_im[i],
                                    s5_c_re[i], s5_c_im[i], [sl // S5_CHUNK for sl in seq_lens]),
                "d": s5_d[i][None],
                "w_glu": s5_w_glu[i].astype(BF16),
                "wc": od_w_out[i][:CONV_CH].astype(BF16),
                "ws": od_w_out[i][CONV_CH:].astype(BF16),
            })

    gains = [[norm_g[layer, i][None] for i in range(norm_g.shape[1])] for layer in range(depth)]

    def trunk(x):
        batch, seq_len, _ = x.shape
        h = x.reshape(batch * seq_len, D_MODEL)
        for layer in range(depth):
            g = gains[layer]
            h = _ffn(h, g[0], *ffn_w[layer][0], g[1])
            mixer = _even_mixer if layer % 2 == 0 else _odd_mixer
            h = mixer(h, g[2], g[3], mixers[layer], batch, seq_len)
            h = _ffn(h, g[4], *ffn_w[layer][1], g[5])
        return h.reshape(batch, seq_len, D_MODEL)

    return (trunk(x_prompt), trunk(x_sample))
```

```python
import functools
import math

import jax
import jax.numpy as jnp
import numpy as np
from jax import lax
from jax.experimental import pallas as pl
from jax.experimental.pallas import tpu as pltpu

F32 = jnp.float32
BF16 = jnp.bfloat16

LANES = 128
SUBLANES = 8

D_MODEL = 1024
D_FF = 2816
GRID_W = 64

MLA_HEADS = 8
MLA_Q_RANK = 256
MLA_KV_RANK = 128
MLA_NOPE = 64
MLA_ROPE = 32
MLA_V = 64
ROPE_THETA = 10000.0
HEAD_PAD = 128
V_ROWS = 80
MLA_HPB = LANES // MLA_V

NAT_HEADS = 8
NAT_HEAD_DIM = 64
NAT_W = NAT_HEADS * NAT_HEAD_DIM
NAT_KH_MAX = 8
NAT_KW = 16

CONV_CH = 512
CONV_WIDTH = 31
CONV_HALO = 16

S5_CH = 512
S5_GROUP = 16
S5_GROUPS = S5_CH // S5_GROUP
S5_STATE = 64
S5_CHUNK = 16
S5_PAIRS = S5_GROUPS // 2

FFN_RES_SCALE = 0.5
NORM_EPS = 1e-6
NEG_INF = -1e30

VMEM_LIMIT = 48 * 1024 * 1024
FFN_VMEM_LIMIT = 56 * 1024 * 1024


def _cparams(*sem):
    return pltpu.CompilerParams(dimension_semantics=sem, vmem_limit_bytes=VMEM_LIMIT)


def _rms(x, g):
    return x * lax.rsqrt(jnp.mean(x * x, axis=-1, keepdims=True) + NORM_EPS) * g


def _dot(a, b):
    return jnp.dot(a, b, preferred_element_type=F32)


def _dot_nt(a, b):
    return lax.dot_general(a, b, (((1,), (1,)), ((), ())), preferred_element_type=F32)


def _full(shape):
    n = len(shape)
    return pl.BlockSpec(shape, lambda *_: (0,) * n)


def _ffn_kernel(x_ref, gpre_ref, wg_ref, wu_ref, wd_ref, gpost_ref, o_ref):
    x = x_ref[...]
    xn = _rms(x, gpre_ref[...]).astype(BF16)
    gate = _dot(xn, wg_ref[...])
    up = _dot(xn, wu_ref[...])
    hdn = (gate * jax.nn.sigmoid(gate) * up).astype(BF16)
    o_ref[...] = x + FFN_RES_SCALE * _rms(_dot(hdn, wd_ref[...]), gpost_ref[...])


def _ffn(h, gpre, wg, wu, wd, gpost, layer, j, *, tm=512):
    m = h.shape[0]
    resident = lambda shape: pl.BlockSpec((None, None) + shape, lambda i: (layer, j, 0, 0),
                                          pipeline_mode=pl.Buffered(1))
    return pl.pallas_call(
        _ffn_kernel,
        grid=(m // tm,),
        in_specs=[
            pl.BlockSpec((tm, D_MODEL), lambda i: (i, 0)),
            _full((1, D_MODEL)),
            resident((D_MODEL, D_FF)), resident((D_MODEL, D_FF)), resident((D_FF, D_MODEL)),
            _full((1, D_MODEL)),
        ],
        out_specs=pl.BlockSpec((tm, D_MODEL), lambda i: (i, 0)),
        out_shape=jax.ShapeDtypeStruct((m, D_MODEL), F32),
        compiler_params=pltpu.CompilerParams(dimension_semantics=("parallel",), vmem_limit_bytes=FFN_VMEM_LIMIT),
        name="ffn",
    )(h, gpre, wg, wu, wd, gpost)


EV_Z = MLA_Q_RANK + MLA_KV_RANK + 2 * HEAD_PAD + 3 * NAT_W
HP_ALL = MLA_HEADS * HEAD_PAD


def _ev_in_kernel(h_ref, g_ref, win_ref, qn_ref, kvn_ref, wuq_ref, wuqr_ref, wuk_ref, wuv_ref,
                  tab_ref, tabt_ref, vone_ref, qt_ref, k_ref, vt_ref, nq_ref, nk_ref, nv_ref):
    m = _rms(h_ref[...], g_ref[...]).astype(BF16)
    z = _dot(m, win_ref[...])
    c0 = MLA_Q_RANK
    c1 = c0 + MLA_KV_RANK
    c2 = c1 + HEAD_PAD
    c3 = c2 + HEAD_PAD
    q_lat = z[:, :c0]
    kv_lat = z[:, c0:c1]
    kr = z[:, c1:c2]
    kr_rot = z[:, c2:c3]
    nq_ref[...] = (z[:, c3:c3 + NAT_W] * (NAT_HEAD_DIM ** -0.5)).astype(BF16)
    nk_ref[...] = z[:, c3 + NAT_W:c3 + 2 * NAT_W].astype(BF16)
    nv_ref[...] = z[:, c3 + 2 * NAT_W:c3 + 3 * NAT_W].astype(BF16)

    qn = _rms(q_lat, qn_ref[...]).astype(BF16)
    kvn = _rms(kv_lat, kvn_ref[...]).astype(BF16)
    q_raw_t = _dot_nt(wuq_ref[...], qn)
    q_rot_t = _dot_nt(wuqr_ref[...], qn)
    v_t = _dot_nt(wuv_ref[...], kvn) + vone_ref[...]
    k_nope = _dot(kvn, wuk_ref[...])

    tab = tab_ref[...]
    k_rope = kr * tab[:, :HEAD_PAD] + kr_rot * tab[:, HEAD_PAD:]
    cq_t = tabt_ref[:HEAD_PAD, :]
    sq_t = tabt_ref[HEAD_PAD:, :]
    for hd in range(MLA_HEADS):
        sl = slice(hd * HEAD_PAD, (hd + 1) * HEAD_PAD)
        qt_ref[hd] = (q_raw_t[sl] * cq_t + q_rot_t[sl] * sq_t).astype(BF16)
        k_ref[:, sl] = (k_nope[:, sl] + k_rope).astype(BF16)
        vt_ref[hd] = v_t[hd * V_ROWS:(hd + 1) * V_ROWS].astype(BF16)


def _ev_in(h, g, p, seq_len, *, tm=512):
    m = h.shape[0]
    nblk = seq_len // tm
    tok = lambda w: pl.BlockSpec((tm, w), lambda i: (i, 0))
    feat = lambda r: pl.BlockSpec((MLA_HEADS, r, tm), lambda i: (0, 0, i))
    outs = [jax.ShapeDtypeStruct((MLA_HEADS, HEAD_PAD, m), BF16), jax.ShapeDtypeStruct((m, HP_ALL), BF16),
            jax.ShapeDtypeStruct((MLA_HEADS, V_ROWS, m), BF16)] + [jax.ShapeDtypeStruct((m, NAT_W), BF16)] * 3
    tab, tab_t = p["rope_tab"][seq_len]
    return pl.pallas_call(
        _ev_in_kernel,
        grid=(m // tm,),
        in_specs=[
            tok(D_MODEL), _full((1, D_MODEL)), _full((D_MODEL, EV_Z)),
            _full((1, MLA_Q_RANK)), _full((1, MLA_KV_RANK)),
            _full((HP_ALL, MLA_Q_RANK)), _full((HP_ALL, MLA_Q_RANK)),
            _full((MLA_KV_RANK, HP_ALL)), _full((MLA_HEADS * V_ROWS, MLA_KV_RANK)),
            pl.BlockSpec((tm, 2 * HEAD_PAD), lambda i: (i % nblk, 0)),
            pl.BlockSpec((2 * HEAD_PAD, tm), lambda i: (0, i % nblk)),
            _full((MLA_HEADS * V_ROWS, 1)),
        ],
        out_specs=[feat(HEAD_PAD), tok(HP_ALL), feat(V_ROWS)] + [tok(NAT_W)] * 3,
        out_shape=outs,
        compiler_params=_cparams("parallel"),
        name="ev_in",
    )(h, g, p["w_in"], p["q_norm"], p["kv_norm"], p["w_uq_t"], p["w_uq_rot_t"], p["w_uk"], p["w_uv_t"],
      tab, tab_t, p["v_one"])


def _mla_kernel(qt_ref, k_ref, vt_ref, o_ref, s_scr, *, tk, nk, unroll):
    tq = qt_ref.shape[2]
    outs = []
    for hh in range(MLA_HPB):
        qt = qt_ref[hh]
        head = slice(hh * HEAD_PAD, (hh + 1) * HEAD_PAD)

        def scores(j):
            off = pl.multiple_of(j * tk, tk)
            return _dot(k_ref[pl.ds(off, tk), head], qt)

        s_scr[0] = scores(0)

        def body(jj, carry):
            m_prev, acc = carry
            for i in range(unroll):
                j = jj * unroll + i
                s_scr[(i + 1) % 2] = scores(jnp.minimum(j + 1, nk - 1))
                st = s_scr[i % 2]
                m_new = jnp.maximum(m_prev, jnp.max(st, axis=0, keepdims=True))
                alpha = jnp.exp2(m_prev - m_new)
                pt = jnp.exp2(st - m_new).astype(BF16)
                off = pl.multiple_of(j * tk, tk)
                acc = alpha * acc + _dot(vt_ref[hh, :, pl.ds(off, tk)], pt)
                m_prev = m_new
            return m_prev, acc

        m_init = jnp.full((1, tq), jnp.finfo(F32).min, F32)
        acc = jnp.zeros((V_ROWS, tq), F32)
        _, acc = lax.fori_loop(0, nk // unroll, body, (m_init, acc))
        outs.append(acc[:MLA_V] / acc[MLA_V:MLA_V + 1])
    o_ref[...] = jnp.concatenate(outs, axis=0).T.astype(BF16)


def _mla(qt, k, vt, batch, seq_len, *, tq=512, tk=512, unroll=8):
    m = k.shape[0]
    tk = min(tk, seq_len // unroll)
    nq = seq_len // tq
    nk = seq_len // tk
    assert unroll % 2 == 0 and nk % unroll == 0
    return pl.pallas_call(
        functools.partial(_mla_kernel, tk=tk, nk=nk, unroll=unroll),
        grid=(batch, MLA_HEADS // MLA_HPB, nq),
        in_specs=[
            pl.BlockSpec((MLA_HPB, HEAD_PAD, tq), lambda b, h, i: (h, 0, b * nq + i)),
            pl.BlockSpec((seq_len, MLA_HPB * HEAD_PAD), lambda b, h, i: (b, h)),
            pl.BlockSpec((MLA_HPB, V_ROWS, seq_len), lambda b, h, i: (h, 0, b)),
        ],
        out_specs=pl.BlockSpec((tq, MLA_HPB * MLA_V), lambda b, h, i: (b * nq + i, h)),
        out_shape=jax.ShapeDtypeStruct((m, MLA_HEADS * MLA_V), BF16),
        scratch_shapes=[pltpu.VMEM((2, tk, tq), F32)],
        compiler_params=_cparams("parallel", "parallel", "arbitrary"),
        name="mla",
    )(qt, k, vt)


NAT_LANES = 2 * NAT_HEAD_DIM


def _nat_kernel(q_ref, k_ref, v_ref, bias_ref, o_ref, *, rows, kh, rblk):
    i = pl.program_id(2)
    lane = lax.broadcasted_iota(jnp.int32, (GRID_W, NAT_LANES), 1)
    head0 = lane < NAT_HEAD_DIM

    def row_body(rr, carry):
        r = i * rblk + rr
        start = jnp.clip(r - kh // 2, 0, rows - kh)
        delta = r - start
        koff = pl.multiple_of(start * GRID_W, GRID_W)
        qoff = pl.multiple_of(rr * GRID_W, GRID_W)
        qrow = q_ref[pl.ds(qoff, GRID_W), :]
        kwin = k_ref[pl.ds(koff, kh * GRID_W), :]
        vwin = v_ref[pl.ds(koff, kh * GRID_W), :]
        zero = jnp.zeros_like(qrow)
        q2 = jnp.concatenate([jnp.where(head0, qrow, zero), jnp.where(head0, zero, qrow)], axis=0)
        st = _dot_nt(kwin, q2) + bias_ref[delta, 0]
        e = jnp.exp(st - jnp.max(st, axis=0, keepdims=True))
        pt = (e / jnp.sum(e, axis=0, keepdims=True)).astype(BF16)
        o2 = lax.dot_general(pt, vwin, (((0,), (0,)), ((), ())), preferred_element_type=F32)
        o_ref[pl.ds(qoff, GRID_W), :] = jnp.where(head0, o2[:GRID_W], o2[GRID_W:]).astype(BF16)
        return carry

    lax.fori_loop(0, rblk, row_body, 0, unroll=True)


def _nat(nq, nk, nv, bias, batch, seq_len, *, rblk=32):
    m = nq.shape[0]
    rows = seq_len // GRID_W
    kh = min(NAT_KH_MAX, rows)
    rblk = min(rblk, rows)
    nblk = rows // rblk
    tq = rblk * GRID_W
    return pl.pallas_call(
        functools.partial(_nat_kernel, rows=rows, kh=kh, rblk=rblk),
        grid=(batch, NAT_HEADS // 2, nblk),
        in_specs=[
            pl.BlockSpec((tq, NAT_LANES), lambda b, hp, i: (b * nblk + i, hp)),
            pl.BlockSpec((seq_len, NAT_LANES), lambda b, hp, i: (b, hp)),
            pl.BlockSpec((seq_len, NAT_LANES), lambda b, hp, i: (b, hp)),
            pl.BlockSpec((kh, 1, kh * GRID_W, 2 * GRID_W), lambda b, hp, i: (0, hp, 0, 0)),
        ],
        out_specs=pl.BlockSpec((tq, NAT_LANES), lambda b, hp, i: (b * nblk + i, hp)),
        out_shape=jax.ShapeDtypeStruct((m, NAT_W), BF16),
        compiler_params=_cparams("parallel", "parallel", "arbitrary"),
        name="nat",
    )(nq, nk, nv, bias)


def _mix_out_kernel(h_ref, a_ref, b_ref, wa_ref, wb_ref, g_ref, o_ref):
    mix = _dot(a_ref[...], wa_ref[...]) + _dot(b_ref[...], wb_ref[...])
    o_ref[...] = h_ref[...] + _rms(mix, g_ref[...])


def _ev_out(h, a, b, wa, wb, g, *, tm=512):
    m = h.shape[0]
    tok = lambda w: pl.BlockSpec((tm, w), lambda i: (i, 0))
    return pl.pallas_call(
        _mix_out_kernel,
        grid=(m // tm,),
        in_specs=[tok(D_MODEL), tok(a.shape[1]), tok(b.shape[1]),
                  _full(wa.shape), _full(wb.shape), _full((1, D_MODEL))],
        out_specs=tok(D_MODEL),
        out_shape=jax.ShapeDtypeStruct((m, D_MODEL), F32),
        compiler_params=_cparams("parallel"),
        name="ev_out",
    )(h, a, b, wa, wb, g)


def _od_in_kernel(h_ref, g_ref, win_ref, u_ref, su_ref):
    m = _rms(h_ref[...], g_ref[...]).astype(BF16)
    z = _dot(m, win_ref[...])
    ca = z[:, :CONV_CH]
    cg = z[:, CONV_CH:2 * CONV_CH]
    u_ref[...] = ca * jax.nn.sigmoid(cg)
    su_ref[...] = z[:, 2 * CONV_CH:]


def _od_in(h, g, w_in, *, tm=512):
    m = h.shape[0]
    tok = lambda w: pl.BlockSpec((tm, w), lambda i: (i, 0))
    return pl.pallas_call(
        _od_in_kernel,
        grid=(m // tm,),
        in_specs=[tok(D_MODEL), _full((1, D_MODEL)), _full(w_in.shape)],
        out_specs=[tok(CONV_CH), tok(S5_CH)],
        out_shape=[jax.ShapeDtypeStruct((m, CONV_CH), F32), jax.ShapeDtypeStruct((m, S5_CH), F32)],
        compiler_params=_cparams("parallel"),
        name="od_in",
    )(h, g, w_in)


CONV_SUB = 64


def _conv_kernel(prev_ref, cur_ref, next_ref, w_ref, b_ref, lg_ref, lb_ref, o_ref, scr, *, tm, nblk):
    i = pl.program_id(0)
    first = (i % nblk) == 0
    last = (i % nblk) == nblk - 1
    scr[0:CONV_HALO, :] = jnp.where(first, 0.0, prev_ref[...])
    scr[CONV_HALO:CONV_HALO + tm, :] = cur_ref[...]
    scr[CONV_HALO + tm:, :] = jnp.where(last, 0.0, next_ref[...])
    w = w_ref[...]
    shift = CONV_HALO - CONV_WIDTH // 2

    def sub(c, carry):
        base = pl.multiple_of(c * CONV_SUB, CONV_SUB)
        cols = []
        for lb in range(CONV_CH // LANES):
            ls = slice(lb * LANES, (lb + 1) * LANES)
            win = scr[pl.ds(base, CONV_SUB + 2 * CONV_HALO), ls]
            acc = jnp.zeros((CONV_SUB, LANES), F32)
            nwin = CONV_SUB + 2 * CONV_HALO
            for b in range(SUBLANES):
                wb = pltpu.roll(win, nwin - b, 0) if b else win
                for a in range(2 * CONV_HALO // SUBLANES):
                    kk = SUBLANES * a + b - shift
                    if 0 <= kk < CONV_WIDTH:
                        acc = acc + wb[SUBLANES * a:SUBLANES * a + CONV_SUB, :] * w[kk:kk + 1, ls]
            cols.append(acc)
        y = jnp.concatenate(cols, axis=1) + b_ref[...]
        mu = jnp.mean(y, axis=-1, keepdims=True)
        yc = y - mu
        yn = yc * lax.rsqrt(jnp.mean(yc * yc, axis=-1, keepdims=True) + NORM_EPS)
        yn = yn * lg_ref[...] + lb_ref[...]
        o_ref[pl.ds(base, CONV_SUB), :] = (yn * jax.nn.sigmoid(yn)).astype(BF16)
        return carry

    lax.fori_loop(0, tm // CONV_SUB, sub, 0)


def _conv(u, w, b, lg, lb, seq_len, *, tm=512):
    m = u.shape[0]
    nblk = seq_len // tm
    hb = tm // CONV_HALO
    nh = m // CONV_HALO
    return pl.pallas_call(
        functools.partial(_conv_kernel, tm=tm, nblk=nblk),
        grid=(m // tm,),
        in_specs=[
            pl.BlockSpec((CONV_HALO, CONV_CH), lambda i: (jnp.maximum(i * hb - 1, 0), 0)),
            pl.BlockSpec((tm, CONV_CH), lambda i: (i, 0)),
            pl.BlockSpec((CONV_HALO, CONV_CH), lambda i: (jnp.minimum((i + 1) * hb, nh - 1), 0)),
            _full((CONV_WIDTH, CONV_CH)), _full((1, CONV_CH)), _full((1, CONV_CH)), _full((1, CONV_CH)),
        ],
        out_specs=pl.BlockSpec((tm, CONV_CH), lambda i: (i, 0)),
        out_shape=jax.ShapeDtypeStruct((m, CONV_CH), BF16),
        scratch_shapes=[pltpu.VMEM((tm + 2 * CONV_HALO, CONV_CH), F32)],
        compiler_params=_cparams("parallel"),
        name="conv",
    )(u, u, u, w, b, lg, lb)


S5_W = 2 * S5_GROUP * S5_CHUNK
S5_X = 2 * S5_STATE
S5_PIECE = 2 * S5_GROUP
S5_NPP = LANES // S5_PIECE
S5_VMEM_LIMIT = 56 * 1024 * 1024


def _s5_kernel(su_ref, m_ref, b_ref, c_ref, pw_ref, y_ref, u_scr, y_scr, *, nchunk):
    nstep = nchunk.bit_length() - 1
    slot = lax.broadcasted_iota(jnp.int32, (nchunk, LANES), 1) // S5_PIECE
    row = lax.broadcasted_iota(jnp.int32, (nchunk, S5_X), 0)

    def token_rows(ref, sigma):
        return ref.at[pl.ds(sigma, nchunk, stride=S5_CHUNK), :]

    def move(x, src, dst):
        return x if src == dst else pltpu.roll(x, ((dst - src) % S5_NPP) * S5_PIECE, 1)

    for col in range(S5_W // LANES):
        acc = [None] * S5_NPP
        for s in range(S5_NPP):
            x = token_rows(su_ref, S5_NPP * col + s)[...]
            for pp in range(S5_NPP):
                r = move(x, pp, s)
                acc[pp] = r if s == 0 else jnp.where(slot == s, r, acc[pp])
        for pp in range(S5_NPP):
            u_scr[pp, :, col * LANES:(col + 1) * LANES] = acc[pp].astype(BF16)

    for pp in range(S5_NPP):
        u = u_scr[pp]
        contrib = _dot(u, b_ref[pp])
        states = []
        for d in range(2):
            xr = contrib[:, (2 * d) * S5_X:(2 * d + 1) * S5_X]
            xi = contrib[:, (2 * d + 1) * S5_X:(2 * d + 2) * S5_X]

            def shifted(x, sh):
                if d == 0:
                    return jnp.where(row >= sh, pltpu.roll(x, sh, 0), 0.0)
                return jnp.where(row < nchunk - sh, pltpu.roll(x, nchunk - sh, 0), 0.0)

            for k in range(nstep):
                ar = pw_ref[pp, d, k, 0:1, :]
                ai = pw_ref[pp, d, k, 1:2, :]
                sr = shifted(xr, 1 << k)
                si = shifted(xi, 1 << k)
                xr, xi = xr + ar * sr - ai * si, xi + ar * si + ai * sr
            states += [shifted(xr, 1), shifted(xi, 1)]
        x = jnp.concatenate(states, axis=1).astype(BF16)
        y_scr[pp] = _dot(u, m_ref[pp]) + _dot(x, c_ref[pp])

    for col in range(S5_W // LANES):
        ys = [y_scr[pp, :, col * LANES:(col + 1) * LANES] for pp in range(S5_NPP)]
        for s in range(S5_NPP):
            out = move(ys[0], s, 0)
            for pp in range(1, S5_NPP):
                out = jnp.where(slot == pp, move(ys[pp], s, pp), out)
            token_rows(y_ref, S5_NPP * col + s)[...] = out


def _s5(su, ops, batch, seq_len):
    m = su.shape[0]
    nchunk = seq_len // S5_CHUNK
    assert nchunk & (nchunk - 1) == 0
    nstep = max(nchunk.bit_length() - 1, 1)
    seq_blk = pl.BlockSpec((seq_len, LANES), lambda q, b: (b, q), pipeline_mode=pl.Buffered(1))
    per_q = lambda *shape: pl.BlockSpec((S5_NPP,) + shape, lambda q, b: (q,) + (0,) * len(shape))
    return pl.pallas_call(
        functools.partial(_s5_kernel, nchunk=nchunk),
        grid=(S5_CH // LANES, batch),
        in_specs=[seq_blk, per_q(S5_W, S5_W), per_q(S5_W, 4 * S5_X), per_q(4 * S5_X, S5_W),
                  per_q(2, nstep, 2, S5_X)],
        out_specs=seq_blk,
        out_shape=jax.ShapeDtypeStruct((m, S5_CH), F32),
        scratch_shapes=[pltpu.VMEM((S5_NPP, nchunk, S5_W), BF16), pltpu.VMEM((S5_NPP, nchunk, S5_W), F32)],
        compiler_params=pltpu.CompilerParams(dimension_semantics=("parallel", "arbitrary"),
                                             vmem_limit_bytes=S5_VMEM_LIMIT),
        name="s5",
    )(su, ops["m"], ops["b"], ops["c"], ops["pw"][nchunk])


def _od_out_kernel(h_ref, c_ref, su_ref, ys_ref, d_ref, wglu_ref, wc_ref, ws_ref, g_ref, o_ref):
    y = d_ref[...] * su_ref[...] + ys_ref[...]
    z = jax.nn.gelu(y, approximate=True)
    sg = (z * jax.nn.sigmoid(_dot(z.astype(BF16), wglu_ref[...]))).astype(BF16)
    mix = _dot(c_ref[...], wc_ref[...]) + _dot(sg, ws_ref[...])
    o_ref[...] = h_ref[...] + _rms(mix, g_ref[...])


def _od_out(h, c, su, ys, d, wglu, wc, ws, g, *, tm=512):
    m = h.shape[0]
    tok = lambda w: pl.BlockSpec((tm, w), lambda i: (i, 0))
    return pl.pallas_call(
        _od_out_kernel,
        grid=(m // tm,),
        in_specs=[tok(D_MODEL), tok(CONV_CH), tok(S5_CH), tok(S5_CH), _full((1, S5_CH)),
                  _full(wglu.shape), _full(wc.shape), _full(ws.shape), _full((1, D_MODEL))],
        out_specs=tok(D_MODEL),
        out_shape=jax.ShapeDtypeStruct((m, D_MODEL), F32),
        compiler_params=_cparams("parallel"),
        name="od_out",
    )(h, c, su, ys, d, wglu, wc, ws, g)


def _pad_heads(w, lo, hi, width):
    k = w.shape[0]
    w = w.reshape(k, MLA_HEADS, -1)[:, :, lo:hi]
    return jnp.pad(w, ((0, 0), (0, 0), (0, width - (hi - lo)))).reshape(k, MLA_HEADS * width)


def _rot_cols(w):
    half = w.shape[-1] // 2
    return jnp.concatenate([-w[..., half:], w[..., :half]], axis=-1)


def _rope_tables(seq_len):
    half = MLA_ROPE // 2
    inv = ROPE_THETA ** (-jnp.arange(half, dtype=F32) / half)
    ang = jnp.arange(seq_len, dtype=F32)[:, None] * inv[None, :]
    cos = jnp.concatenate([jnp.cos(ang)] * 2, axis=1)
    sin = jnp.concatenate([jnp.sin(ang)] * 2, axis=1)
    scale = (MLA_NOPE + MLA_ROPE) ** -0.5 * math.log2(math.e)
    z64 = jnp.zeros((seq_len, MLA_NOPE), F32)
    z32 = jnp.zeros((seq_len, HEAD_PAD - MLA_NOPE - MLA_ROPE), F32)
    cq = jnp.concatenate([z64 + scale, cos * scale, z32], axis=1)
    sq = jnp.concatenate([z64, sin * scale, z32], axis=1)
    ck = jnp.concatenate([z64, cos, z32], axis=1)
    sk = jnp.concatenate([z64, sin, z32], axis=1)
    return jnp.concatenate([ck, sk], axis=1), jnp.concatenate([cq, sq], axis=1).T


def _nat_bias(rpb, kh):
    c = np.arange(GRID_W)
    col_start = np.clip(c - NAT_KW // 2, 0, GRID_W - NAT_KW)
    col_ok = (c[None, :] >= col_start[:, None]) & (c[None, :] < col_start[:, None] + NAT_KW)
    col_off = np.clip(c[None, :] - c[:, None], -(NAT_KW - 1), NAT_KW - 1) + (NAT_KW - 1)
    delta = np.arange(kh)
    row_off = np.arange(kh)[None, :] - delta[:, None] + (NAT_KH_MAX - 1)
    row_sel = (row_off[:, :, None] == np.arange(2 * NAT_KH_MAX - 1)).astype(np.float32)
    col_sel = (col_off[:, :, None] == np.arange(2 * NAT_KW - 1)).astype(np.float32)
    bias = jnp.einsum("hrc,djr,qkc->dhqjk", rpb.astype(F32), row_sel, col_sel, precision=lax.Precision.HIGHEST)
    bias = jnp.where(col_ok[None, None, :, None, :], bias, NEG_INF)
    bias = bias.reshape(kh, NAT_HEADS // 2, 2 * GRID_W, kh * GRID_W)
    return bias.transpose(0, 1, 3, 2)


def _pair_diag(x, spec, rows, cols):
    x = x.reshape((S5_PAIRS, 2) + x.shape[1:])
    return jnp.einsum(spec, x, jnp.eye(2, dtype=x.dtype)).reshape(S5_PAIRS, rows, cols)


def _s5_operators(lam_re, lam_im, log_step, b_re, b_im, c_re, c_im, nchunks):
    t = S5_CHUNK
    w = S5_GROUP * t
    dt = jnp.exp(log_step)[:, :, None]
    ar, ai = lam_re * dt, lam_im * dt
    er = jnp.exp(ar)
    lbr, lbi = er * jnp.cos(ai), er * jnp.sin(ai)
    den = lam_re * lam_re + lam_im * lam_im
    fr = ((lbr - 1.0) * lam_re + lbi * lam_im) / den
    fi = (lbi * lam_re - (lbr - 1.0) * lam_im) / den
    bbr = fr[..., None] * b_re - fi[..., None] * b_im
    bbi = fr[..., None] * b_im + fi[..., None] * b_re

    def power(d):
        d = d.astype(F32)[None, None, :, None]
        mag = jnp.exp(ar[:, :, None, :] * d)
        return mag * jnp.cos(ai[:, :, None, :] * d), mag * jnp.sin(ai[:, :, None, :] * d)

    hi = lax.Precision.HIGHEST
    pr, pi = power(jnp.arange(t + 1))
    wr = c_re[:, :, None] * pr[:, :, :, None, :] - c_im[:, :, None] * pi[:, :, :, None, :]
    wi = c_re[:, :, None] * pi[:, :, :, None, :] + c_im[:, :, None] * pr[:, :, :, None, :]
    kmat = (jnp.einsum("xgdcp,xgpk->xgdck", wr[:, :, :t], bbr, precision=hi)
            - jnp.einsum("xgdcp,xgpk->xgdck", wi[:, :, :t], bbi, precision=hi))
    kf, kr = kmat[0], kmat[1]
    kcat = jnp.concatenate([kr[:, :0:-1], kf[:, :1] + kr[:, :1], kf[:, 1:]], axis=1)
    piece = 2 * S5_GROUP
    kc = _pair_diag(kcat, "padck,ab->pakdbc", piece, (2 * t - 1) * piece)
    mmat = jnp.concatenate([kc[:, :, (t - 1 - s) * piece:(2 * t - 1 - s) * piece] for s in range(t)], axis=1)

    def contrib(pr_, pi_, br_, bi_):
        brt, bit = br_.transpose(0, 2, 1)[:, None], bi_.transpose(0, 2, 1)[:, None]
        re = pr_[:, :, None, :] * brt - pi_[:, :, None, :] * bit
        im = pr_[:, :, None, :] * bit + pi_[:, :, None, :] * brt
        return [_pair_diag(x, "paxys,ab->pxaybs", 2 * w, S5_X) for x in (re, im)]

    bmat = jnp.concatenate(
        contrib(pr[0, :, t - 1::-1], pi[0, :, t - 1::-1], bbr[0], bbi[0])
        + contrib(pr[1, :, :t], pi[1, :, :t], bbr[1], bbi[1]), axis=2)

    def readout(w_):
        return _pair_diag(w_, "patcs,ab->pastbc", S5_X, 2 * w)

    cmat = jnp.concatenate([
        readout(wr[0, :, 1:t + 1]), readout(-wi[0, :, 1:t + 1]),
        readout(wr[1, :, t:0:-1]), readout(-wi[1, :, t:0:-1]),
    ], axis=1)

    pws = {}
    for nchunk in nchunks:
        nstep = max(nchunk.bit_length() - 1, 1)
        qr, qi = power(t * (2 ** jnp.arange(nstep)))

        def lanes(q):
            q = q.reshape(2, S5_PAIRS, 2, nstep, S5_STATE).transpose(1, 0, 3, 2, 4)
            return q.reshape(S5_PAIRS, 2, nstep, S5_X)

        pws[nchunk] = jnp.stack([lanes(qr), lanes(qi)], axis=3)
    return {"m": mmat.astype(BF16), "b": bmat.astype(BF16), "c": cmat.astype(BF16), "pw": pws}


def _even_params(ev_w_in, q_norm, kv_norm, w_uq, w_ukv, rpb, ev_w_out, seq_lens):
    c0 = MLA_Q_RANK
    c1 = c0 + MLA_KV_RANK
    c2 = c1 + MLA_ROPE
    w_kr = ev_w_in[:, c1:c2]
    pad_l = jnp.zeros((D_MODEL, MLA_NOPE), F32)
    pad_r = jnp.zeros((D_MODEL, HEAD_PAD - MLA_NOPE - MLA_ROPE), F32)
    w_in = jnp.concatenate([
        ev_w_in[:, :c1],
        pad_l, w_kr, pad_r,
        pad_l, _rot_cols(w_kr), pad_r,
        ev_w_in[:, c2:],
    ], axis=1).astype(BF16)
    uq = w_uq.reshape(MLA_Q_RANK, MLA_HEADS, MLA_NOPE + MLA_ROPE)
    uq_rot = jnp.concatenate([jnp.zeros_like(uq[..., :MLA_NOPE]), _rot_cols(uq[..., MLA_NOPE:])], axis=-1)
    qk_dim = MLA_NOPE + MLA_ROPE
    v_one = np.zeros((MLA_HEADS, V_ROWS), np.float32)
    v_one[:, MLA_V] = 1.0
    nat_bias = {kh: _nat_bias(rpb, kh) for kh in {min(NAT_KH_MAX, sl // GRID_W) for sl in seq_lens}}
    return {
        "w_in": w_in,
        "q_norm": q_norm[None], "kv_norm": kv_norm[None],
        "w_uq_t": _pad_heads(w_uq, 0, qk_dim, HEAD_PAD).T.astype(BF16),
        "w_uq_rot_t": _pad_heads(uq_rot.reshape(MLA_Q_RANK, -1), 0, qk_dim, HEAD_PAD).T.astype(BF16),
        "w_uk": _pad_heads(w_ukv, 0, MLA_NOPE, HEAD_PAD).astype(BF16),
        "w_uv_t": _pad_heads(w_ukv, MLA_NOPE, MLA_NOPE + MLA_V, V_ROWS).T.astype(BF16),
        "v_one": jnp.asarray(v_one.reshape(MLA_HEADS * V_ROWS, 1)),
        "rope_tab": {sl: _rope_tables(sl) for sl in seq_lens},
        "nat_bias": {sl: nat_bias[min(NAT_KH_MAX, sl // GRID_W)] for sl in seq_lens},
        "wa": ev_w_out[:MLA_HEADS * MLA_V].astype(BF16),
        "wb": ev_w_out[MLA_HEADS * MLA_V:].astype(BF16),
    }


def _even_mixer(h, g_pre, g_post, p, batch, seq_len):
    qt, k, vt, nq, nk, nv = _ev_in(h, g_pre, p, seq_len)
    a = _mla(qt, k, vt, batch, seq_len)
    b = _nat(nq, nk, nv, p["nat_bias"][seq_len], batch, seq_len)
    return _ev_out(h, a, b, p["wa"], p["wb"], g_post)


def _odd_mixer(h, g_pre, g_post, p, batch, seq_len):
    u, su = _od_in(h, g_pre, p["w_in"])
    c = _conv(u, p["dw_w"], p["dw_b"], p["ln_g"], p["ln_b"], seq_len)
    ys = _s5(su, p["s5"], batch, seq_len)
    return _od_out(h, c, su, ys, p["d"], p["w_glu"], p["wc"], p["ws"], g_post)


def kernel(x_prompt, x_sample, norm_g, ffn_w_gate, ffn_w_up, ffn_w_down, ev_w_in, mla_q_norm, mla_kv_norm, mla_w_uq, mla_w_ukv, nat_rpb, ev_w_out, od_w_in, conv_dw_w, conv_dw_b, conv_ln_g, conv_ln_b, s5_lambda_re, s5_lambda_im, s5_log_step, s5_b_re, s5_b_im, s5_c_re, s5_c_im, s5_d, s5_w_glu, od_w_out):
    depth = norm_g.shape[0]
    seq_lens = sorted({x_prompt.shape[1], x_sample.shape[1]})
    ffn_w = tuple(w.astype(BF16) for w in (ffn_w_gate, ffn_w_up, ffn_w_down))
    mixers = []
    for layer in range(depth):
        i = layer // 2
        if layer % 2 == 0:
            mixers.append(_even_params(ev_w_in[i], mla_q_norm[i], mla_kv_norm[i], mla_w_uq[i], mla_w_ukv[i],
                                       nat_rpb[i], ev_w_out[i], seq_lens))
        else:
            mixers.append({
                "w_in": od_w_in[i].astype(BF16),
                "dw_w": conv_dw_w[i], "dw_b": conv_dw_b[i][None],
                "ln_g": conv_ln_g[i][None], "ln_b": conv_ln_b[i][None],
                "s5": _s5_operators(s5_lambda_re[i], s5_lambda_im[i], s5_log_step[i], s5_b_re[i], s5_b_im[i],
                                    s5_c_re[i], s5_c_im[i], [sl // S5_CHUNK for sl in seq_lens]),
                "d": s5_d[i][None],
                "w_glu": s5_w_glu[i].astype(BF16),
                "wc": od_w_out[i][:CONV_CH].astype(BF16),
                "ws": od_w_out[i][CONV_CH:].astype(BF16),
            })

    gains = [[norm_g[layer, i][None] for i in range(norm_g.shape[1])] for layer in range(depth)]

    def trunk(x):
        batch, seq_len, _ = x.shape
        h = x.reshape(batch * seq_len, D_MODEL)
        for layer in range(depth):
            g = gains[layer]
            h = _ffn(h, g[0], *ffn_w, g[1], layer, 0)
            mixer = _even_mixer if layer % 2 == 0 else _odd_mixer
            h = mixer(h, g[2], g[3], mixers[layer], batch, seq_len)
            h = _ffn(h, g[4], *ffn_w, g[5], layer, 1)
        return h.reshape(batch, seq_len, D_MODEL)

    return (trunk(x_prompt), trunk(x_sample))
```

```python
import functools
import math

import jax
import jax.numpy as jnp
import numpy as np
from jax import lax
from jax.experimental import pallas as pl
from jax.experimental.pallas import tpu as pltpu

F32 = jnp.float32
BF16 = jnp.bfloat16

LANES = 128
SUBLANES = 8

D_MODEL = 1024
D_FF = 2816
GRID_W = 64

MLA_HEADS = 8
MLA_Q_RANK = 256
MLA_KV_RANK = 128
MLA_NOPE = 64
MLA_ROPE = 32
MLA_V = 64
ROPE_THETA = 10000.0
HEAD_PAD = 128
V_ROWS = 80
MLA_HPB = LANES // MLA_V

NAT_HEADS = 8
NAT_HEAD_DIM = 64
NAT_W = NAT_HEADS * NAT_HEAD_DIM
NAT_KH_MAX = 8
NAT_KW = 16

CONV_CH = 512
CONV_WIDTH = 31
CONV_HALO = 16

S5_CH = 512
S5_GROUP = 16
S5_GROUPS = S5_CH // S5_GROUP
S5_STATE = 64
S5_CHUNK = 16
S5_PAIRS = S5_GROUPS // 2

FFN_RES_SCALE = 0.5
NORM_EPS = 1e-6
NEG_INF = -1e30

VMEM_LIMIT = 48 * 1024 * 1024
FFN_VMEM_LIMIT = 56 * 1024 * 1024


def _cparams(*sem):
    return pltpu.CompilerParams(dimension_semantics=sem, vmem_limit_bytes=VMEM_LIMIT)


def _rms(x, g):
    return x * lax.rsqrt(jnp.mean(x * x, axis=-1, keepdims=True) + NORM_EPS) * g


def _dot(a, b):
    return jnp.dot(a, b, preferred_element_type=F32)


def _dot_nt(a, b):
    return lax.dot_general(a, b, (((1,), (1,)), ((), ())), preferred_element_type=F32)


def _full(shape):
    n = len(shape)
    return pl.BlockSpec(shape, lambda *_: (0,) * n)


def _ffn_kernel(*refs, mix_fn, n_tok, n_const):
    h_ref = refs[0]
    gpre_ref, wg_ref, wu_ref, wd_ref, gpost_ref, o_ref = refs[-6:]
    x = h_ref[...]
    if mix_fn is not None:
        mix = mix_fn(*refs[1:1 + n_tok + n_const])
        x = x + _rms(mix, refs[1 + n_tok + n_const][...])
    xn = _rms(x, gpre_ref[...]).astype(BF16)
    gate = _dot(xn, wg_ref[...])
    up = _dot(xn, wu_ref[...])
    hdn = (gate * jax.nn.sigmoid(gate) * up).astype(BF16)
    o_ref[...] = x + FFN_RES_SCALE * _rms(_dot(hdn, wd_ref[...]), gpost_ref[...])


def _ffn(h, gpre, wg, wu, wd, gpost, layer, j, mix=None, *, tm=512):
    m = h.shape[0]
    mix_fn, toks, consts, gmix = mix if mix is not None else (None, (), (), None)
    tok = lambda w: pl.BlockSpec((tm, w), lambda i: (i, 0))
    resident = lambda shape: pl.BlockSpec((None, None) + shape, lambda i: (layer, j, 0, 0),
                                          pipeline_mode=pl.Buffered(1))
    mix_args = [*toks, *consts] + ([gmix] if mix_fn is not None else [])
    mix_specs = [tok(t.shape[1]) for t in toks] + [_full(c.shape) for c in consts] + [
        _full((1, D_MODEL))] * (mix_fn is not None)
    return pl.pallas_call(
        functools.partial(_ffn_kernel, mix_fn=mix_fn, n_tok=len(toks), n_const=len(consts)),
        grid=(m // tm,),
        in_specs=[tok(D_MODEL), *mix_specs, _full((1, D_MODEL)),
                  resident((D_MODEL, D_FF)), resident((D_MODEL, D_FF)), resident((D_FF, D_MODEL)),
                  _full((1, D_MODEL))],
        out_specs=tok(D_MODEL),
        out_shape=jax.ShapeDtypeStruct((m, D_MODEL), F32),
        compiler_params=pltpu.CompilerParams(dimension_semantics=("parallel",), vmem_limit_bytes=FFN_VMEM_LIMIT),
        name="ffn",
    )(h, *mix_args, gpre, wg, wu, wd, gpost)


EV_Z = MLA_Q_RANK + MLA_KV_RANK + 2 * HEAD_PAD + 3 * NAT_W
HP_ALL = MLA_HEADS * HEAD_PAD


def _ev_in_kernel(h_ref, g_ref, win_ref, qn_ref, kvn_ref, wuq_ref, wuqr_ref, wuk_ref, wuv_ref,
                  tab_ref, tabt_ref, vone_ref, qt_ref, k_ref, vt_ref, nq_ref, nk_ref, nv_ref):
    m = _rms(h_ref[...], g_ref[...]).astype(BF16)
    z = _dot(m, win_ref[...])
    c0 = MLA_Q_RANK
    c1 = c0 + MLA_KV_RANK
    c2 = c1 + HEAD_PAD
    c3 = c2 + HEAD_PAD
    q_lat = z[:, :c0]
    kv_lat = z[:, c0:c1]
    kr = z[:, c1:c2]
    kr_rot = z[:, c2:c3]
    nq_ref[...] = (z[:, c3:c3 + NAT_W] * (NAT_HEAD_DIM ** -0.5)).astype(BF16)
    nk_ref[...] = z[:, c3 + NAT_W:c3 + 2 * NAT_W].astype(BF16)
    nv_ref[...] = z[:, c3 + 2 * NAT_W:c3 + 3 * NAT_W].astype(BF16)

    qn = _rms(q_lat, qn_ref[...]).astype(BF16)
    kvn = _rms(kv_lat, kvn_ref[...]).astype(BF16)
    q_raw_t = _dot_nt(wuq_ref[...], qn)
    q_rot_t = _dot_nt(wuqr_ref[...], qn)
    v_t = _dot_nt(wuv_ref[...], kvn) + vone_ref[...]
    k_nope = _dot(kvn, wuk_ref[...])

    tab = tab_ref[...]
    k_rope = kr * tab[:, :HEAD_PAD] + kr_rot * tab[:, HEAD_PAD:]
    cq_t = tabt_ref[:HEAD_PAD, :]
    sq_t = tabt_ref[HEAD_PAD:, :]
    for hd in range(MLA_HEADS):
        sl = slice(hd * HEAD_PAD, (hd + 1) * HEAD_PAD)
        qt_ref[hd] = (q_raw_t[sl] * cq_t + q_rot_t[sl] * sq_t).astype(BF16)
        k_ref[:, sl] = (k_nope[:, sl] + k_rope).astype(BF16)
        vt_ref[hd] = v_t[hd * V_ROWS:(hd + 1) * V_ROWS].astype(BF16)


def _ev_in(h, g, p, seq_len, *, tm=512):
    m = h.shape[0]
    nblk = seq_len // tm
    tok = lambda w: pl.BlockSpec((tm, w), lambda i: (i, 0))
    feat = lambda r: pl.BlockSpec((MLA_HEADS, r, tm), lambda i: (0, 0, i))
    outs = [jax.ShapeDtypeStruct((MLA_HEADS, HEAD_PAD, m), BF16), jax.ShapeDtypeStruct((m, HP_ALL), BF16),
            jax.ShapeDtypeStruct((MLA_HEADS, V_ROWS, m), BF16)] + [jax.ShapeDtypeStruct((m, NAT_W), BF16)] * 3
    tab, tab_t = p["rope_tab"][seq_len]
    return pl.pallas_call(
        _ev_in_kernel,
        grid=(m // tm,),
        in_specs=[
            tok(D_MODEL), _full((1, D_MODEL)), _full((D_MODEL, EV_Z)),
            _full((1, MLA_Q_RANK)), _full((1, MLA_KV_RANK)),
            _full((HP_ALL, MLA_Q_RANK)), _full((HP_ALL, MLA_Q_RANK)),
            _full((MLA_KV_RANK, HP_ALL)), _full((MLA_HEADS * V_ROWS, MLA_KV_RANK)),
            pl.BlockSpec((tm, 2 * HEAD_PAD), lambda i: (i % nblk, 0)),
            pl.BlockSpec((2 * HEAD_PAD, tm), lambda i: (0, i % nblk)),
            _full((MLA_HEADS * V_ROWS, 1)),
        ],
        out_specs=[feat(HEAD_PAD), tok(HP_ALL), feat(V_ROWS)] + [tok(NAT_W)] * 3,
        out_shape=outs,
        compiler_params=_cparams("parallel"),
        name="ev_in",
    )(h, g, p["w_in"], p["q_norm"], p["kv_norm"], p["w_uq_t"], p["w_uq_rot_t"], p["w_uk"], p["w_uv_t"],
      tab, tab_t, p["v_one"])


def _mla_kernel(qt_ref, k_ref, vt_ref, o_ref, s_scr, *, tk, nk, unroll):
    tq = qt_ref.shape[2]
    outs = []
    for hh in range(MLA_HPB):
        qt = qt_ref[hh]
        head = slice(hh * HEAD_PAD, (hh + 1) * HEAD_PAD)

        def scores(j):
            off = pl.multiple_of(j * tk, tk)
            return _dot(k_ref[pl.ds(off, tk), head], qt)

        s_scr[0] = scores(0)

        def body(jj, carry):
            m_prev, acc = carry
            for i in range(unroll):
                j = jj * unroll + i
                s_scr[(i + 1) % 2] = scores(jnp.minimum(j + 1, nk - 1))
                st = s_scr[i % 2]
                m_new = jnp.maximum(m_prev, jnp.max(st, axis=0, keepdims=True))
                alpha = jnp.exp2(m_prev - m_new)
                pt = jnp.exp2(st - m_new).astype(BF16)
                off = pl.multiple_of(j * tk, tk)
                acc = alpha * acc + _dot(vt_ref[hh, :, pl.ds(off, tk)], pt)
                m_prev = m_new
            return m_prev, acc

        m_init = jnp.full((1, tq), jnp.finfo(F32).min, F32)
        acc = jnp.zeros((V_ROWS, tq), F32)
        _, acc = lax.fori_loop(0, nk // unroll, body, (m_init, acc))
        outs.append(acc[:MLA_V] / acc[MLA_V:MLA_V + 1])
    o_ref[...] = jnp.concatenate(outs, axis=0).T.astype(BF16)


def _mla(qt, k, vt, batch, seq_len, *, tq=512, tk=512, unroll=8):
    m = k.shape[0]
    tk = min(tk, seq_len // unroll)
    nq = seq_len // tq
    nk = seq_len // tk
    assert unroll % 2 == 0 and nk % unroll == 0
    return pl.pallas_call(
        functools.partial(_mla_kernel, tk=tk, nk=nk, unroll=unroll),
        grid=(batch, MLA_HEADS // MLA_HPB, nq),
        in_specs=[
            pl.BlockSpec((MLA_HPB, HEAD_PAD, tq), lambda b, h, i: (h, 0, b * nq + i)),
            pl.BlockSpec((seq_len, MLA_HPB * HEAD_PAD), lambda b, h, i: (b, h)),
            pl.BlockSpec((MLA_HPB, V_ROWS, seq_len), lambda b, h, i: (h, 0, b)),
        ],
        out_specs=pl.BlockSpec((tq, MLA_HPB * MLA_V), lambda b, h, i: (b * nq + i, h)),
        out_shape=jax.ShapeDtypeStruct((m, MLA_HEADS * MLA_V), BF16),
        scratch_shapes=[pltpu.VMEM((2, tk, tq), F32)],
        compiler_params=_cparams("parallel", "parallel", "arbitrary"),
        name="mla",
    )(qt, k, vt)


NAT_LANES = 2 * NAT_HEAD_DIM


def _nat_kernel(q_ref, k_ref, v_ref, bias_ref, o_ref, *, rows, kh, rblk):
    i = pl.program_id(2)
    lane = lax.broadcasted_iota(jnp.int32, (GRID_W, NAT_LANES), 1)
    head0 = lane < NAT_HEAD_DIM

    def row_body(rr, carry):
        r = i * rblk + rr
        start = jnp.clip(r - kh // 2, 0, rows - kh)
        delta = r - start
        koff = pl.multiple_of(start * GRID_W, GRID_W)
        qoff = pl.multiple_of(rr * GRID_W, GRID_W)
        qrow = q_ref[pl.ds(qoff, GRID_W), :]
        kwin = k_ref[pl.ds(koff, kh * GRID_W), :]
        vwin = v_ref[pl.ds(koff, kh * GRID_W), :]
        zero = jnp.zeros_like(qrow)
        q2 = jnp.concatenate([jnp.where(head0, qrow, zero), jnp.where(head0, zero, qrow)], axis=0)
        st = _dot_nt(kwin, q2) + bias_ref[delta, 0]
        e = jnp.exp(st - jnp.max(st, axis=0, keepdims=True))
        pt = (e / jnp.sum(e, axis=0, keepdims=True)).astype(BF16)
        o2 = lax.dot_general(pt, vwin, (((0,), (0,)), ((), ())), preferred_element_type=F32)
        o_ref[pl.ds(qoff, GRID_W), :] = jnp.where(head0, o2[:GRID_W], o2[GRID_W:]).astype(BF16)
        return carry

    lax.fori_loop(0, rblk, row_body, 0, unroll=True)


def _nat(nq, nk, nv, bias, batch, seq_len, *, rblk=32):
    m = nq.shape[0]
    rows = seq_len // GRID_W
    kh = min(NAT_KH_MAX, rows)
    rblk = min(rblk, rows)
    nblk = rows // rblk
    tq = rblk * GRID_W
    return pl.pallas_call(
        functools.partial(_nat_kernel, rows=rows, kh=kh, rblk=rblk),
        grid=(batch, NAT_HEADS // 2, nblk),
        in_specs=[
            pl.BlockSpec((tq, NAT_LANES), lambda b, hp, i: (b * nblk + i, hp)),
            pl.BlockSpec((seq_len, NAT_LANES), lambda b, hp, i: (b, hp)),
            pl.BlockSpec((seq_len, NAT_LANES), lambda b, hp, i: (b, hp)),
            pl.BlockSpec((kh, 1, kh * GRID_W, 2 * GRID_W), lambda b, hp, i: (0, hp, 0, 0)),
        ],
        out_specs=pl.BlockSpec((tq, NAT_LANES), lambda b, hp, i: (b * nblk + i, hp)),
        out_shape=jax.ShapeDtypeStruct((m, NAT_W), BF16),
        compiler_params=_cparams("parallel", "parallel", "arbitrary"),
        name="nat",
    )(nq, nk, nv, bias)


def _ev_mix(a_ref, b_ref, wa_ref, wb_ref):
    return _dot(a_ref[...], wa_ref[...]) + _dot(b_ref[...], wb_ref[...])


def _od_in_kernel(h_ref, g_ref, win_ref, u_ref, su_ref):
    m = _rms(h_ref[...], g_ref[...]).astype(BF16)
    z = _dot(m, win_ref[...])
    ca = z[:, :CONV_CH]
    cg = z[:, CONV_CH:2 * CONV_CH]
    u_ref[...] = ca * jax.nn.sigmoid(cg)
    su_ref[...] = z[:, 2 * CONV_CH:]


def _od_in(h, g, w_in, *, tm=512):
    m = h.shape[0]
    tok = lambda w: pl.BlockSpec((tm, w), lambda i: (i, 0))
    return pl.pallas_call(
        _od_in_kernel,
        grid=(m // tm,),
        in_specs=[tok(D_MODEL), _full((1, D_MODEL)), _full(w_in.shape)],
        out_specs=[tok(CONV_CH), tok(S5_CH)],
        out_shape=[jax.ShapeDtypeStruct((m, CONV_CH), F32), jax.ShapeDtypeStruct((m, S5_CH), F32)],
        compiler_params=_cparams("parallel"),
        name="od_in",
    )(h, g, w_in)


CONV_SUB = 64


def _conv_kernel(prev_ref, cur_ref, next_ref, w_ref, b_ref, lg_ref, lb_ref, o_ref, scr, *, tm, nblk):
    i = pl.program_id(0)
    first = (i % nblk) == 0
    last = (i % nblk) == nblk - 1
    scr[0:CONV_HALO, :] = jnp.where(first, 0.0, prev_ref[...])
    scr[CONV_HALO:CONV_HALO + tm, :] = cur_ref[...]
    scr[CONV_HALO + tm:, :] = jnp.where(last, 0.0, next_ref[...])
    w = w_ref[...]
    shift = CONV_HALO - CONV_WIDTH // 2

    def sub(c, carry):
        base = pl.multiple_of(c * CONV_SUB, CONV_SUB)
        cols = []
        for lb in range(CONV_CH // LANES):
            ls = slice(lb * LANES, (lb + 1) * LANES)
            win = scr[pl.ds(base, CONV_SUB + 2 * CONV_HALO), ls]
            acc = jnp.zeros((CONV_SUB, LANES), F32)
            nwin = CONV_SUB + 2 * CONV_HALO
            for b in range(SUBLANES):
                wb = pltpu.roll(win, nwin - b, 0) if b else win
                for a in range(2 * CONV_HALO // SUBLANES):
                    kk = SUBLANES * a + b - shift
                    if 0 <= kk < CONV_WIDTH:
                        acc = acc + wb[SUBLANES * a:SUBLANES * a + CONV_SUB, :] * w[kk:kk + 1, ls]
            cols.append(acc)
        y = jnp.concatenate(cols, axis=1) + b_ref[...]
        mu = jnp.mean(y, axis=-1, keepdims=True)
        yc = y - mu
        yn = yc * lax.rsqrt(jnp.mean(yc * yc, axis=-1, keepdims=True) + NORM_EPS)
        yn = yn * lg_ref[...] + lb_ref[...]
        o_ref[pl.ds(base, CONV_SUB), :] = (yn * jax.nn.sigmoid(yn)).astype(BF16)
        return carry

    lax.fori_loop(0, tm // CONV_SUB, sub, 0)


def _conv(u, w, b, lg, lb, seq_len, *, tm=512):
    m = u.shape[0]
    nblk = seq_len // tm
    hb = tm // CONV_HALO
    nh = m // CONV_HALO
    return pl.pallas_call(
        functools.partial(_conv_kernel, tm=tm, nblk=nblk),
        grid=(m // tm,),
        in_specs=[
            pl.BlockSpec((CONV_HALO, CONV_CH), lambda i: (jnp.maximum(i * hb - 1, 0), 0)),
            pl.BlockSpec((tm, CONV_CH), lambda i: (i, 0)),
            pl.BlockSpec((CONV_HALO, CONV_CH), lambda i: (jnp.minimum((i + 1) * hb, nh - 1), 0)),
            _full((CONV_WIDTH, CONV_CH)), _full((1, CONV_CH)), _full((1, CONV_CH)), _full((1, CONV_CH)),
        ],
        out_specs=pl.BlockSpec((tm, CONV_CH), lambda i: (i, 0)),
        out_shape=jax.ShapeDtypeStruct((m, CONV_CH), BF16),
        scratch_shapes=[pltpu.VMEM((tm + 2 * CONV_HALO, CONV_CH), F32)],
        compiler_params=_cparams("parallel"),
        name="conv",
    )(u, u, u, w, b, lg, lb)


S5_W = 2 * S5_GROUP * S5_CHUNK
S5_X = 2 * S5_STATE
S5_PIECE = 2 * S5_GROUP
S5_NPP = LANES // S5_PIECE
S5_VMEM_LIMIT = 56 * 1024 * 1024


def _s5_kernel(su_ref, m_ref, b_ref, c_ref, pw_ref, y_ref, u_scr, y_scr, *, nchunk):
    nstep = nchunk.bit_length() - 1
    slot = lax.broadcasted_iota(jnp.int32, (nchunk, LANES), 1) // S5_PIECE
    row = lax.broadcasted_iota(jnp.int32, (nchunk, S5_X), 0)

    def token_rows(ref, sigma):
        return ref.at[pl.ds(sigma, nchunk, stride=S5_CHUNK), :]

    def move(x, src, dst):
        return x if src == dst else pltpu.roll(x, ((dst - src) % S5_NPP) * S5_PIECE, 1)

    for col in range(S5_W // LANES):
        acc = [None] * S5_NPP
        for s in range(S5_NPP):
            x = token_rows(su_ref, S5_NPP * col + s)[...]
            for pp in range(S5_NPP):
                r = move(x, pp, s)
                acc[pp] = r if s == 0 else jnp.where(slot == s, r, acc[pp])
        for pp in range(S5_NPP):
            u_scr[pp, :, col * LANES:(col + 1) * LANES] = acc[pp].astype(BF16)

    for pp in range(S5_NPP):
        u = u_scr[pp]
        contrib = _dot(u, b_ref[pp])
        states = []
        for d in range(2):
            xr = contrib[:, (2 * d) * S5_X:(2 * d + 1) * S5_X]
            xi = contrib[:, (2 * d + 1) * S5_X:(2 * d + 2) * S5_X]

            def shifted(x, sh):
                if d == 0:
                    return jnp.where(row >= sh, pltpu.roll(x, sh, 0), 0.0)
                return jnp.where(row < nchunk - sh, pltpu.roll(x, nchunk - sh, 0), 0.0)

            for k in range(nstep):
                ar = pw_ref[pp, d, k, 0:1, :]
                ai = pw_ref[pp, d, k, 1:2, :]
                sr = shifted(xr, 1 << k)
                si = shifted(xi, 1 << k)
                xr, xi = xr + ar * sr - ai * si, xi + ar * si + ai * sr
            states += [shifted(xr, 1), shifted(xi, 1)]
        x = jnp.concatenate(states, axis=1).astype(BF16)
        y_scr[pp] = _dot(u, m_ref[pp]) + _dot(x, c_ref[pp])

    for col in range(S5_W // LANES):
        ys = [y_scr[pp, :, col * LANES:(col + 1) * LANES] for pp in range(S5_NPP)]
        for s in range(S5_NPP):
            out = move(ys[0], s, 0)
            for pp in range(1, S5_NPP):
                out = jnp.where(slot == pp, move(ys[pp], s, pp), out)
            token_rows(y_ref, S5_NPP * col + s)[...] = out


def _s5(su, ops, batch, seq_len):
    m = su.shape[0]
    nchunk = seq_len // S5_CHUNK
    assert nchunk & (nchunk - 1) == 0
    nstep = max(nchunk.bit_length() - 1, 1)
    seq_blk = pl.BlockSpec((seq_len, LANES), lambda q, b: (b, q), pipeline_mode=pl.Buffered(1))
    per_q = lambda *shape: pl.BlockSpec((S5_NPP,) + shape, lambda q, b: (q,) + (0,) * len(shape))
    return pl.pallas_call(
        functools.partial(_s5_kernel, nchunk=nchunk),
        grid=(S5_CH // LANES, batch),
        in_specs=[seq_blk, per_q(S5_W, S5_W), per_q(S5_W, 4 * S5_X), per_q(4 * S5_X, S5_W),
                  per_q(2, nstep, 2, S5_X)],
        out_specs=seq_blk,
        out_shape=jax.ShapeDtypeStruct((m, S5_CH), F32),
        scratch_shapes=[pltpu.VMEM((S5_NPP, nchunk, S5_W), BF16), pltpu.VMEM((S5_NPP, nchunk, S5_W), F32)],
        compiler_params=pltpu.CompilerParams(dimension_semantics=("parallel", "arbitrary"),
                                             vmem_limit_bytes=S5_VMEM_LIMIT),
        name="s5",
    )(su, ops["m"], ops["b"], ops["c"], ops["pw"][nchunk])


def _od_mix(c_ref, su_ref, ys_ref, d_ref, wglu_ref, wc_ref, ws_ref):
    y = d_ref[...] * su_ref[...] + ys_ref[...]
    z = jax.nn.gelu(y, approximate=True)
    sg = (z * jax.nn.sigmoid(_dot(z.astype(BF16), wglu_ref[...]))).astype(BF16)
    return _dot(c_ref[...], wc_ref[...]) + _dot(sg, ws_ref[...])


def _pad_heads(w, lo, hi, width):
    k = w.shape[0]
    w = w.reshape(k, MLA_HEADS, -1)[:, :, lo:hi]
    return jnp.pad(w, ((0, 0), (0, 0), (0, width - (hi - lo)))).reshape(k, MLA_HEADS * width)


def _rot_cols(w):
    half = w.shape[-1] // 2
    return jnp.concatenate([-w[..., half:], w[..., :half]], axis=-1)


def _rope_tables(seq_len):
    half = MLA_ROPE // 2
    inv = ROPE_THETA ** (-jnp.arange(half, dtype=F32) / half)
    ang = jnp.arange(seq_len, dtype=F32)[:, None] * inv[None, :]
    cos = jnp.concatenate([jnp.cos(ang)] * 2, axis=1)
    sin = jnp.concatenate([jnp.sin(ang)] * 2, axis=1)
    scale = (MLA_NOPE + MLA_ROPE) ** -0.5 * math.log2(math.e)
    z64 = jnp.zeros((seq_len, MLA_NOPE), F32)
    z32 = jnp.zeros((seq_len, HEAD_PAD - MLA_NOPE - MLA_ROPE), F32)
    cq = jnp.concatenate([z64 + scale, cos * scale, z32], axis=1)
    sq = jnp.concatenate([z64, sin * scale, z32], axis=1)
    ck = jnp.concatenate([z64, cos, z32], axis=1)
    sk = jnp.concatenate([z64, sin, z32], axis=1)
    return jnp.concatenate([ck, sk], axis=1), jnp.concatenate([cq, sq], axis=1).T


def _nat_bias(rpb, kh):
    c = np.arange(GRID_W)
    col_start = np.clip(c - NAT_KW // 2, 0, GRID_W - NAT_KW)
    col_ok = (c[None, :] >= col_start[:, None]) & (c[None, :] < col_start[:, None] + NAT_KW)
    col_off = np.clip(c[None, :] - c[:, None], -(NAT_KW - 1), NAT_KW - 1) + (NAT_KW - 1)
    delta = np.arange(kh)
    row_off = np.arange(kh)[None, :] - delta[:, None] + (NAT_KH_MAX - 1)
    row_sel = (row_off[:, :, None] == np.arange(2 * NAT_KH_MAX - 1)).astype(np.float32)
    col_sel = (col_off[:, :, None] == np.arange(2 * NAT_KW - 1)).astype(np.float32)
    bias = jnp.einsum("hrc,djr,qkc->dhqjk", rpb.astype(F32), row_sel, col_sel, precision=lax.Precision.HIGHEST)
    bias = jnp.where(col_ok[None, None, :, None, :], bias, NEG_INF)
    bias = bias.reshape(kh, NAT_HEADS // 2, 2 * GRID_W, kh * GRID_W)
    return bias.transpose(0, 1, 3, 2)


def _pair_diag(x, spec, rows, cols):
    x = x.reshape((S5_PAIRS, 2) + x.shape[1:])
    return jnp.einsum(spec, x, jnp.eye(2, dtype=x.dtype)).reshape(S5_PAIRS, rows, cols)


def _s5_operators(lam_re, lam_im, log_step, b_re, b_im, c_re, c_im, nchunks):
    t = S5_CHUNK
    w = S5_GROUP * t
    dt = jnp.exp(log_step)[:, :, None]
    ar, ai = lam_re * dt, lam_im * dt
    er = jnp.exp(ar)
    lbr, lbi = er * jnp.cos(ai), er * jnp.sin(ai)
    den = lam_re * lam_re + lam_im * lam_im
    fr = ((lbr - 1.0) * lam_re + lbi * lam_im) / den
    fi = (lbi * lam_re - (lbr - 1.0) * lam_im) / den
    bbr = fr[..., None] * b_re - fi[..., None] * b_im
    bbi = fr[..., None] * b_im + fi[..., None] * b_re

    def power(d):
        d = d.astype(F32)[None, None, :, None]
        mag = jnp.exp(ar[:, :, None, :] * d)
        return mag * jnp.cos(ai[:, :, None, :] * d), mag * jnp.sin(ai[:, :, None, :] * d)

    hi = lax.Precision.HIGHEST
    pr, pi = power(jnp.arange(t + 1))
    wr = c_re[:, :, None] * pr[:, :, :, None, :] - c_im[:, :, None] * pi[:, :, :, None, :]
    wi = c_re[:, :, None] * pi[:, :, :, None, :] + c_im[:, :, None] * pr[:, :, :, None, :]
    kmat = (jnp.einsum("xgdcp,xgpk->xgdck", wr[:, :, :t], bbr, precision=hi)
            - jnp.einsum("xgdcp,xgpk->xgdck", wi[:, :, :t], bbi, precision=hi))
    kf, kr = kmat[0], kmat[1]
    kcat = jnp.concatenate([kr[:, :0:-1], kf[:, :1] + kr[:, :1], kf[:, 1:]], axis=1)
    piece = 2 * S5_GROUP
    kc = _pair_diag(kcat, "padck,ab->pakdbc", piece, (2 * t - 1) * piece)
    mmat = jnp.concatenate([kc[:, :, (t - 1 - s) * piece:(2 * t - 1 - s) * piece] for s in range(t)], axis=1)

    def contrib(pr_, pi_, br_, bi_):
        brt, bit = br_.transpose(0, 2, 1)[:, None], bi_.transpose(0, 2, 1)[:, None]
        re = pr_[:, :, None, :] * brt - pi_[:, :, None, :] * bit
        im = pr_[:, :, None, :] * bit + pi_[:, :, None, :] * brt
        return [_pair_diag(x, "paxys,ab->pxaybs", 2 * w, S5_X) for x in (re, im)]

    bmat = jnp.concatenate(
        contrib(pr[0, :, t - 1::-1], pi[0, :, t - 1::-1], bbr[0], bbi[0])
        + contrib(pr[1, :, :t], pi[1, :, :t], bbr[1], bbi[1]), axis=2)

    def readout(w_):
        return _pair_diag(w_, "patcs,ab->pastbc", S5_X, 2 * w)

    cmat = jnp.concatenate([
        readout(wr[0, :, 1:t + 1]), readout(-wi[0, :, 1:t + 1]),
        readout(wr[1, :, t:0:-1]), readout(-wi[1, :, t:0:-1]),
    ], axis=1)

    pws = {}
    for nchunk in nchunks:
        nstep = max(nchunk.bit_length() - 1, 1)
        qr, qi = power(t * (2 ** jnp.arange(nstep)))

        def lanes(q):
            q = q.reshape(2, S5_PAIRS, 2, nstep, S5_STATE).transpose(1, 0, 3, 2, 4)
            return q.reshape(S5_PAIRS, 2, nstep, S5_X)

        pws[nchunk] = jnp.stack([lanes(qr), lanes(qi)], axis=3)
    return {"m": mmat.astype(BF16), "b": bmat.astype(BF16), "c": cmat.astype(BF16), "pw": pws}


def _even_params(ev_w_in, q_norm, kv_norm, w_uq, w_ukv, rpb, ev_w_out, seq_lens):
    c0 = MLA_Q_RANK
    c1 = c0 + MLA_KV_RANK
    c2 = c1 + MLA_ROPE
    w_kr = ev_w_in[:, c1:c2]
    pad_l = jnp.zeros((D_MODEL, MLA_NOPE), F32)
    pad_r = jnp.zeros((D_MODEL, HEAD_PAD - MLA_NOPE - MLA_ROPE), F32)
    w_in = jnp.concatenate([
        ev_w_in[:, :c1],
        pad_l, w_kr, pad_r,
        pad_l, _rot_cols(w_kr), pad_r,
        ev_w_in[:, c2:],
    ], axis=1).astype(BF16)
    uq = w_uq.reshape(MLA_Q_RANK, MLA_HEADS, MLA_NOPE + MLA_ROPE)
    uq_rot = jnp.concatenate([jnp.zeros_like(uq[..., :MLA_NOPE]), _rot_cols(uq[..., MLA_NOPE:])], axis=-1)
    qk_dim = MLA_NOPE + MLA_ROPE
    v_one = np.zeros((MLA_HEADS, V_ROWS), np.float32)
    v_one[:, MLA_V] = 1.0
    nat_bias = {kh: _nat_bias(rpb, kh) for kh in {min(NAT_KH_MAX, sl // GRID_W) for sl in seq_lens}}
    return {
        "w_in": w_in,
        "q_norm": q_norm[None], "kv_norm": kv_norm[None],
        "w_uq_t": _pad_heads(w_uq, 0, qk_dim, HEAD_PAD).T.astype(BF16),
        "w_uq_rot_t": _pad_heads(uq_rot.reshape(MLA_Q_RANK, -1), 0, qk_dim, HEAD_PAD).T.astype(BF16),
        "w_uk": _pad_heads(w_ukv, 0, MLA_NOPE, HEAD_PAD).astype(BF16),
        "w_uv_t": _pad_heads(w_ukv, MLA_NOPE, MLA_NOPE + MLA_V, V_ROWS).T.astype(BF16),
        "v_one": jnp.asarray(v_one.reshape(MLA_HEADS * V_ROWS, 1)),
        "rope_tab": {sl: _rope_tables(sl) for sl in seq_lens},
        "nat_bias": {sl: nat_bias[min(NAT_KH_MAX, sl // GRID_W)] for sl in seq_lens},
        "wa": ev_w_out[:MLA_HEADS * MLA_V].astype(BF16),
        "wb": ev_w_out[MLA_HEADS * MLA_V:].astype(BF16),
    }


def _even_mixer(h, g_pre, p, batch, seq_len):
    qt, k, vt, nq, nk, nv = _ev_in(h, g_pre, p, seq_len)
    a = _mla(qt, k, vt, batch, seq_len)
    b = _nat(nq, nk, nv, p["nat_bias"][seq_len], batch, seq_len)
    return _ev_mix, (a, b), (p["wa"], p["wb"])


def _odd_mixer(h, g_pre, p, batch, seq_len):
    u, su = _od_in(h, g_pre, p["w_in"])
    c = _conv(u, p["dw_w"], p["dw_b"], p["ln_g"], p["ln_b"], seq_len)
    ys = _s5(su, p["s5"], batch, seq_len)
    return _od_mix, (c, su, ys), (p["d"], p["w_glu"], p["wc"], p["ws"])


def kernel(x_prompt, x_sample, norm_g, ffn_w_gate, ffn_w_up, ffn_w_down, ev_w_in, mla_q_norm, mla_kv_norm, mla_w_uq, mla_w_ukv, nat_rpb, ev_w_out, od_w_in, conv_dw_w, conv_dw_b, conv_ln_g, conv_ln_b, s5_lambda_re, s5_lambda_im, s5_log_step, s5_b_re, s5_b_im, s5_c_re, s5_c_im, s5_d, s5_w_glu, od_w_out):
    depth = norm_g.shape[0]
    seq_lens = sorted({x_prompt.shape[1], x_sample.shape[1]})
    ffn_w = tuple(w.astype(BF16) for w in (ffn_w_gate, ffn_w_up, ffn_w_down))
    mixers = []
    for layer in range(depth):
        i = layer // 2
        if layer % 2 == 0:
            mixers.append(_even_params(ev_w_in[i], mla_q_norm[i], mla_kv_norm[i], mla_w_uq[i], mla_w_ukv[i],
                                       nat_rpb[i], ev_w_out[i], seq_lens))
        else:
            mixers.append({
                "w_in": od_w_in[i].astype(BF16),
                "dw_w": conv_dw_w[i], "dw_b": conv_dw_b[i][None],
                "ln_g": conv_ln_g[i][None], "ln_b": conv_ln_b[i][None],
                "s5": _s5_operators(s5_lambda_re[i], s5_lambda_im[i], s5_log_step[i], s5_b_re[i], s5_b_im[i],
                                    s5_c_re[i], s5_c_im[i], [sl // S5_CHUNK for sl in seq_lens]),
                "d": s5_d[i][None],
                "w_glu": s5_w_glu[i].astype(BF16),
                "wc": od_w_out[i][:CONV_CH].astype(BF16),
                "ws": od_w_out[i][CONV_CH:].astype(BF16),
            })

    gains = [[norm_g[layer, i][None] for i in range(norm_g.shape[1])] for layer in range(depth)]

    def trunk(x):
        batch, seq_len, _ = x.shape
        h = x.reshape(batch * seq_len, D_MODEL)
        for layer in range(depth):
            g = gains[layer]
            h = _ffn(h, g[0], *ffn_w, g[1], layer, 0)
            mixer = _even_mixer if layer % 2 == 0 else _odd_mixer
            mix = mixer(h, g[2], mixers[layer], batch, seq_len)
            h = _ffn(h, g[4], *ffn_w, g[5], layer, 1, mix + (g[3],))
        return h.reshape(batch, seq_len, D_MODEL)

    return (trunk(x_prompt), trunk(x_sample))
```

```python
import functools
import math

import jax
import jax.numpy as jnp
import numpy as np
from jax import lax
from jax.experimental import pallas as pl
from jax.experimental.pallas import tpu as pltpu

F32 = jnp.float32
BF16 = jnp.bfloat16

LANES = 128
SUBLANES = 8

D_MODEL = 1024
D_FF = 2816
GRID_W = 64

MLA_HEADS = 8
MLA_Q_RANK = 256
MLA_KV_RANK = 128
MLA_NOPE = 64
MLA_ROPE = 32
MLA_V = 64
ROPE_THETA = 10000.0
HEAD_PAD = 128
V_ROWS = 80
MLA_HPB = LANES // MLA_V

NAT_HEADS = 8
NAT_HEAD_DIM = 64
NAT_W = NAT_HEADS * NAT_HEAD_DIM
NAT_KH_MAX = 8
NAT_KW = 16

CONV_CH = 512
CONV_WIDTH = 31
CONV_HALO = 16

S5_CH = 512
S5_GROUP = 16
S5_GROUPS = S5_CH // S5_GROUP
S5_STATE = 64
S5_CHUNK = 16
S5_PAIRS = S5_GROUPS // 2

FFN_RES_SCALE = 0.5
NORM_EPS = 1e-6
NEG_INF = -1e30

VMEM_LIMIT = 48 * 1024 * 1024
FFN_VMEM_LIMIT = 56 * 1024 * 1024


def _cparams(*sem):
    return pltpu.CompilerParams(dimension_semantics=sem, vmem_limit_bytes=VMEM_LIMIT)


def _rms(x, g):
    return x * lax.rsqrt(jnp.mean(x * x, axis=-1, keepdims=True) + NORM_EPS) * g


def _dot(a, b):
    return jnp.dot(a, b, preferred_element_type=F32)


def _dot_nt(a, b):
    return lax.dot_general(a, b, (((1,), (1,)), ((), ())), preferred_element_type=F32)


def _full(shape):
    n = len(shape)
    return pl.BlockSpec(shape, lambda *_: (0,) * n)


def _ffn_kernel(*refs, mix_fn, n_tok, n_const):
    h_ref = refs[0]
    gpre_ref, wg_ref, wu_ref, wd_ref, gpost_ref, o_ref = refs[-6:]
    x = h_ref[...]
    if mix_fn is not None:
        mix = mix_fn(*refs[1:1 + n_tok + n_const])
        x = x + _rms(mix, refs[1 + n_tok + n_const][...])
    xn = _rms(x, gpre_ref[...]).astype(BF16)
    gate = _dot(xn, wg_ref[...])
    up = _dot(xn, wu_ref[...])
    hdn = (gate * jax.nn.sigmoid(gate) * up).astype(BF16)
    o_ref[...] = x + FFN_RES_SCALE * _rms(_dot(hdn, wd_ref[...]), gpost_ref[...])


def _ffn(h, gpre, wg, wu, wd, gpost, layer, j, mix=None, *, tm=512):
    m = h.shape[0]
    mix_fn, toks, consts, gmix = mix if mix is not None else (None, (), (), None)
    tok = lambda w: pl.BlockSpec((tm, w), lambda i: (i, 0))
    resident = lambda shape: pl.BlockSpec((None, None) + shape, lambda i: (layer, j, 0, 0),
                                          pipeline_mode=pl.Buffered(1))
    mix_args = [*toks, *consts] + ([gmix] if mix_fn is not None else [])
    mix_specs = [tok(t.shape[1]) for t in toks] + [_full(c.shape) for c in consts] + [
        _full((1, D_MODEL))] * (mix_fn is not None)
    return pl.pallas_call(
        functools.partial(_ffn_kernel, mix_fn=mix_fn, n_tok=len(toks), n_const=len(consts)),
        grid=(m // tm,),
        in_specs=[tok(D_MODEL), *mix_specs, _full((1, D_MODEL)),
                  resident((D_MODEL, D_FF)), resident((D_MODEL, D_FF)), resident((D_FF, D_MODEL)),
                  _full((1, D_MODEL))],
        out_specs=tok(D_MODEL),
        out_shape=jax.ShapeDtypeStruct((m, D_MODEL), F32),
        compiler_params=pltpu.CompilerParams(dimension_semantics=("parallel",), vmem_limit_bytes=FFN_VMEM_LIMIT),
        name="ffn",
    )(h, *mix_args, gpre, wg, wu, wd, gpost)


EV_Z = MLA_Q_RANK + MLA_KV_RANK + 2 * HEAD_PAD + 3 * NAT_W
HP_ALL = MLA_HEADS * HEAD_PAD


def _ev_in_kernel(h_ref, g_ref, win_ref, qn_ref, kvn_ref, wuq_ref, wuqr_ref, wuk_ref, wuv_ref,
                  tab_ref, tabt_ref, vone_ref, qt_ref, k_ref, vt_ref, nq_ref, nk_ref, nv_ref):
    m = _rms(h_ref[...], g_ref[...]).astype(BF16)
    z = _dot(m, win_ref[...])
    c0 = MLA_Q_RANK
    c1 = c0 + MLA_KV_RANK
    c2 = c1 + HEAD_PAD
    c3 = c2 + HEAD_PAD
    q_lat = z[:, :c0]
    kv_lat = z[:, c0:c1]
    kr = z[:, c1:c2]
    kr_rot = z[:, c2:c3]
    nq_ref[...] = (z[:, c3:c3 + NAT_W] * (NAT_HEAD_DIM ** -0.5)).astype(BF16)
    nk_ref[...] = z[:, c3 + NAT_W:c3 + 2 * NAT_W].astype(BF16)
    nv_ref[...] = z[:, c3 + 2 * NAT_W:c3 + 3 * NAT_W].astype(BF16)

    qn = _rms(q_lat, qn_ref[...]).astype(BF16)
    kvn = _rms(kv_lat, kvn_ref[...]).astype(BF16)
    q_raw_t = _dot_nt(wuq_ref[...], qn)
    q_rot_t = _dot_nt(wuqr_ref[...], qn)
    v_t = _dot_nt(wuv_ref[...], kvn) + vone_ref[...]
    k_nope = _dot(kvn, wuk_ref[...])

    tab = tab_ref[...]
    k_rope = kr * tab[:, :HEAD_PAD] + kr_rot * tab[:, HEAD_PAD:]
    cq_t = tabt_ref[:HEAD_PAD, :]
    sq_t = tabt_ref[HEAD_PAD:, :]
    for hd in range(MLA_HEADS):
        sl = slice(hd * HEAD_PAD, (hd + 1) * HEAD_PAD)
        qt_ref[hd] = (q_raw_t[sl] * cq_t + q_rot_t[sl] * sq_t).astype(BF16)
        k_ref[:, sl] = (k_nope[:, sl] + k_rope).astype(BF16)
        vt_ref[hd] = v_t[hd * V_ROWS:(hd + 1) * V_ROWS].astype(BF16)


def _ev_in(h, g, p, seq_len, *, tm=512):
    m = h.shape[0]
    nblk = seq_len // tm
    tok = lambda w: pl.BlockSpec((tm, w), lambda i: (i, 0))
    feat = lambda r: pl.BlockSpec((MLA_HEADS, r, tm), lambda i: (0, 0, i))
    outs = [jax.ShapeDtypeStruct((MLA_HEADS, HEAD_PAD, m), BF16), jax.ShapeDtypeStruct((m, HP_ALL), BF16),
            jax.ShapeDtypeStruct((MLA_HEADS, V_ROWS, m), BF16)] + [jax.ShapeDtypeStruct((m, NAT_W), BF16)] * 3
    tab, tab_t = p["rope_tab"][seq_len]
    return pl.pallas_call(
        _ev_in_kernel,
        grid=(m // tm,),
        in_specs=[
            tok(D_MODEL), _full((1, D_MODEL)), _full((D_MODEL, EV_Z)),
            _full((1, MLA_Q_RANK)), _full((1, MLA_KV_RANK)),
            _full((HP_ALL, MLA_Q_RANK)), _full((HP_ALL, MLA_Q_RANK)),
            _full((MLA_KV_RANK, HP_ALL)), _full((MLA_HEADS * V_ROWS, MLA_KV_RANK)),
            pl.BlockSpec((tm, 2 * HEAD_PAD), lambda i: (i % nblk, 0)),
            pl.BlockSpec((2 * HEAD_PAD, tm), lambda i: (0, i % nblk)),
            _full((MLA_HEADS * V_ROWS, 1)),
        ],
        out_specs=[feat(HEAD_PAD), tok(HP_ALL), feat(V_ROWS)] + [tok(NAT_W)] * 3,
        out_shape=outs,
        compiler_params=_cparams("parallel"),
        name="ev_in",
    )(h, g, p["w_in"], p["q_norm"], p["kv_norm"], p["w_uq_t"], p["w_uq_rot_t"], p["w_uk"], p["w_uv_t"],
      tab, tab_t, p["v_one"])


def _mla_kernel(qt_ref, k_ref, vt_ref, o_ref, s_scr, *, tk, nk, unroll):
    tq = qt_ref.shape[2]
    outs = []
    for hh in range(MLA_HPB):
        qt = qt_ref[hh]
        head = slice(hh * HEAD_PAD, (hh + 1) * HEAD_PAD)

        def scores(j):
            off = pl.multiple_of(j * tk, tk)
            return _dot(k_ref[pl.ds(off, tk), head], qt)

        s_scr[0] = scores(0)

        def body(jj, carry):
            m_prev, acc = carry
            for i in range(unroll):
                j = jj * unroll + i
                s_scr[(i + 1) % 2] = scores(jnp.minimum(j + 1, nk - 1))
                st = s_scr[i % 2]
                m_new = jnp.maximum(m_prev, jnp.max(st, axis=0, keepdims=True))
                alpha = jnp.exp2(m_prev - m_new)
                pt = jnp.exp2(st - m_new).astype(BF16)
                off = pl.multiple_of(j * tk, tk)
                acc = alpha * acc + _dot(vt_ref[hh, :, pl.ds(off, tk)], pt)
                m_prev = m_new
            return m_prev, acc

        m_init = jnp.full((1, tq), jnp.finfo(F32).min, F32)
        acc = jnp.zeros((V_ROWS, tq), F32)
        _, acc = lax.fori_loop(0, nk // unroll, body, (m_init, acc))
        outs.append(acc[:MLA_V] / acc[MLA_V:MLA_V + 1])
    o_ref[...] = jnp.concatenate(outs, axis=0).T.astype(BF16)


def _mla(qt, k, vt, batch, seq_len, *, tq=512, tk=512, unroll=16):
    m = k.shape[0]
    tk = min(tk, seq_len // unroll)
    nq = seq_len // tq
    nk = seq_len // tk
    assert unroll % 2 == 0 and nk % unroll == 0
    return pl.pallas_call(
        functools.partial(_mla_kernel, tk=tk, nk=nk, unroll=unroll),
        grid=(batch, MLA_HEADS // MLA_HPB, nq),
        in_specs=[
            pl.BlockSpec((MLA_HPB, HEAD_PAD, tq), lambda b, h, i: (h, 0, b * nq + i)),
            pl.BlockSpec((seq_len, MLA_HPB * HEAD_PAD), lambda b, h, i: (b, h)),
            pl.BlockSpec((MLA_HPB, V_ROWS, seq_len), lambda b, h, i: (h, 0, b)),
        ],
        out_specs=pl.BlockSpec((tq, MLA_HPB * MLA_V), lambda b, h, i: (b * nq + i, h)),
        out_shape=jax.ShapeDtypeStruct((m, MLA_HEADS * MLA_V), BF16),
        scratch_shapes=[pltpu.VMEM((2, tk, tq), F32)],
        compiler_params=_cparams("parallel", "parallel", "arbitrary"),
        name="mla",
    )(qt, k, vt)


NAT_LANES = 2 * NAT_HEAD_DIM


def _nat_kernel(q_ref, k_ref, v_ref, bias_ref, o_ref, *, rows, kh, rblk):
    i = pl.program_id(2)
    lane = lax.broadcasted_iota(jnp.int32, (GRID_W, NAT_LANES), 1)
    head0 = lane < NAT_HEAD_DIM

    def row_body(rr, carry):
        r = i * rblk + rr
        start = jnp.clip(r - kh // 2, 0, rows - kh)
        delta = r - start
        koff = pl.multiple_of(start * GRID_W, GRID_W)
        qoff = pl.multiple_of(rr * GRID_W, GRID_W)
        qrow = q_ref[pl.ds(qoff, GRID_W), :]
        kwin = k_ref[pl.ds(koff, kh * GRID_W), :]
        vwin = v_ref[pl.ds(koff, kh * GRID_W), :]
        zero = jnp.zeros_like(qrow)
        q2 = jnp.concatenate([jnp.where(head0, qrow, zero), jnp.where(head0, zero, qrow)], axis=0)
        st = _dot_nt(kwin, q2) + bias_ref[delta, 0]
        e = jnp.exp(st - jnp.max(st, axis=0, keepdims=True))
        pt = (e / jnp.sum(e, axis=0, keepdims=True)).astype(BF16)
        o2 = lax.dot_general(pt, vwin, (((0,), (0,)), ((), ())), preferred_element_type=F32)
        o_ref[pl.ds(qoff, GRID_W), :] = jnp.where(head0, o2[:GRID_W], o2[GRID_W:]).astype(BF16)
        return carry

    lax.fori_loop(0, rblk, row_body, 0, unroll=True)


def _nat(nq, nk, nv, bias, batch, seq_len, *, rblk=32):
    m = nq.shape[0]
    rows = seq_len // GRID_W
    kh = min(NAT_KH_MAX, rows)
    rblk = min(rblk, rows)
    nblk = rows // rblk
    tq = rblk * GRID_W
    return pl.pallas_call(
        functools.partial(_nat_kernel, rows=rows, kh=kh, rblk=rblk),
        grid=(batch, NAT_HEADS // 2, nblk),
        in_specs=[
            pl.BlockSpec((tq, NAT_LANES), lambda b, hp, i: (b * nblk + i, hp)),
            pl.BlockSpec((seq_len, NAT_LANES), lambda b, hp, i: (b, hp)),
            pl.BlockSpec((seq_len, NAT_LANES), lambda b, hp, i: (b, hp)),
            pl.BlockSpec((kh, 1, kh * GRID_W, 2 * GRID_W), lambda b, hp, i: (0, hp, 0, 0)),
        ],
        out_specs=pl.BlockSpec((tq, NAT_LANES), lambda b, hp, i: (b * nblk + i, hp)),
        out_shape=jax.ShapeDtypeStruct((m, NAT_W), BF16),
        compiler_params=_cparams("parallel", "parallel", "arbitrary"),
        name="nat",
    )(nq, nk, nv, bias)


def _ev_mix(a_ref, b_ref, wa_ref, wb_ref):
    return _dot(a_ref[...], wa_ref[...]) + _dot(b_ref[...], wb_ref[...])


def _od_in_kernel(h_ref, g_ref, win_ref, u_ref, su_ref):
    m = _rms(h_ref[...], g_ref[...]).astype(BF16)
    z = _dot(m, win_ref[...])
    ca = z[:, :CONV_CH]
    cg = z[:, CONV_CH:2 * CONV_CH]
    u_ref[...] = ca * jax.nn.sigmoid(cg)
    su_ref[...] = z[:, 2 * CONV_CH:]


def _od_in(h, g, w_in, *, tm=512):
    m = h.shape[0]
    tok = lambda w: pl.BlockSpec((tm, w), lambda i: (i, 0))
    return pl.pallas_call(
        _od_in_kernel,
        grid=(m // tm,),
        in_specs=[tok(D_MODEL), _full((1, D_MODEL)), _full(w_in.shape)],
        out_specs=[tok(CONV_CH), tok(S5_CH)],
        out_shape=[jax.ShapeDtypeStruct((m, CONV_CH), F32), jax.ShapeDtypeStruct((m, S5_CH), F32)],
        compiler_params=_cparams("parallel"),
        name="od_in",
    )(h, g, w_in)


CONV_SUB = 64


def _conv_kernel(prev_ref, cur_ref, next_ref, w_ref, b_ref, lg_ref, lb_ref, o_ref, scr, *, tm, nblk):
    i = pl.program_id(0)
    first = (i % nblk) == 0
    last = (i % nblk) == nblk - 1
    scr[0:CONV_HALO, :] = jnp.where(first, 0.0, prev_ref[...])
    scr[CONV_HALO:CONV_HALO + tm, :] = cur_ref[...]
    scr[CONV_HALO + tm:, :] = jnp.where(last, 0.0, next_ref[...])
    w = w_ref[...]
    shift = CONV_HALO - CONV_WIDTH // 2

    def sub(c, carry):
        base = pl.multiple_of(c * CONV_SUB, CONV_SUB)
        cols = []
        for lb in range(CONV_CH // LANES):
            ls = slice(lb * LANES, (lb + 1) * LANES)
            win = scr[pl.ds(base, CONV_SUB + 2 * CONV_HALO), ls]
            acc = jnp.zeros((CONV_SUB, LANES), F32)
            nwin = CONV_SUB + 2 * CONV_HALO
            for b in range(SUBLANES):
                wb = pltpu.roll(win, nwin - b, 0) if b else win
                for a in range(2 * CONV_HALO // SUBLANES):
                    kk = SUBLANES * a + b - shift
                    if 0 <= kk < CONV_WIDTH:
                        acc = acc + wb[SUBLANES * a:SUBLANES * a + CONV_SUB, :] * w[kk:kk + 1, ls]
            cols.append(acc)
        y = jnp.concatenate(cols, axis=1) + b_ref[...]
        mu = jnp.mean(y, axis=-1, keepdims=True)
        yc = y - mu
        yn = yc * lax.rsqrt(jnp.mean(yc * yc, axis=-1, keepdims=True) + NORM_EPS)
        yn = yn * lg_ref[...] + lb_ref[...]
        o_ref[pl.ds(base, CONV_SUB), :] = (yn * jax.nn.sigmoid(yn)).astype(BF16)
        return carry

    lax.fori_loop(0, tm // CONV_SUB, sub, 0)


def _conv(u, w, b, lg, lb, seq_len, *, tm=512):
    m = u.shape[0]
    nblk = seq_len // tm
    hb = tm // CONV_HALO
    nh = m // CONV_HALO
    return pl.pallas_call(
        functools.partial(_conv_kernel, tm=tm, nblk=nblk),
        grid=(m // tm,),
        in_specs=[
            pl.BlockSpec((CONV_HALO, CONV_CH), lambda i: (jnp.maximum(i * hb - 1, 0), 0)),
            pl.BlockSpec((tm, CONV_CH), lambda i: (i, 0)),
            pl.BlockSpec((CONV_HALO, CONV_CH), lambda i: (jnp.minimum((i + 1) * hb, nh - 1), 0)),
            _full((CONV_WIDTH, CONV_CH)), _full((1, CONV_CH)), _full((1, CONV_CH)), _full((1, CONV_CH)),
        ],
        out_specs=pl.BlockSpec((tm, CONV_CH), lambda i: (i, 0)),
        out_shape=jax.ShapeDtypeStruct((m, CONV_CH), BF16),
        scratch_shapes=[pltpu.VMEM((tm + 2 * CONV_HALO, CONV_CH), F32)],
        compiler_params=_cparams("parallel"),
        name="conv",
    )(u, u, u, w, b, lg, lb)


S5_W = 2 * S5_GROUP * S5_CHUNK
S5_X = 2 * S5_STATE
S5_PIECE = 2 * S5_GROUP
S5_NPP = LANES // S5_PIECE
S5_VMEM_LIMIT = 56 * 1024 * 1024


def _s5_kernel(su_ref, m_ref, b_ref, c_ref, pw_ref, y_ref, u_scr, y_scr, *, nchunk):
    nstep = nchunk.bit_length() - 1
    slot = lax.broadcasted_iota(jnp.int32, (nchunk, LANES), 1) // S5_PIECE
    row = lax.broadcasted_iota(jnp.int32, (nchunk, S5_X), 0)

    def token_rows(ref, sigma):
        return ref.at[pl.ds(sigma, nchunk, stride=S5_CHUNK), :]

    def move(x, src, dst):
        return x if src == dst else pltpu.roll(x, ((dst - src) % S5_NPP) * S5_PIECE, 1)

    for col in range(S5_W // LANES):
        acc = [None] * S5_NPP
        for s in range(S5_NPP):
            x = token_rows(su_ref, S5_NPP * col + s)[...]
            for pp in range(S5_NPP):
                r = move(x, pp, s)
                acc[pp] = r if s == 0 else jnp.where(slot == s, r, acc[pp])
        for pp in range(S5_NPP):
            u_scr[pp, :, col * LANES:(col + 1) * LANES] = acc[pp].astype(BF16)

    for pp in range(S5_NPP):
        u = u_scr[pp]
        contrib = _dot(u, b_ref[pp])
        states = []
        for d in range(2):
            xr = contrib[:, (2 * d) * S5_X:(2 * d + 1) * S5_X]
            xi = contrib[:, (2 * d + 1) * S5_X:(2 * d + 2) * S5_X]

            def shifted(x, sh):
                if d == 0:
                    return jnp.where(row >= sh, pltpu.roll(x, sh, 0), 0.0)
                return jnp.where(row < nchunk - sh, pltpu.roll(x, nchunk - sh, 0), 0.0)

            for k in range(nstep):
                ar = pw_ref[pp, d, k, 0:1, :]
                ai = pw_ref[pp, d, k, 1:2, :]
                sr = shifted(xr, 1 << k)
                si = shifted(xi, 1 << k)
                xr, xi = xr + ar * sr - ai * si, xi + ar * si + ai * sr
            states += [shifted(xr, 1), shifted(xi, 1)]
        x = jnp.concatenate(states, axis=1).astype(BF16)
        y_scr[pp] = _dot(u, m_ref[pp]) + _dot(x, c_ref[pp])

    for col in range(S5_W // LANES):
        ys = [y_scr[pp, :, col * LANES:(col + 1) * LANES] for pp in range(S5_NPP)]
        for s in range(S5_NPP):
            out = move(ys[0], s, 0)
            for pp in range(1, S5_NPP):
                out = jnp.where(slot == pp, move(ys[pp], s, pp), out)
            token_rows(y_ref, S5_NPP * col + s)[...] = out


def _s5(su, ops, batch, seq_len):
    m = su.shape[0]
    nchunk = seq_len // S5_CHUNK
    assert nchunk & (nchunk - 1) == 0
    nstep = max(nchunk.bit_length() - 1, 1)
    seq_blk = pl.BlockSpec((seq_len, LANES), lambda q, b: (b, q), pipeline_mode=pl.Buffered(1))
    per_q = lambda *shape: pl.BlockSpec((S5_NPP,) + shape, lambda q, b: (q,) + (0,) * len(shape))
    return pl.pallas_call(
        functools.partial(_s5_kernel, nchunk=nchunk),
        grid=(S5_CH // LANES, batch),
        in_specs=[seq_blk, per_q(S5_W, S5_W), per_q(S5_W, 4 * S5_X), per_q(4 * S5_X, S5_W),
                  per_q(2, nstep, 2, S5_X)],
        out_specs=seq_blk,
        out_shape=jax.ShapeDtypeStruct((m, S5_CH), F32),
        scratch_shapes=[pltpu.VMEM((S5_NPP, nchunk, S5_W), BF16), pltpu.VMEM((S5_NPP, nchunk, S5_W), F32)],
        compiler_params=pltpu.CompilerParams(dimension_semantics=("parallel", "arbitrary"),
                                             vmem_limit_bytes=S5_VMEM_LIMIT),
        name="s5",
    )(su, ops["m"], ops["b"], ops["c"], ops["pw"][nchunk])


def _od_mix(c_ref, su_ref, ys_ref, d_ref, wglu_ref, wc_ref, ws_ref):
    y = d_ref[...] * su_ref[...] + ys_ref[...]
    z = jax.nn.gelu(y, approximate=True)
    sg = (z * jax.nn.sigmoid(_dot(z.astype(BF16), wglu_ref[...]))).astype(BF16)
    return _dot(c_ref[...], wc_ref[...]) + _dot(sg, ws_ref[...])


def _pad_heads(w, lo, hi, width):
    k = w.shape[0]
    w = w.reshape(k, MLA_HEADS, -1)[:, :, lo:hi]
    return jnp.pad(w, ((0, 0), (0, 0), (0, width - (hi - lo)))).reshape(k, MLA_HEADS * width)


def _rot_cols(w):
    half = w.shape[-1] // 2
    return jnp.concatenate([-w[..., half:], w[..., :half]], axis=-1)


def _rope_tables(seq_len):
    half = MLA_ROPE // 2
    inv = ROPE_THETA ** (-jnp.arange(half, dtype=F32) / half)
    ang = jnp.arange(seq_len, dtype=F32)[:, None] * inv[None, :]
    cos = jnp.concatenate([jnp.cos(ang)] * 2, axis=1)
    sin = jnp.concatenate([jnp.sin(ang)] * 2, axis=1)
    scale = (MLA_NOPE + MLA_ROPE) ** -0.5 * math.log2(math.e)
    z64 = jnp.zeros((seq_len, MLA_NOPE), F32)
    z32 = jnp.zeros((seq_len, HEAD_PAD - MLA_NOPE - MLA_ROPE), F32)
    cq = jnp.concatenate([z64 + scale, cos * scale, z32], axis=1)
    sq = jnp.concatenate([z64, sin * scale, z32], axis=1)
    ck = jnp.concatenate([z64, cos, z32], axis=1)
    sk = jnp.concatenate([z64, sin, z32], axis=1)
    return jnp.concatenate([ck, sk], axis=1), jnp.concatenate([cq, sq], axis=1).T


def _nat_bias(rpb, kh):
    c = np.arange(GRID_W)
    col_start = np.clip(c - NAT_KW // 2, 0, GRID_W - NAT_KW)
    col_ok = (c[None, :] >= col_start[:, None]) & (c[None, :] < col_start[:, None] + NAT_KW)
    col_off = np.clip(c[None, :] - c[:, None], -(NAT_KW - 1), NAT_KW - 1) + (NAT_KW - 1)
    delta = np.arange(kh)
    row_off = np.arange(kh)[None, :] - delta[:, None] + (NAT_KH_MAX - 1)
    row_sel = (row_off[:, :, None] == np.arange(2 * NAT_KH_MAX - 1)).astype(np.float32)
    col_sel = (col_off[:, :, None] == np.arange(2 * NAT_KW - 1)).astype(np.float32)
    bias = jnp.einsum("hrc,djr,qkc->dhqjk", rpb.astype(F32), row_sel, col_sel, precision=lax.Precision.HIGHEST)
    bias = jnp.where(col_ok[None, None, :, None, :], bias, NEG_INF)
    bias = bias.reshape(kh, NAT_HEADS // 2, 2 * GRID_W, kh * GRID_W)
    return bias.transpose(0, 1, 3, 2)


def _pair_diag(x, spec, rows, cols):
    x = x.reshape((S5_PAIRS, 2) + x.shape[1:])
    return jnp.einsum(spec, x, jnp.eye(2, dtype=x.dtype)).reshape(S5_PAIRS, rows, cols)


def _s5_operators(lam_re, lam_im, log_step, b_re, b_im, c_re, c_im, nchunks):
    t = S5_CHUNK
    w = S5_GROUP * t
    dt = jnp.exp(log_step)[:, :, None]
    ar, ai = lam_re * dt, lam_im * dt
    er = jnp.exp(ar)
    lbr, lbi = er * jnp.cos(ai), er * jnp.sin(ai)
    den = lam_re * lam_re + lam_im * lam_im
    fr = ((lbr - 1.0) * lam_re + lbi * lam_im) / den
    fi = (lbi * lam_re - (lbr - 1.0) * lam_im) / den
    bbr = fr[..., None] * b_re - fi[..., None] * b_im
    bbi = fr[..., None] * b_im + fi[..., None] * b_re

    def power(d):
        d = d.astype(F32)[None, None, :, None]
        mag = jnp.exp(ar[:, :, None, :] * d)
        return mag * jnp.cos(ai[:, :, None, :] * d), mag * jnp.sin(ai[:, :, None, :] * d)

    hi = lax.Precision.HIGHEST
    pr, pi = power(jnp.arange(t + 1))
    wr = c_re[:, :, None] * pr[:, :, :, None, :] - c_im[:, :, None] * pi[:, :, :, None, :]
    wi = c_re[:, :, None] * pi[:, :, :, None, :] + c_im[:, :, None] * pr[:, :, :, None, :]
    kmat = (jnp.einsum("xgdcp,xgpk->xgdck", wr[:, :, :t], bbr, precision=hi)
            - jnp.einsum("xgdcp,xgpk->xgdck", wi[:, :, :t], bbi, precision=hi))
    kf, kr = kmat[0], kmat[1]
    kcat = jnp.concatenate([kr[:, :0:-1], kf[:, :1] + kr[:, :1], kf[:, 1:]], axis=1)
    piece = 2 * S5_GROUP
    kc = _pair_diag(kcat, "padck,ab->pakdbc", piece, (2 * t - 1) * piece)
    mmat = jnp.concatenate([kc[:, :, (t - 1 - s) * piece:(2 * t - 1 - s) * piece] for s in range(t)], axis=1)

    def contrib(pr_, pi_, br_, bi_):
        brt, bit = br_.transpose(0, 2, 1)[:, None], bi_.transpose(0, 2, 1)[:, None]
        re = pr_[:, :, None, :] * brt - pi_[:, :, None, :] * bit
        im = pr_[:, :, None, :] * bit + pi_[:, :, None, :] * brt
        return [_pair_diag(x, "paxys,ab->pxaybs", 2 * w, S5_X) for x in (re, im)]

    bmat = jnp.concatenate(
        contrib(pr[0, :, t - 1::-1], pi[0, :, t - 1::-1], bbr[0], bbi[0])
        + contrib(pr[1, :, :t], pi[1, :, :t], bbr[1], bbi[1]), axis=2)

    def readout(w_):
        return _pair_diag(w_, "patcs,ab->pastbc", S5_X, 2 * w)

    cmat = jnp.concatenate([
        readout(wr[0, :, 1:t + 1]), readout(-wi[0, :, 1:t + 1]),
        readout(wr[1, :, t:0:-1]), readout(-wi[1, :, t:0:-1]),
    ], axis=1)

    pws = {}
    for nchunk in nchunks:
        nstep = max(nchunk.bit_length() - 1, 1)
        qr, qi = power(t * (2 ** jnp.arange(nstep)))

        def lanes(q):
            q = q.reshape(2, S5_PAIRS, 2, nstep, S5_STATE).transpose(1, 0, 3, 2, 4)
            return q.reshape(S5_PAIRS, 2, nstep, S5_X)

        pws[nchunk] = jnp.stack([lanes(qr), lanes(qi)], axis=3)
    return {"m": mmat.astype(BF16), "b": bmat.astype(BF16), "c": cmat.astype(BF16), "pw": pws}


def _even_params(ev_w_in, q_norm, kv_norm, w_uq, w_ukv, rpb, ev_w_out, seq_lens):
    c0 = MLA_Q_RANK
    c1 = c0 + MLA_KV_RANK
    c2 = c1 + MLA_ROPE
    w_kr = ev_w_in[:, c1:c2]
    pad_l = jnp.zeros((D_MODEL, MLA_NOPE), F32)
    pad_r = jnp.zeros((D_MODEL, HEAD_PAD - MLA_NOPE - MLA_ROPE), F32)
    w_in = jnp.concatenate([
        ev_w_in[:, :c1],
        pad_l, w_kr, pad_r,
        pad_l, _rot_cols(w_kr), pad_r,
        ev_w_in[:, c2:],
    ], axis=1).astype(BF16)
    uq = w_uq.reshape(MLA_Q_RANK, MLA_HEADS, MLA_NOPE + MLA_ROPE)
    uq_rot = jnp.concatenate([jnp.zeros_like(uq[..., :MLA_NOPE]), _rot_cols(uq[..., MLA_NOPE:])], axis=-1)
    qk_dim = MLA_NOPE + MLA_ROPE
    v_one = np.zeros((MLA_HEADS, V_ROWS), np.float32)
    v_one[:, MLA_V] = 1.0
    nat_bias = {kh: _nat_bias(rpb, kh) for kh in {min(NAT_KH_MAX, sl // GRID_W) for sl in seq_lens}}
    return {
        "w_in": w_in,
        "q_norm": q_norm[None], "kv_norm": kv_norm[None],
        "w_uq_t": _pad_heads(w_uq, 0, qk_dim, HEAD_PAD).T.astype(BF16),
        "w_uq_rot_t": _pad_heads(uq_rot.reshape(MLA_Q_RANK, -1), 0, qk_dim, HEAD_PAD).T.astype(BF16),
        "w_uk": _pad_heads(w_ukv, 0, MLA_NOPE, HEAD_PAD).astype(BF16),
        "w_uv_t": _pad_heads(w_ukv, MLA_NOPE, MLA_NOPE + MLA_V, V_ROWS).T.astype(BF16),
        "v_one": jnp.asarray(v_one.reshape(MLA_HEADS * V_ROWS, 1)),
        "rope_tab": {sl: _rope_tables(sl) for sl in seq_lens},
        "nat_bias": {sl: nat_bias[min(NAT_KH_MAX, sl // GRID_W)] for sl in seq_lens},
        "wa": ev_w_out[:MLA_HEADS * MLA_V].astype(BF16),
        "wb": ev_w_out[MLA_HEADS * MLA_V:].astype(BF16),
    }


def _even_mixer(h, g_pre, p, batch, seq_len):
    qt, k, vt, nq, nk, nv = _ev_in(h, g_pre, p, seq_len)
    a = _mla(qt, k, vt, batch, seq_len)
    b = _nat(nq, nk, nv, p["nat_bias"][seq_len], batch, seq_len)
    return _ev_mix, (a, b), (p["wa"], p["wb"])


def _odd_mixer(h, g_pre, p, batch, seq_len):
    u, su = _od_in(h, g_pre, p["w_in"])
    c = _conv(u, p["dw_w"], p["dw_b"], p["ln_g"], p["ln_b"], seq_len)
    ys = _s5(su, p["s5"], batch, seq_len)
    return _od_mix, (c, su, ys), (p["d"], p["w_glu"], p["wc"], p["ws"])


def kernel(x_prompt, x_sample, norm_g, ffn_w_gate, ffn_w_up, ffn_w_down, ev_w_in, mla_q_norm, mla_kv_norm, mla_w_uq, mla_w_ukv, nat_rpb, ev_w_out, od_w_in, conv_dw_w, conv_dw_b, conv_ln_g, conv_ln_b, s5_lambda_re, s5_lambda_im, s5_log_step, s5_b_re, s5_b_im, s5_c_re, s5_c_im, s5_d, s5_w_glu, od_w_out):
    depth = norm_g.shape[0]
    seq_lens = sorted({x_prompt.shape[1], x_sample.shape[1]})
    ffn_w = tuple(w.astype(BF16) for w in (ffn_w_gate, ffn_w_up, ffn_w_down))
    mixers = []
    for layer in range(depth):
        i = layer // 2
        if layer % 2 == 0:
            mixers.append(_even_params(ev_w_in[i], mla_q_norm[i], mla_kv_norm[i], mla_w_uq[i], mla_w_ukv[i],
                                       nat_rpb[i], ev_w_out[i], seq_lens))
        else:
            mixers.append({
                "w_in": od_w_in[i].astype(BF16),
                "dw_w": conv_dw_w[i], "dw_b": conv_dw_b[i][None],
                "ln_g": conv_ln_g[i][None], "ln_b": conv_ln_b[i][None],
                "s5": _s5_operators(s5_lambda_re[i], s5_lambda_im[i], s5_log_step[i], s5_b_re[i], s5_b_im[i],
                                    s5_c_re[i], s5_c_im[i], [sl // S5_CHUNK for sl in seq_lens]),
                "d": s5_d[i][None],
                "w_glu": s5_w_glu[i].astype(BF16),
                "wc": od_w_out[i][:CONV_CH].astype(BF16),
                "ws": od_w_out[i][CONV_CH:].astype(BF16),
            })

    gains = [[norm_g[layer, i][None] for i in range(norm_g.shape[1])] for layer in range(depth)]

    def trunk(x):
        batch, seq_len, _ = x.shape
        h = x.reshape(batch * seq_len, D_MODEL)
        for layer in range(depth):
            g = gains[layer]
            h = _ffn(h, g[0], *ffn_w, g[1], layer, 0)
            mixer = _even_mixer if layer % 2 == 0 else _odd_mixer
            mix = mixer(h, g[2], mixers[layer], batch, seq_len)
            h = _ffn(h, g[4], *ffn_w, g[5], layer, 1, mix + (g[3],))
        return h.reshape(batch, seq_len, D_MODEL)

    return (trunk(x_prompt), trunk(x_sample))
```

```python
import functools
import math

import jax
import jax.numpy as jnp
import numpy as np
from jax import lax
from jax.experimental import pallas as pl
from jax.experimental.pallas import tpu as pltpu

F32 = jnp.float32
BF16 = jnp.bfloat16

LANES = 128
SUBLANES = 8

D_MODEL = 1024
D_FF = 2816
GRID_W = 64

MLA_HEADS = 8
MLA_Q_RANK = 256
MLA_KV_RANK = 128
MLA_NOPE = 64
MLA_ROPE = 32
MLA_V = 64
ROPE_THETA = 10000.0
HEAD_PAD = 128
V_ROWS = 80
MLA_HPB = LANES // MLA_V

NAT_HEADS = 8
NAT_HEAD_DIM = 64
NAT_W = NAT_HEADS * NAT_HEAD_DIM
NAT_KH_MAX = 8
NAT_KW = 16

CONV_CH = 512
CONV_WIDTH = 31
CONV_HALO = 16

S5_CH = 512
S5_GROUP = 16
S5_GROUPS = S5_CH // S5_GROUP
S5_STATE = 64
S5_CHUNK = 16
S5_PAIRS = S5_GROUPS // 2

FFN_RES_SCALE = 0.5
NORM_EPS = 1e-6
NEG_INF = -1e30

VMEM_LIMIT = 48 * 1024 * 1024
FFN_VMEM_LIMIT = 56 * 1024 * 1024


def _cparams(*sem):
    return pltpu.CompilerParams(dimension_semantics=sem, vmem_limit_bytes=VMEM_LIMIT)


def _rms(x, g):
    return x * lax.rsqrt(jnp.mean(x * x, axis=-1, keepdims=True) + NORM_EPS) * g


def _dot(a, b):
    return jnp.dot(a, b, preferred_element_type=F32)


def _dot_nt(a, b):
    return lax.dot_general(a, b, (((1,), (1,)), ((), ())), preferred_element_type=F32)


def _full(shape):
    n = len(shape)
    return pl.BlockSpec(shape, lambda *_: (0,) * n)


def _ffn_kernel(*refs, mix_fn, n_tok, n_const):
    h_ref = refs[0]
    gpre_ref, wg_ref, wu_ref, wd_ref, gpost_ref, o_ref = refs[-6:]
    x = h_ref[...]
    if mix_fn is not None:
        mix = mix_fn(*refs[1:1 + n_tok + n_const])
        x = x + _rms(mix, refs[1 + n_tok + n_const][...])
    xn = _rms(x, gpre_ref[...]).astype(BF16)
    gate = _dot(xn, wg_ref[...])
    up = _dot(xn, wu_ref[...])
    hdn = (gate * jax.nn.sigmoid(gate) * up).astype(BF16)
    o_ref[...] = x + FFN_RES_SCALE * _rms(_dot(hdn, wd_ref[...]), gpost_ref[...])


def _ffn(h, gpre, wg, wu, wd, gpost, layer, j, mix=None, *, tm=512):
    m = h.shape[0]
    mix_fn, toks, consts, gmix = mix if mix is not None else (None, (), (), None)
    tok = lambda w: pl.BlockSpec((tm, w), lambda i: (i, 0))
    resident = lambda shape: pl.BlockSpec((None, None) + shape, lambda i: (layer, j, 0, 0),
                                          pipeline_mode=pl.Buffered(1))
    mix_args = [*toks, *consts] + ([gmix] if mix_fn is not None else [])
    mix_specs = [tok(t.shape[1]) for t in toks] + [_full(c.shape) for c in consts] + [
        _full((1, D_MODEL))] * (mix_fn is not None)
    return pl.pallas_call(
        functools.partial(_ffn_kernel, mix_fn=mix_fn, n_tok=len(toks), n_const=len(consts)),
        grid=(m // tm,),
        in_specs=[tok(D_MODEL), *mix_specs, _full((1, D_MODEL)),
                  resident((D_MODEL, D_FF)), resident((D_MODEL, D_FF)), resident((D_FF, D_MODEL)),
                  _full((1, D_MODEL))],
        out_specs=tok(D_MODEL),
        out_shape=jax.ShapeDtypeStruct((m, D_MODEL), F32),
        compiler_params=pltpu.CompilerParams(dimension_semantics=("parallel",), vmem_limit_bytes=FFN_VMEM_LIMIT),
        name="ffn",
    )(h, *mix_args, gpre, wg, wu, wd, gpost)


EV_Z = MLA_Q_RANK + MLA_KV_RANK + 2 * HEAD_PAD + 3 * NAT_W
HP_ALL = MLA_HEADS * HEAD_PAD


def _ev_in_kernel(h_ref, g_ref, win_ref, qn_ref, kvn_ref, wuq_ref, wuqr_ref, wuk_ref, wuv_ref,
                  tab_ref, tabt_ref, vone_ref, qt_ref, k_ref, vt_ref, nq_ref, nk_ref, nv_ref):
    m = _rms(h_ref[...], g_ref[...]).astype(BF16)
    z = _dot(m, win_ref[...])
    c0 = MLA_Q_RANK
    c1 = c0 + MLA_KV_RANK
    c2 = c1 + HEAD_PAD
    c3 = c2 + HEAD_PAD
    q_lat = z[:, :c0]
    kv_lat = z[:, c0:c1]
    kr = z[:, c1:c2]
    kr_rot = z[:, c2:c3]
    nq_ref[...] = (z[:, c3:c3 + NAT_W] * (NAT_HEAD_DIM ** -0.5)).astype(BF16)
    nk_ref[...] = z[:, c3 + NAT_W:c3 + 2 * NAT_W].astype(BF16)
    nv_ref[...] = z[:, c3 + 2 * NAT_W:c3 + 3 * NAT_W].astype(BF16)

    qn = _rms(q_lat, qn_ref[...]).astype(BF16)
    kvn = _rms(kv_lat, kvn_ref[...]).astype(BF16)
    q_raw_t = _dot_nt(wuq_ref[...], qn)
    q_rot_t = _dot_nt(wuqr_ref[...], qn)
    v_t = _dot_nt(wuv_ref[...], kvn) + vone_ref[...]
    k_nope = _dot(kvn, wuk_ref[...])

    tab = tab_ref[...]
    k_rope = kr * tab[:, :HEAD_PAD] + kr_rot * tab[:, HEAD_PAD:]
    cq_t = tabt_ref[:HEAD_PAD, :]
    sq_t = tabt_ref[HEAD_PAD:, :]
    for hd in range(MLA_HEADS):
        sl = slice(hd * HEAD_PAD, (hd + 1) * HEAD_PAD)
        qt_ref[hd] = (q_raw_t[sl] * cq_t + q_rot_t[sl] * sq_t).astype(BF16)
        k_ref[:, sl] = (k_nope[:, sl] + k_rope).astype(BF16)
        vt_ref[hd] = v_t[hd * V_ROWS:(hd + 1) * V_ROWS].astype(BF16)


def _ev_in(h, g, p, seq_len, *, tm=512):
    m = h.shape[0]
    nblk = seq_len // tm
    tok = lambda w: pl.BlockSpec((tm, w), lambda i: (i, 0))
    feat = lambda r: pl.BlockSpec((MLA_HEADS, r, tm), lambda i: (0, 0, i))
    outs = [jax.ShapeDtypeStruct((MLA_HEADS, HEAD_PAD, m), BF16), jax.ShapeDtypeStruct((m, HP_ALL), BF16),
            jax.ShapeDtypeStruct((MLA_HEADS, V_ROWS, m), BF16)] + [jax.ShapeDtypeStruct((m, NAT_W), BF16)] * 3
    tab, tab_t = p["rope_tab"][seq_len]
    return pl.pallas_call(
        _ev_in_kernel,
        grid=(m // tm,),
        in_specs=[
            tok(D_MODEL), _full((1, D_MODEL)), _full((D_MODEL, EV_Z)),
            _full((1, MLA_Q_RANK)), _full((1, MLA_KV_RANK)),
            _full((HP_ALL, MLA_Q_RANK)), _full((HP_ALL, MLA_Q_RANK)),
            _full((MLA_KV_RANK, HP_ALL)), _full((MLA_HEADS * V_ROWS, MLA_KV_RANK)),
            pl.BlockSpec((tm, 2 * HEAD_PAD), lambda i: (i % nblk, 0)),
            pl.BlockSpec((2 * HEAD_PAD, tm), lambda i: (0, i % nblk)),
            _full((MLA_HEADS * V_ROWS, 1)),
        ],
        out_specs=[feat(HEAD_PAD), tok(HP_ALL), feat(V_ROWS)] + [tok(NAT_W)] * 3,
        out_shape=outs,
        compiler_params=_cparams("parallel"),
        name="ev_in",
    )(h, g, p["w_in"], p["q_norm"], p["kv_norm"], p["w_uq_t"], p["w_uq_rot_t"], p["w_uk"], p["w_uv_t"],
      tab, tab_t, p["v_one"])


def _mla_kernel(qt_ref, k_ref, vt_ref, o_ref, s_scr, *, tk, nk, unroll):
    tq = qt_ref.shape[2]
    outs = []
    for hh in range(MLA_HPB):
        qt = qt_ref[hh]
        head = slice(hh * HEAD_PAD, (hh + 1) * HEAD_PAD)

        def scores(j):
            off = pl.multiple_of(j * tk, tk)
            return _dot(k_ref[pl.ds(off, tk), head], qt)

        s_scr[0] = scores(0)

        def body(jj, carry):
            m_prev, acc = carry
            for i in range(unroll):
                j = jj * unroll + i
                s_scr[(i + 1) % 2] = scores(jnp.minimum(j + 1, nk - 1))
                st = s_scr[i % 2]
                m_new = jnp.maximum(m_prev, jnp.max(st, axis=0, keepdims=True))
                alpha = jnp.exp2(m_prev - m_new)
                pt = jnp.exp2(st - m_new).astype(BF16)
                off = pl.multiple_of(j * tk, tk)
                acc = alpha * acc + _dot(vt_ref[hh, :, pl.ds(off, tk)], pt)
                m_prev = m_new
            return m_prev, acc

        m_init = jnp.full((1, tq), jnp.finfo(F32).min, F32)
        acc = jnp.zeros((V_ROWS, tq), F32)
        _, acc = lax.fori_loop(0, nk // unroll, body, (m_init, acc))
        outs.append(acc[:MLA_V] / acc[MLA_V:MLA_V + 1])
    o_ref[...] = jnp.concatenate(outs, axis=0).T.astype(BF16)


def _mla(qt, k, vt, batch, seq_len, *, tq=512, tk=512, unroll=16):
    m = k.shape[0]
    tk = min(tk, seq_len // 2)
    nq = seq_len // tq
    nk = seq_len // tk
    unroll = min(unroll, max(2, nk // 2))
    assert unroll % 2 == 0 and nk % unroll == 0
    return pl.pallas_call(
        functools.partial(_mla_kernel, tk=tk, nk=nk, unroll=unroll),
        grid=(batch, MLA_HEADS // MLA_HPB, nq),
        in_specs=[
            pl.BlockSpec((MLA_HPB, HEAD_PAD, tq), lambda b, h, i: (h, 0, b * nq + i)),
            pl.BlockSpec((seq_len, MLA_HPB * HEAD_PAD), lambda b, h, i: (b, h)),
            pl.BlockSpec((MLA_HPB, V_ROWS, seq_len), lambda b, h, i: (h, 0, b)),
        ],
        out_specs=pl.BlockSpec((tq, MLA_HPB * MLA_V), lambda b, h, i: (b * nq + i, h)),
        out_shape=jax.ShapeDtypeStruct((m, MLA_HEADS * MLA_V), BF16),
        scratch_shapes=[pltpu.VMEM((2, tk, tq), F32)],
        compiler_params=_cparams("parallel", "parallel", "arbitrary"),
        name="mla",
    )(qt, k, vt)


NAT_LANES = 2 * NAT_HEAD_DIM


def _nat_kernel(q_ref, k_ref, v_ref, bias_ref, o_ref, *, rows, kh, rblk):
    i = pl.program_id(2)
    lane = lax.broadcasted_iota(jnp.int32, (GRID_W, NAT_LANES), 1)
    head0 = lane < NAT_HEAD_DIM

    def row_body(rr, carry):
        r = i * rblk + rr
        start = jnp.clip(r - kh // 2, 0, rows - kh)
        delta = r - start
        koff = pl.multiple_of(start * GRID_W, GRID_W)
        qoff = pl.multiple_of(rr * GRID_W, GRID_W)
        qrow = q_ref[pl.ds(qoff, GRID_W), :]
        kwin = k_ref[pl.ds(koff, kh * GRID_W), :]
        vwin = v_ref[pl.ds(koff, kh * GRID_W), :]
        zero = jnp.zeros_like(qrow)
        q2 = jnp.concatenate([jnp.where(head0, qrow, zero), jnp.where(head0, zero, qrow)], axis=0)
        st = _dot_nt(kwin, q2) + bias_ref[delta, 0]
        e = jnp.exp(st - jnp.max(st, axis=0, keepdims=True))
        pt = (e / jnp.sum(e, axis=0, keepdims=True)).astype(BF16)
        o2 = lax.dot_general(pt, vwin, (((0,), (0,)), ((), ())), preferred_element_type=F32)
        o_ref[pl.ds(qoff, GRID_W), :] = jnp.where(head0, o2[:GRID_W], o2[GRID_W:]).astype(BF16)
        return carry

    lax.fori_loop(0, rblk, row_body, 0, unroll=True)


def _nat(nq, nk, nv, bias, batch, seq_len, *, rblk=32):
    m = nq.shape[0]
    rows = seq_len // GRID_W
    kh = min(NAT_KH_MAX, rows)
    rblk = min(rblk, rows)
    nblk = rows // rblk
    tq = rblk * GRID_W
    return pl.pallas_call(
        functools.partial(_nat_kernel, rows=rows, kh=kh, rblk=rblk),
        grid=(batch, NAT_HEADS // 2, nblk),
        in_specs=[
            pl.BlockSpec((tq, NAT_LANES), lambda b, hp, i: (b * nblk + i, hp)),
            pl.BlockSpec((seq_len, NAT_LANES), lambda b, hp, i: (b, hp)),
            pl.BlockSpec((seq_len, NAT_LANES), lambda b, hp, i: (b, hp)),
            pl.BlockSpec((kh, 1, kh * GRID_W, 2 * GRID_W), lambda b, hp, i: (0, hp, 0, 0)),
        ],
        out_specs=pl.BlockSpec((tq, NAT_LANES), lambda b, hp, i: (b * nblk + i, hp)),
        out_shape=jax.ShapeDtypeStruct((m, NAT_W), BF16),
        compiler_params=_cparams("parallel", "parallel", "arbitrary"),
        name="nat",
    )(nq, nk, nv, bias)


def _ev_mix(a_ref, b_ref, wa_ref, wb_ref):
    return _dot(a_ref[...], wa_ref[...]) + _dot(b_ref[...], wb_ref[...])


def _od_in_kernel(h_ref, g_ref, win_ref, u_ref, su_ref):
    m = _rms(h_ref[...], g_ref[...]).astype(BF16)
    z = _dot(m, win_ref[...])
    ca = z[:, :CONV_CH]
    cg = z[:, CONV_CH:2 * CONV_CH]
    u_ref[...] = ca * jax.nn.sigmoid(cg)
    su_ref[...] = z[:, 2 * CONV_CH:]


def _od_in(h, g, w_in, *, tm=512):
    m = h.shape[0]
    tok = lambda w: pl.BlockSpec((tm, w), lambda i: (i, 0))
    return pl.pallas_call(
        _od_in_kernel,
        grid=(m // tm,),
        in_specs=[tok(D_MODEL), _full((1, D_MODEL)), _full(w_in.shape)],
        out_specs=[tok(CONV_CH), tok(S5_CH)],
        out_shape=[jax.ShapeDtypeStruct((m, CONV_CH), F32), jax.ShapeDtypeStruct((m, S5_CH), F32)],
        compiler_params=_cparams("parallel"),
        name="od_in",
    )(h, g, w_in)


CONV_SUB = 64


def _conv_kernel(prev_ref, cur_ref, next_ref, w_ref, b_ref, lg_ref, lb_ref, o_ref, scr, *, tm, nblk):
    i = pl.program_id(0)
    first = (i % nblk) == 0
    last = (i % nblk) == nblk - 1
    scr[0:CONV_HALO, :] = jnp.where(first, 0.0, prev_ref[...])
    scr[CONV_HALO:CONV_HALO + tm, :] = cur_ref[...]
    scr[CONV_HALO + tm:, :] = jnp.where(last, 0.0, next_ref[...])
    w = w_ref[...]
    shift = CONV_HALO - CONV_WIDTH // 2

    def sub(c, carry):
        base = pl.multiple_of(c * CONV_SUB, CONV_SUB)
        cols = []
        for lb in range(CONV_CH // LANES):
            ls = slice(lb * LANES, (lb + 1) * LANES)
            win = scr[pl.ds(base, CONV_SUB + 2 * CONV_HALO), ls]
            acc = jnp.zeros((CONV_SUB, LANES), F32)
            nwin = CONV_SUB + 2 * CONV_HALO
            for b in range(SUBLANES):
                wb = pltpu.roll(win, nwin - b, 0) if b else win
                for a in range(2 * CONV_HALO // SUBLANES):
                    kk = SUBLANES * a + b - shift
                    if 0 <= kk < CONV_WIDTH:
                        acc = acc + wb[SUBLANES * a:SUBLANES * a + CONV_SUB, :] * w[kk:kk + 1, ls]
            cols.append(acc)
        y = jnp.concatenate(cols, axis=1) + b_ref[...]
        mu = jnp.mean(y, axis=-1, keepdims=True)
        yc = y - mu
        yn = yc * lax.rsqrt(jnp.mean(yc * yc, axis=-1, keepdims=True) + NORM_EPS)
        yn = yn * lg_ref[...] + lb_ref[...]
        o_ref[pl.ds(base, CONV_SUB), :] = (yn * jax.nn.sigmoid(yn)).astype(BF16)
        return carry

    lax.fori_loop(0, tm // CONV_SUB, sub, 0)


def _conv(u, w, b, lg, lb, seq_len, *, tm=512):
    m = u.shape[0]
    nblk = seq_len // tm
    hb = tm // CONV_HALO
    nh = m // CONV_HALO
    return pl.pallas_call(
        functools.partial(_conv_kernel, tm=tm, nblk=nblk),
        grid=(m // tm,),
        in_specs=[
            pl.BlockSpec((CONV_HALO, CONV_CH), lambda i: (jnp.maximum(i * hb - 1, 0), 0)),
            pl.BlockSpec((tm, CONV_CH), lambda i: (i, 0)),
            pl.BlockSpec((CONV_HALO, CONV_CH), lambda i: (jnp.minimum((i + 1) * hb, nh - 1), 0)),
            _full((CONV_WIDTH, CONV_CH)), _full((1, CONV_CH)), _full((1, CONV_CH)), _full((1, CONV_CH)),
        ],
        out_specs=pl.BlockSpec((tm, CONV_CH), lambda i: (i, 0)),
        out_shape=jax.ShapeDtypeStruct((m, CONV_CH), BF16),
        scratch_shapes=[pltpu.VMEM((tm + 2 * CONV_HALO, CONV_CH), F32)],
        compiler_params=_cparams("parallel"),
        name="conv",
    )(u, u, u, w, b, lg, lb)


S5_W = 2 * S5_GROUP * S5_CHUNK
S5_X = 2 * S5_STATE
S5_PIECE = 2 * S5_GROUP
S5_NPP = LANES // S5_PIECE
S5_VMEM_LIMIT = 56 * 1024 * 1024


def _s5_kernel(su_ref, m_ref, b_ref, c_ref, pw_ref, y_ref, u_scr, y_scr, *, nchunk):
    nstep = nchunk.bit_length() - 1
    slot = lax.broadcasted_iota(jnp.int32, (nchunk, LANES), 1) // S5_PIECE
    row = lax.broadcasted_iota(jnp.int32, (nchunk, S5_X), 0)

    def token_rows(ref, sigma):
        return ref.at[pl.ds(sigma, nchunk, stride=S5_CHUNK), :]

    def move(x, src, dst):
        return x if src == dst else pltpu.roll(x, ((dst - src) % S5_NPP) * S5_PIECE, 1)

    for col in range(S5_W // LANES):
        acc = [None] * S5_NPP
        for s in range(S5_NPP):
            x = token_rows(su_ref, S5_NPP * col + s)[...]
            for pp in range(S5_NPP):
                r = move(x, pp, s)
                acc[pp] = r if s == 0 else jnp.where(slot == s, r, acc[pp])
        for pp in range(S5_NPP):
            u_scr[pp, :, col * LANES:(col + 1) * LANES] = acc[pp].astype(BF16)

    for pp in range(S5_NPP):
        u = u_scr[pp]
        contrib = _dot(u, b_ref[pp])
        states = []
        for d in range(2):
            xr = contrib[:, (2 * d) * S5_X:(2 * d + 1) * S5_X]
            xi = contrib[:, (2 * d + 1) * S5_X:(2 * d + 2) * S5_X]

            def shifted(x, sh):
                if d == 0:
                    return jnp.where(row >= sh, pltpu.roll(x, sh, 0), 0.0)
                return jnp.where(row < nchunk - sh, pltpu.roll(x, nchunk - sh, 0), 0.0)

            for k in range(nstep):
                ar = pw_ref[pp, d, k, 0:1, :]
                ai = pw_ref[pp, d, k, 1:2, :]
                sr = shifted(xr, 1 << k)
                si = shifted(xi, 1 << k)
                xr, xi = xr + ar * sr - ai * si, xi + ar * si + ai * sr
            states += [shifted(xr, 1), shifted(xi, 1)]
        x = jnp.concatenate(states, axis=1).astype(BF16)
        y_scr[pp] = _dot(u, m_ref[pp]) + _dot(x, c_ref[pp])

    for col in range(S5_W // LANES):
        ys = [y_scr[pp, :, col * LANES:(col + 1) * LANES] for pp in range(S5_NPP)]
        for s in range(S5_NPP):
            out = move(ys[0], s, 0)
            for pp in range(1, S5_NPP):
                out = jnp.where(slot == pp, move(ys[pp], s, pp), out)
            token_rows(y_ref, S5_NPP * col + s)[...] = out


def _s5(su, ops, batch, seq_len):
    m = su.shape[0]
    nchunk = seq_len // S5_CHUNK
    assert nchunk & (nchunk - 1) == 0
    nstep = max(nchunk.bit_length() - 1, 1)
    seq_blk = pl.BlockSpec((seq_len, LANES), lambda q, b: (b, q), pipeline_mode=pl.Buffered(1))
    per_q = lambda *shape: pl.BlockSpec((S5_NPP,) + shape, lambda q, b: (q,) + (0,) * len(shape))
    return pl.pallas_call(
        functools.partial(_s5_kernel, nchunk=nchunk),
        grid=(S5_CH // LANES, batch),
        in_specs=[seq_blk, per_q(S5_W, S5_W), per_q(S5_W, 4 * S5_X), per_q(4 * S5_X, S5_W),
                  per_q(2, nstep, 2, S5_X)],
        out_specs=seq_blk,
        out_shape=jax.ShapeDtypeStruct((m, S5_CH), F32),
        scratch_shapes=[pltpu.VMEM((S5_NPP, nchunk, S5_W), BF16), pltpu.VMEM((S5_NPP, nchunk, S5_W), F32)],
        compiler_params=pltpu.CompilerParams(dimension_semantics=("parallel", "arbitrary"),
                                             vmem_limit_bytes=S5_VMEM_LIMIT),
        name="s5",
    )(su, ops["m"], ops["b"], ops["c"], ops["pw"][nchunk])


def _od_mix(c_ref, su_ref, ys_ref, d_ref, wglu_ref, wc_ref, ws_ref):
    y = d_ref[...] * su_ref[...] + ys_ref[...]
    z = jax.nn.gelu(y, approximate=True)
    sg = (z * jax.nn.sigmoid(_dot(z.astype(BF16), wglu_ref[...]))).astype(BF16)
    return _dot(c_ref[...], wc_ref[...]) + _dot(sg, ws_ref[...])


def _pad_heads(w, lo, hi, width):
    k = w.shape[0]
    w = w.reshape(k, MLA_HEADS, -1)[:, :, lo:hi]
    return jnp.pad(w, ((0, 0), (0, 0), (0, width - (hi - lo)))).reshape(k, MLA_HEADS * width)


def _rot_cols(w):
    half = w.shape[-1] // 2
    return jnp.concatenate([-w[..., half:], w[..., :half]], axis=-1)


def _rope_tables(seq_len):
    half = MLA_ROPE // 2
    inv = ROPE_THETA ** (-jnp.arange(half, dtype=F32) / half)
    ang = jnp.arange(seq_len, dtype=F32)[:, None] * inv[None, :]
    cos = jnp.concatenate([jnp.cos(ang)] * 2, axis=1)
    sin = jnp.concatenate([jnp.sin(ang)] * 2, axis=1)
    scale = (MLA_NOPE + MLA_ROPE) ** -0.5 * math.log2(math.e)
    z64 = jnp.zeros((seq_len, MLA_NOPE), F32)
    z32 = jnp.zeros((seq_len, HEAD_PAD - MLA_NOPE - MLA_ROPE), F32)
    cq = jnp.concatenate([z64 + scale, cos * scale, z32], axis=1)
    sq = jnp.concatenate([z64, sin * scale, z32], axis=1)
    ck = jnp.concatenate([z64, cos, z32], axis=1)
    sk = jnp.concatenate([z64, sin, z32], axis=1)
    return jnp.concatenate([ck, sk], axis=1), jnp.concatenate([cq, sq], axis=1).T


def _nat_bias(rpb, kh):
    c = np.arange(GRID_W)
    col_start = np.clip(c - NAT_KW // 2, 0, GRID_W - NAT_KW)
    col_ok = (c[None, :] >= col_start[:, None]) & (c[None, :] < col_start[:, None] + NAT_KW)
    col_off = np.clip(c[None, :] - c[:, None], -(NAT_KW - 1), NAT_KW - 1) + (NAT_KW - 1)
    delta = np.arange(kh)
    row_off = np.arange(kh)[None, :] - delta[:, None] + (NAT_KH_MAX - 1)
    row_sel = (row_off[:, :, None] == np.arange(2 * NAT_KH_MAX - 1)).astype(np.float32)
    col_sel = (col_off[:, :, None] == np.arange(2 * NAT_KW - 1)).astype(np.float32)
    bias = jnp.einsum("hrc,djr,qkc->dhqjk", rpb.astype(F32), row_sel, col_sel, precision=lax.Precision.HIGHEST)
    bias = jnp.where(col_ok[None, None, :, None, :], bias, NEG_INF)
    bias = bias.reshape(kh, NAT_HEADS // 2, 2 * GRID_W, kh * GRID_W)
    return bias.transpose(0, 1, 3, 2)


def _pair_diag(x, spec, rows, cols):
    x = x.reshape((S5_PAIRS, 2) + x.shape[1:])
    return jnp.einsum(spec, x, jnp.eye(2, dtype=x.dtype)).reshape(S5_PAIRS, rows, cols)


def _s5_operators(lam_re, lam_im, log_step, b_re, b_im, c_re, c_im, nchunks):
    t = S5_CHUNK
    w = S5_GROUP * t
    dt = jnp.exp(log_step)[:, :, None]
    ar, ai = lam_re * dt, lam_im * dt
    er = jnp.exp(ar)
    lbr, lbi = er * jnp.cos(ai), er * jnp.sin(ai)
    den = lam_re * lam_re + lam_im * lam_im
    fr = ((lbr - 1.0) * lam_re + lbi * lam_im) / den
    fi = (lbi * lam_re - (lbr - 1.0) * lam_im) / den
    bbr = fr[..., None] * b_re - fi[..., None] * b_im
    bbi = fr[..., None] * b_im + fi[..., None] * b_re

    def power(d):
        d = d.astype(F32)[None, None, :, None]
        mag = jnp.exp(ar[:, :, None, :] * d)
        return mag * jnp.cos(ai[:, :, None, :] * d), mag * jnp.sin(ai[:, :, None, :] * d)

    hi = lax.Precision.HIGHEST
    pr, pi = power(jnp.arange(t + 1))
    wr = c_re[:, :, None] * pr[:, :, :, None, :] - c_im[:, :, None] * pi[:, :, :, None, :]
    wi = c_re[:, :, None] * pi[:, :, :, None, :] + c_im[:, :, None] * pr[:, :, :, None, :]
    kmat = (jnp.einsum("xgdcp,xgpk->xgdck", wr[:, :, :t], bbr, precision=hi)
            - jnp.einsum("xgdcp,xgpk->xgdck", wi[:, :, :t], bbi, precision=hi))
    kf, kr = kmat[0], kmat[1]
    kcat = jnp.concatenate([kr[:, :0:-1], kf[:, :1] + kr[:, :1], kf[:, 1:]], axis=1)
    piece = 2 * S5_GROUP
    kc = _pair_diag(kcat, "padck,ab->pakdbc", piece, (2 * t - 1) * piece)
    mmat = jnp.concatenate([kc[:, :, (t - 1 - s) * piece:(2 * t - 1 - s) * piece] for s in range(t)], axis=1)

    def contrib(pr_, pi_, br_, bi_):
        brt, bit = br_.transpose(0, 2, 1)[:, None], bi_.transpose(0, 2, 1)[:, None]
        re = pr_[:, :, None, :] * brt - pi_[:, :, None, :] * bit
        im = pr_[:, :, None, :] * bit + pi_[:, :, None, :] * brt
        return [_pair_diag(x, "paxys,ab->pxaybs", 2 * w, S5_X) for x in (re, im)]

    bmat = jnp.concatenate(
        contrib(pr[0, :, t - 1::-1], pi[0, :, t - 1::-1], bbr[0], bbi[0])
        + contrib(pr[1, :, :t], pi[1, :, :t], bbr[1], bbi[1]), axis=2)

    def readout(w_):
        return _pair_diag(w_, "patcs,ab->pastbc", S5_X, 2 * w)

    cmat = jnp.concatenate([
        readout(wr[0, :, 1:t + 1]), readout(-wi[0, :, 1:t + 1]),
        readout(wr[1, :, t:0:-1]), readout(-wi[1, :, t:0:-1]),
    ], axis=1)

    pws = {}
    for nchunk in nchunks:
        nstep = max(nchunk.bit_length() - 1, 1)
        qr, qi = power(t * (2 ** jnp.arange(nstep)))

        def lanes(q):
            q = q.reshape(2, S5_PAIRS, 2, nstep, S5_STATE).transpose(1, 0, 3, 2, 4)
            return q.reshape(S5_PAIRS, 2, nstep, S5_X)

        pws[nchunk] = jnp.stack([lanes(qr), lanes(qi)], axis=3)
    return {"m": mmat.astype(BF16), "b": bmat.astype(BF16), "c": cmat.astype(BF16), "pw": pws}


def _even_params(ev_w_in, q_norm, kv_norm, w_uq, w_ukv, rpb, ev_w_out, seq_lens):
    c0 = MLA_Q_RANK
    c1 = c0 + MLA_KV_RANK
    c2 = c1 + MLA_ROPE
    w_kr = ev_w_in[:, c1:c2]
    pad_l = jnp.zeros((D_MODEL, MLA_NOPE), F32)
    pad_r = jnp.zeros((D_MODEL, HEAD_PAD - MLA_NOPE - MLA_ROPE), F32)
    w_in = jnp.concatenate([
        ev_w_in[:, :c1],
        pad_l, w_kr, pad_r,
        pad_l, _rot_cols(w_kr), pad_r,
        ev_w_in[:, c2:],
    ], axis=1).astype(BF16)
    uq = w_uq.reshape(MLA_Q_RANK, MLA_HEADS, MLA_NOPE + MLA_ROPE)
    uq_rot = jnp.concatenate([jnp.zeros_like(uq[..., :MLA_NOPE]), _rot_cols(uq[..., MLA_NOPE:])], axis=-1)
    qk_dim = MLA_NOPE + MLA_ROPE
    v_one = np.zeros((MLA_HEADS, V_ROWS), np.float32)
    v_one[:, MLA_V] = 1.0
    nat_bias = {kh: _nat_bias(rpb, kh) for kh in {min(NAT_KH_MAX, sl // GRID_W) for sl in seq_lens}}
    return {
        "w_in": w_in,
        "q_norm": q_norm[None], "kv_norm": kv_norm[None],
        "w_uq_t": _pad_heads(w_uq, 0, qk_dim, HEAD_PAD).T.astype(BF16),
        "w_uq_rot_t": _pad_heads(uq_rot.reshape(MLA_Q_RANK, -1), 0, qk_dim, HEAD_PAD).T.astype(BF16),
        "w_uk": _pad_heads(w_ukv, 0, MLA_NOPE, HEAD_PAD).astype(BF16),
        "w_uv_t": _pad_heads(w_ukv, MLA_NOPE, MLA_NOPE + MLA_V, V_ROWS).T.astype(BF16),
        "v_one": jnp.asarray(v_one.reshape(MLA_HEADS * V_ROWS, 1)),
        "rope_tab": {sl: _rope_tables(sl) for sl in seq_lens},
        "nat_bias": {sl: nat_bias[min(NAT_KH_MAX, sl // GRID_W)] for sl in seq_lens},
        "wa": ev_w_out[:MLA_HEADS * MLA_V].astype(BF16),
        "wb": ev_w_out[MLA_HEADS * MLA_V:].astype(BF16),
    }


def _even_mixer(h, g_pre, p, batch, seq_len):
    qt, k, vt, nq, nk, nv = _ev_in(h, g_pre, p, seq_len)
    a = _mla(qt, k, vt, batch, seq_len)
    b = _nat(nq, nk, nv, p["nat_bias"][seq_len], batch, seq_len)
    return _ev_mix, (a, b), (p["wa"], p["wb"])


def _odd_mixer(h, g_pre, p, batch, seq_len):
    u, su = _od_in(h, g_pre, p["w_in"])
    c = _conv(u, p["dw_w"], p["dw_b"], p["ln_g"], p["ln_b"], seq_len)
    ys = _s5(su, p["s5"], batch, seq_len)
    return _od_mix, (c, su, ys), (p["d"], p["w_glu"], p["wc"], p["ws"])


def kernel(x_prompt, x_sample, norm_g, ffn_w_gate, ffn_w_up, ffn_w_down, ev_w_in, mla_q_norm, mla_kv_norm, mla_w_uq, mla_w_ukv, nat_rpb, ev_w_out, od_w_in, conv_dw_w, conv_dw_b, conv_ln_g, conv_ln_b, s5_lambda_re, s5_lambda_im, s5_log_step, s5_b_re, s5_b_im, s5_c_re, s5_c_im, s5_d, s5_w_glu, od_w_out):
    depth = norm_g.shape[0]
    seq_lens = sorted({x_prompt.shape[1], x_sample.shape[1]})
    ffn_w = tuple(w.astype(BF16) for w in (ffn_w_gate, ffn_w_up, ffn_w_down))
    mixers = []
    for layer in range(depth):
        i = layer // 2
        if layer % 2 == 0:
            mixers.append(_even_params(ev_w_in[i], mla_q_norm[i], mla_kv_norm[i], mla_w_uq[i], mla_w_ukv[i],
                                       nat_rpb[i], ev_w_out[i], seq_lens))
        else:
            mixers.append({
                "w_in": od_w_in[i].astype(BF16),
                "dw_w": conv_dw_w[i], "dw_b": conv_dw_b[i][None],
                "ln_g": conv_ln_g[i][None], "ln_b": conv_ln_b[i][None],
                "s5": _s5_operators(s5_lambda_re[i], s5_lambda_im[i], s5_log_step[i], s5_b_re[i], s5_b_im[i],
                                    s5_c_re[i], s5_c_im[i], [sl // S5_CHUNK for sl in seq_lens]),
                "d": s5_d[i][None],
                "w_glu": s5_w_glu[i].astype(BF16),
                "wc": od_w_out[i][:CONV_CH].astype(BF16),
                "ws": od_w_out[i][CONV_CH:].astype(BF16),
            })

    gains = [[norm_g[layer, i][None] for i in range(norm_g.shape[1])] for layer in range(depth)]

    def trunk(x):
        batch, seq_len, _ = x.shape
        h = x.reshape(batch * seq_len, D_MODEL)
        for layer in range(depth):
            g = gains[layer]
            h = _ffn(h, g[0], *ffn_w, g[1], layer, 0)
            mixer = _even_mixer if layer % 2 == 0 else _odd_mixer
            mix = mixer(h, g[2], mixers[layer], batch, seq_len)
            h = _ffn(h, g[4], *ffn_w, g[5], layer, 1, mix + (g[3],))
        return h.reshape(batch, seq_len, D_MODEL)

    return (trunk(x_prompt), trunk(x_sample))
```

```python
import functools
import math

import jax
import jax.numpy as jnp
import numpy as np
from jax import lax
from jax.experimental import pallas as pl
from jax.experimental.pallas import tpu as pltpu

F32 = jnp.float32
BF16 = jnp.bfloat16

LANES = 128
SUBLANES = 8

D_MODEL = 1024
D_FF = 2816
GRID_W = 64

MLA_HEADS = 8
MLA_Q_RANK = 256
MLA_KV_RANK = 128
MLA_NOPE = 64
MLA_ROPE = 32
MLA_V = 64
ROPE_THETA = 10000.0
HEAD_PAD = 128
V_ROWS = 80
MLA_HPB = LANES // MLA_V

NAT_HEADS = 8
NAT_HEAD_DIM = 64
NAT_W = NAT_HEADS * NAT_HEAD_DIM
NAT_KH_MAX = 8
NAT_KW = 16

CONV_CH = 512
CONV_WIDTH = 31
CONV_HALO = 16

S5_CH = 512
S5_GROUP = 16
S5_GROUPS = S5_CH // S5_GROUP
S5_STATE = 64
S5_CHUNK = 16
S5_PAIRS = S5_GROUPS // 2

FFN_RES_SCALE = 0.5
NORM_EPS = 1e-6
NEG_INF = -1e30

VMEM_LIMIT = 48 * 1024 * 1024
FFN_VMEM_LIMIT = 56 * 1024 * 1024


def _cparams(*sem):
    return pltpu.CompilerParams(dimension_semantics=sem, vmem_limit_bytes=VMEM_LIMIT)


def _rms(x, g):
    return x * lax.rsqrt(jnp.mean(x * x, axis=-1, keepdims=True) + NORM_EPS) * g


def _dot(a, b):
    return jnp.dot(a, b, preferred_element_type=F32)


def _dot_nt(a, b):
    return lax.dot_general(a, b, (((1,), (1,)), ((), ())), preferred_element_type=F32)


def _full(shape):
    n = len(shape)
    return pl.BlockSpec(shape, lambda *_: (0,) * n)


def _ffn_kernel(*refs, mix_fn, n_tok, n_const):
    h_ref = refs[0]
    gpre_ref, wg_ref, wu_ref, wd_ref, gpost_ref, o_ref = refs[-6:]
    x = h_ref[...]
    if mix_fn is not None:
        mix = mix_fn(*refs[1:1 + n_tok + n_const])
        x = x + _rms(mix, refs[1 + n_tok + n_const][...])
    xn = _rms(x, gpre_ref[...]).astype(BF16)
    gate = _dot(xn, wg_ref[...])
    up = _dot(xn, wu_ref[...])
    hdn = (gate * jax.nn.sigmoid(gate) * up).astype(BF16)
    o_ref[...] = x + FFN_RES_SCALE * _rms(_dot(hdn, wd_ref[...]), gpost_ref[...])


def _ffn(h, gpre, wg, wu, wd, gpost, layer, j, mix=None, *, tm=512):
    m = h.shape[0]
    mix_fn, toks, consts, gmix = mix if mix is not None else (None, (), (), None)
    tok = lambda w: pl.BlockSpec((tm, w), lambda i: (i, 0))
    resident = lambda shape: pl.BlockSpec((None, None) + shape, lambda i: (layer, j, 0, 0),
                                          pipeline_mode=pl.Buffered(1))
    mix_args = [*toks, *consts] + ([gmix] if mix_fn is not None else [])
    mix_specs = [tok(t.shape[1]) for t in toks] + [_full(c.shape) for c in consts] + [
        _full((1, D_MODEL))] * (mix_fn is not None)
    return pl.pallas_call(
        functools.partial(_ffn_kernel, mix_fn=mix_fn, n_tok=len(toks), n_const=len(consts)),
        grid=(m // tm,),
        in_specs=[tok(D_MODEL), *mix_specs, _full((1, D_MODEL)),
                  resident((D_MODEL, D_FF)), resident((D_MODEL, D_FF)), resident((D_FF, D_MODEL)),
                  _full((1, D_MODEL))],
        out_specs=tok(D_MODEL),
        out_shape=jax.ShapeDtypeStruct((m, D_MODEL), F32),
        compiler_params=pltpu.CompilerParams(dimension_semantics=("parallel",), vmem_limit_bytes=FFN_VMEM_LIMIT),
        name="ffn",
    )(h, *mix_args, gpre, wg, wu, wd, gpost)


EV_Z = MLA_Q_RANK + MLA_KV_RANK + 2 * HEAD_PAD + 3 * NAT_W
HP_ALL = MLA_HEADS * HEAD_PAD


def _ev_in_kernel(h_ref, g_ref, win_ref, qn_ref, kvn_ref, wuq_ref, wuqr_ref, wuk_ref, wuv_ref,
                  tab_ref, tabt_ref, vone_ref, qt_ref, k_ref, vt_ref, nq_ref, nk_ref, nv_ref):
    m = _rms(h_ref[...], g_ref[...]).astype(BF16)
    z = _dot(m, win_ref[...])
    c0 = MLA_Q_RANK
    c1 = c0 + MLA_KV_RANK
    c2 = c1 + HEAD_PAD
    c3 = c2 + HEAD_PAD
    q_lat = z[:, :c0]
    kv_lat = z[:, c0:c1]
    kr = z[:, c1:c2]
    kr_rot = z[:, c2:c3]
    nq_ref[...] = (z[:, c3:c3 + NAT_W] * (NAT_HEAD_DIM ** -0.5)).astype(BF16)
    nk_ref[...] = z[:, c3 + NAT_W:c3 + 2 * NAT_W].astype(BF16)
    nv_ref[...] = z[:, c3 + 2 * NAT_W:c3 + 3 * NAT_W].astype(BF16)

    qn = _rms(q_lat, qn_ref[...]).astype(BF16)
    kvn = _rms(kv_lat, kvn_ref[...]).astype(BF16)
    q_raw_t = _dot_nt(wuq_ref[...], qn)
    q_rot_t = _dot_nt(wuqr_ref[...], qn)
    v_t = _dot_nt(wuv_ref[...], kvn) + vone_ref[...]
    k_nope = _dot(kvn, wuk_ref[...])

    tab = tab_ref[...]
    k_rope = kr * tab[:, :HEAD_PAD] + kr_rot * tab[:, HEAD_PAD:]
    cq_t = tabt_ref[:HEAD_PAD, :]
    sq_t = tabt_ref[HEAD_PAD:, :]
    for hd in range(MLA_HEADS):
        sl = slice(hd * HEAD_PAD, (hd + 1) * HEAD_PAD)
        qt_ref[hd] = (q_raw_t[sl] * cq_t + q_rot_t[sl] * sq_t).astype(BF16)
        k_ref[:, sl] = (k_nope[:, sl] + k_rope).astype(BF16)
        vt_ref[hd] = v_t[hd * V_ROWS:(hd + 1) * V_ROWS].astype(BF16)


def _ev_in(h, g, p, seq_len, *, tm=512):
    m = h.shape[0]
    nblk = seq_len // tm
    tok = lambda w: pl.BlockSpec((tm, w), lambda i: (i, 0))
    feat = lambda r: pl.BlockSpec((MLA_HEADS, r, tm), lambda i: (0, 0, i))
    outs = [jax.ShapeDtypeStruct((MLA_HEADS, HEAD_PAD, m), BF16), jax.ShapeDtypeStruct((m, HP_ALL), BF16),
            jax.ShapeDtypeStruct((MLA_HEADS, V_ROWS, m), BF16)] + [jax.ShapeDtypeStruct((m, NAT_W), BF16)] * 3
    tab, tab_t = p["rope_tab"][seq_len]
    return pl.pallas_call(
        _ev_in_kernel,
        grid=(m // tm,),
        in_specs=[
            tok(D_MODEL), _full((1, D_MODEL)), _full((D_MODEL, EV_Z)),
            _full((1, MLA_Q_RANK)), _full((1, MLA_KV_RANK)),
            _full((HP_ALL, MLA_Q_RANK)), _full((HP_ALL, MLA_Q_RANK)),
            _full((MLA_KV_RANK, HP_ALL)), _full((MLA_HEADS * V_ROWS, MLA_KV_RANK)),
            pl.BlockSpec((tm, 2 * HEAD_PAD), lambda i: (i % nblk, 0)),
            pl.BlockSpec((2 * HEAD_PAD, tm), lambda i: (0, i % nblk)),
            _full((MLA_HEADS * V_ROWS, 1)),
        ],
        out_specs=[feat(HEAD_PAD), tok(HP_ALL), feat(V_ROWS)] + [tok(NAT_W)] * 3,
        out_shape=outs,
        compiler_params=_cparams("parallel"),
        name="ev_in",
    )(h, g, p["w_in"], p["q_norm"], p["kv_norm"], p["w_uq_t"], p["w_uq_rot_t"], p["w_uk"], p["w_uv_t"],
      tab, tab_t, p["v_one"])


def _mla_kernel(qt_ref, k_ref, vt_ref, o_ref, s_scr, *, tk, nk, unroll):
    tq = qt_ref.shape[2]
    outs = []
    for hh in range(MLA_HPB):
        qt = qt_ref[hh]
        head = slice(hh * HEAD_PAD, (hh + 1) * HEAD_PAD)

        def scores(j):
            off = pl.multiple_of(j * tk, tk)
            return _dot(k_ref[pl.ds(off, tk), head], qt)

        s_scr[0] = scores(0)

        def body(jj, carry):
            m_prev, acc = carry
            for i in range(unroll):
                j = jj * unroll + i
                s_scr[(i + 1) % 2] = scores(jnp.minimum(j + 1, nk - 1))
                st = s_scr[i % 2]
                m_new = jnp.maximum(m_prev, jnp.max(st, axis=0, keepdims=True))
                alpha = jnp.exp2(m_prev - m_new)
                pt = jnp.exp2(st - m_new).astype(BF16)
                off = pl.multiple_of(j * tk, tk)
                acc = alpha * acc + _dot(vt_ref[hh, :, pl.ds(off, tk)], pt)
                m_prev = m_new
            return m_prev, acc

        m_init = jnp.full((1, tq), jnp.finfo(F32).min, F32)
        acc = jnp.zeros((V_ROWS, tq), F32)
        _, acc = lax.fori_loop(0, nk // unroll, body, (m_init, acc))
        outs.append(acc[:MLA_V] / acc[MLA_V:MLA_V + 1])
    o_ref[...] = jnp.concatenate(outs, axis=0).T.astype(BF16)


def _mla(qt, k, vt, batch, seq_len, *, tq=512, tk=512, unroll=16):
    m = k.shape[0]
    tk = min(tk, seq_len // 2)
    nq = seq_len // tq
    nk = seq_len // tk
    unroll = min(unroll, max(2, nk // 2))
    assert unroll % 2 == 0 and nk % unroll == 0
    return pl.pallas_call(
        functools.partial(_mla_kernel, tk=tk, nk=nk, unroll=unroll),
        grid=(batch, MLA_HEADS // MLA_HPB, nq),
        in_specs=[
            pl.BlockSpec((MLA_HPB, HEAD_PAD, tq), lambda b, h, i: (h, 0, b * nq + i)),
            pl.BlockSpec((seq_len, MLA_HPB * HEAD_PAD), lambda b, h, i: (b, h)),
            pl.BlockSpec((MLA_HPB, V_ROWS, seq_len), lambda b, h, i: (h, 0, b)),
        ],
        out_specs=pl.BlockSpec((tq, MLA_HPB * MLA_V), lambda b, h, i: (b * nq + i, h)),
        out_shape=jax.ShapeDtypeStruct((m, MLA_HEADS * MLA_V), BF16),
        scratch_shapes=[pltpu.VMEM((2, tk, tq), F32)],
        compiler_params=_cparams("parallel", "parallel", "arbitrary"),
        name="mla",
    )(qt, k, vt)


NAT_LANES = 2 * NAT_HEAD_DIM


def _nat_kernel(q_ref, k_ref, v_ref, bias_ref, o_ref, *, rows, kh, rblk):
    i = pl.program_id(2)
    lane = lax.broadcasted_iota(jnp.int32, (GRID_W, NAT_LANES), 1)
    head0 = lane < NAT_HEAD_DIM

    def row_body(rr, carry):
        r = i * rblk + rr
        start = jnp.clip(r - kh // 2, 0, rows - kh)
        delta = r - start
        koff = pl.multiple_of(start * GRID_W, GRID_W)
        qoff = pl.multiple_of(rr * GRID_W, GRID_W)
        qrow = q_ref[pl.ds(qoff, GRID_W), :]
        kwin = k_ref[pl.ds(koff, kh * GRID_W), :]
        vwin = v_ref[pl.ds(koff, kh * GRID_W), :]
        zero = jnp.zeros_like(qrow)
        q2 = jnp.concatenate([jnp.where(head0, qrow, zero), jnp.where(head0, zero, qrow)], axis=0)
        st = _dot_nt(kwin, q2) + bias_ref[delta, 0]
        e = jnp.exp(st - jnp.max(st, axis=0, keepdims=True))
        pt = (e / jnp.sum(e, axis=0, keepdims=True)).astype(BF16)
        o2 = lax.dot_general(pt, vwin, (((0,), (0,)), ((), ())), preferred_element_type=F32)
        o_ref[pl.ds(qoff, GRID_W), :] = jnp.where(head0, o2[:GRID_W], o2[GRID_W:]).astype(BF16)
        return carry

    lax.fori_loop(0, rblk, row_body, 0, unroll=True)


def _nat(nq, nk, nv, bias, batch, seq_len, *, rblk=64):
    m = nq.shape[0]
    rows = seq_len // GRID_W
    kh = min(NAT_KH_MAX, rows)
    rblk = min(rblk, rows)
    nblk = rows // rblk
    tq = rblk * GRID_W
    return pl.pallas_call(
        functools.partial(_nat_kernel, rows=rows, kh=kh, rblk=rblk),
        grid=(batch, NAT_HEADS // 2, nblk),
        in_specs=[
            pl.BlockSpec((tq, NAT_LANES), lambda b, hp, i: (b * nblk + i, hp)),
            pl.BlockSpec((seq_len, NAT_LANES), lambda b, hp, i: (b, hp)),
            pl.BlockSpec((seq_len, NAT_LANES), lambda b, hp, i: (b, hp)),
            pl.BlockSpec((kh, 1, kh * GRID_W, 2 * GRID_W), lambda b, hp, i: (0, hp, 0, 0)),
        ],
        out_specs=pl.BlockSpec((tq, NAT_LANES), lambda b, hp, i: (b * nblk + i, hp)),
        out_shape=jax.ShapeDtypeStruct((m, NAT_W), BF16),
        compiler_params=_cparams("parallel", "parallel", "arbitrary"),
        name="nat",
    )(nq, nk, nv, bias)


def _ev_mix(a_ref, b_ref, wa_ref, wb_ref):
    return _dot(a_ref[...], wa_ref[...]) + _dot(b_ref[...], wb_ref[...])


def _od_in_kernel(h_ref, g_ref, win_ref, u_ref, su_ref):
    m = _rms(h_ref[...], g_ref[...]).astype(BF16)
    z = _dot(m, win_ref[...])
    ca = z[:, :CONV_CH]
    cg = z[:, CONV_CH:2 * CONV_CH]
    u_ref[...] = ca * jax.nn.sigmoid(cg)
    su_ref[...] = z[:, 2 * CONV_CH:]


def _od_in(h, g, w_in, *, tm=512):
    m = h.shape[0]
    tok = lambda w: pl.BlockSpec((tm, w), lambda i: (i, 0))
    return pl.pallas_call(
        _od_in_kernel,
        grid=(m // tm,),
        in_specs=[tok(D_MODEL), _full((1, D_MODEL)), _full(w_in.shape)],
        out_specs=[tok(CONV_CH), tok(S5_CH)],
        out_shape=[jax.ShapeDtypeStruct((m, CONV_CH), F32), jax.ShapeDtypeStruct((m, S5_CH), F32)],
        compiler_params=_cparams("parallel"),
        name="od_in",
    )(h, g, w_in)


CONV_SUB = 64


def _conv_kernel(prev_ref, cur_ref, next_ref, w_ref, b_ref, lg_ref, lb_ref, o_ref, scr, *, tm, nblk):
    i = pl.program_id(0)
    first = (i % nblk) == 0
    last = (i % nblk) == nblk - 1
    scr[0:CONV_HALO, :] = jnp.where(first, 0.0, prev_ref[...])
    scr[CONV_HALO:CONV_HALO + tm, :] = cur_ref[...]
    scr[CONV_HALO + tm:, :] = jnp.where(last, 0.0, next_ref[...])
    w = w_ref[...]
    shift = CONV_HALO - CONV_WIDTH // 2

    def sub(c, carry):
        base = pl.multiple_of(c * CONV_SUB, CONV_SUB)
        cols = []
        for lb in range(CONV_CH // LANES):
            ls = slice(lb * LANES, (lb + 1) * LANES)
            win = scr[pl.ds(base, CONV_SUB + 2 * CONV_HALO), ls]
            acc = jnp.zeros((CONV_SUB, LANES), F32)
            nwin = CONV_SUB + 2 * CONV_HALO
            for b in range(SUBLANES):
                wb = pltpu.roll(win, nwin - b, 0) if b else win
                for a in range(2 * CONV_HALO // SUBLANES):
                    kk = SUBLANES * a + b - shift
                    if 0 <= kk < CONV_WIDTH:
                        acc = acc + wb[SUBLANES * a:SUBLANES * a + CONV_SUB, :] * w[kk:kk + 1, ls]
            cols.append(acc)
        y = jnp.concatenate(cols, axis=1) + b_ref[...]
        mu = jnp.mean(y, axis=-1, keepdims=True)
        yc = y - mu
        yn = yc * lax.rsqrt(jnp.mean(yc * yc, axis=-1, keepdims=True) + NORM_EPS)
        yn = yn * lg_ref[...] + lb_ref[...]
        o_ref[pl.ds(base, CONV_SUB), :] = (yn * jax.nn.sigmoid(yn)).astype(BF16)
        return carry

    lax.fori_loop(0, tm // CONV_SUB, sub, 0)


def _conv(u, w, b, lg, lb, seq_len, *, tm=512):
    m = u.shape[0]
    nblk = seq_len // tm
    hb = tm // CONV_HALO
    nh = m // CONV_HALO
    return pl.pallas_call(
        functools.partial(_conv_kernel, tm=tm, nblk=nblk),
        grid=(m // tm,),
        in_specs=[
            pl.BlockSpec((CONV_HALO, CONV_CH), lambda i: (jnp.maximum(i * hb - 1, 0), 0)),
            pl.BlockSpec((tm, CONV_CH), lambda i: (i, 0)),
            pl.BlockSpec((CONV_HALO, CONV_CH), lambda i: (jnp.minimum((i + 1) * hb, nh - 1), 0)),
            _full((CONV_WIDTH, CONV_CH)), _full((1, CONV_CH)), _full((1, CONV_CH)), _full((1, CONV_CH)),
        ],
        out_specs=pl.BlockSpec((tm, CONV_CH), lambda i: (i, 0)),
        out_shape=jax.ShapeDtypeStruct((m, CONV_CH), BF16),
        scratch_shapes=[pltpu.VMEM((tm + 2 * CONV_HALO, CONV_CH), F32)],
        compiler_params=_cparams("parallel"),
        name="conv",
    )(u, u, u, w, b, lg, lb)


S5_W = 2 * S5_GROUP * S5_CHUNK
S5_X = 2 * S5_STATE
S5_PIECE = 2 * S5_GROUP
S5_NPP = LANES // S5_PIECE
S5_VMEM_LIMIT = 56 * 1024 * 1024


def _s5_kernel(su_ref, m_ref, b_ref, c_ref, pw_ref, y_ref, u_scr, y_scr, *, nchunk):
    nstep = nchunk.bit_length() - 1
    slot = lax.broadcasted_iota(jnp.int32, (nchunk, LANES), 1) // S5_PIECE
    row = lax.broadcasted_iota(jnp.int32, (nchunk, S5_X), 0)

    def token_rows(ref, sigma):
        return ref.at[pl.ds(sigma, nchunk, stride=S5_CHUNK), :]

    def move(x, src, dst):
        return x if src == dst else pltpu.roll(x, ((dst - src) % S5_NPP) * S5_PIECE, 1)

    for col in range(S5_W // LANES):
        acc = [None] * S5_NPP
        for s in range(S5_NPP):
            x = token_rows(su_ref, S5_NPP * col + s)[...]
            for pp in range(S5_NPP):
                r = move(x, pp, s)
                acc[pp] = r if s == 0 else jnp.where(slot == s, r, acc[pp])
        for pp in range(S5_NPP):
            u_scr[pp, :, col * LANES:(col + 1) * LANES] = acc[pp].astype(BF16)

    for pp in range(S5_NPP):
        u = u_scr[pp]
        contrib = _dot(u, b_ref[pp])
        states = []
        for d in range(2):
            xr = contrib[:, (2 * d) * S5_X:(2 * d + 1) * S5_X]
            xi = contrib[:, (2 * d + 1) * S5_X:(2 * d + 2) * S5_X]

            def shifted(x, sh):
                if d == 0:
                    return jnp.where(row >= sh, pltpu.roll(x, sh, 0), 0.0)
                return jnp.where(row < nchunk - sh, pltpu.roll(x, nchunk - sh, 0), 0.0)

            for k in range(nstep):
                ar = pw_ref[pp, d, k, 0:1, :]
                ai = pw_ref[pp, d, k, 1:2, :]
                sr = shifted(xr, 1 << k)
                si = shifted(xi, 1 << k)
                xr, xi = xr + ar * sr - ai * si, xi + ar * si + ai * sr
            states += [shifted(xr, 1), shifted(xi, 1)]
        x = jnp.concatenate(states, axis=1).astype(BF16)
        y_scr[pp] = _dot(u, m_ref[pp]) + _dot(x, c_ref[pp])

    for col in range(S5_W // LANES):
        ys = [y_scr[pp, :, col * LANES:(col + 1) * LANES] for pp in range(S5_NPP)]
        for s in range(S5_NPP):
            out = move(ys[0], s, 0)
            for pp in range(1, S5_NPP):
                out = jnp.where(slot == pp, move(ys[pp], s, pp), out)
            token_rows(y_ref, S5_NPP * col + s)[...] = out


def _s5(su, ops, batch, seq_len):
    m = su.shape[0]
    nchunk = seq_len // S5_CHUNK
    assert nchunk & (nchunk - 1) == 0
    nstep = max(nchunk.bit_length() - 1, 1)
    seq_blk = pl.BlockSpec((seq_len, LANES), lambda q, b: (b, q), pipeline_mode=pl.Buffered(1))
    per_q = lambda *shape: pl.BlockSpec((S5_NPP,) + shape, lambda q, b: (q,) + (0,) * len(shape))
    return pl.pallas_call(
        functools.partial(_s5_kernel, nchunk=nchunk),
        grid=(S5_CH // LANES, batch),
        in_specs=[seq_blk, per_q(S5_W, S5_W), per_q(S5_W, 4 * S5_X), per_q(4 * S5_X, S5_W),
                  per_q(2, nstep, 2, S5_X)],
        out_specs=seq_blk,
        out_shape=jax.ShapeDtypeStruct((m, S5_CH), F32),
        scratch_shapes=[pltpu.VMEM((S5_NPP, nchunk, S5_W), BF16), pltpu.VMEM((S5_NPP, nchunk, S5_W), F32)],
        compiler_params=pltpu.CompilerParams(dimension_semantics=("parallel", "arbitrary"),
                                             vmem_limit_bytes=S5_VMEM_LIMIT),
        name="s5",
    )(su, ops["m"], ops["b"], ops["c"], ops["pw"][nchunk])


def _od_mix(c_ref, su_ref, ys_ref, d_ref, wglu_ref, wc_ref, ws_ref):
    y = d_ref[...] * su_ref[...] + ys_ref[...]
    z = jax.nn.gelu(y, approximate=True)
    sg = (z * jax.nn.sigmoid(_dot(z.astype(BF16), wglu_ref[...]))).astype(BF16)
    return _dot(c_ref[...], wc_ref[...]) + _dot(sg, ws_ref[...])


def _pad_heads(w, lo, hi, width):
    k = w.shape[0]
    w = w.reshape(k, MLA_HEADS, -1)[:, :, lo:hi]
    return jnp.pad(w, ((0, 0), (0, 0), (0, width - (hi - lo)))).reshape(k, MLA_HEADS * width)


def _rot_cols(w):
    half = w.shape[-1] // 2
    return jnp.concatenate([-w[..., half:], w[..., :half]], axis=-1)


def _rope_tables(seq_len):
    half = MLA_ROPE // 2
    inv = ROPE_THETA ** (-jnp.arange(half, dtype=F32) / half)
    ang = jnp.arange(seq_len, dtype=F32)[:, None] * inv[None, :]
    cos = jnp.concatenate([jnp.cos(ang)] * 2, axis=1)
    sin = jnp.concatenate([jnp.sin(ang)] * 2, axis=1)
    scale = (MLA_NOPE + MLA_ROPE) ** -0.5 * math.log2(math.e)
    z64 = jnp.zeros((seq_len, MLA_NOPE), F32)
    z32 = jnp.zeros((seq_len, HEAD_PAD - MLA_NOPE - MLA_ROPE), F32)
    cq = jnp.concatenate([z64 + scale, cos * scale, z32], axis=1)
    sq = jnp.concatenate([z64, sin * scale, z32], axis=1)
    ck = jnp.concatenate([z64, cos, z32], axis=1)
    sk = jnp.concatenate([z64, sin, z32], axis=1)
    return jnp.concatenate([ck, sk], axis=1), jnp.concatenate([cq, sq], axis=1).T


def _nat_bias(rpb, kh):
    c = np.arange(GRID_W)
    col_start = np.clip(c - NAT_KW // 2, 0, GRID_W - NAT_KW)
    col_ok = (c[None, :] >= col_start[:, None]) & (c[None, :] < col_start[:, None] + NAT_KW)
    col_off = np.clip(c[None, :] - c[:, None], -(NAT_KW - 1), NAT_KW - 1) + (NAT_KW - 1)
    delta = np.arange(kh)
    row_off = np.arange(kh)[None, :] - delta[:, None] + (NAT_KH_MAX - 1)
    row_sel = (row_off[:, :, None] == np.arange(2 * NAT_KH_MAX - 1)).astype(np.float32)
    col_sel = (col_off[:, :, None] == np.arange(2 * NAT_KW - 1)).astype(np.float32)
    bias = jnp.einsum("hrc,djr,qkc->dhqjk", rpb.astype(F32), row_sel, col_sel, precision=lax.Precision.HIGHEST)
    bias = jnp.where(col_ok[None, None, :, None, :], bias, NEG_INF)
    bias = bias.reshape(kh, NAT_HEADS // 2, 2 * GRID_W, kh * GRID_W)
    return bias.transpose(0, 1, 3, 2)


def _pair_diag(x, spec, rows, cols):
    x = x.reshape((S5_PAIRS, 2) + x.shape[1:])
    return jnp.einsum(spec, x, jnp.eye(2, dtype=x.dtype)).reshape(S5_PAIRS, rows, cols)


def _s5_operators(lam_re, lam_im, log_step, b_re, b_im, c_re, c_im, nchunks):
    t = S5_CHUNK
    w = S5_GROUP * t
    dt = jnp.exp(log_step)[:, :, None]
    ar, ai = lam_re * dt, lam_im * dt
    er = jnp.exp(ar)
    lbr, lbi = er * jnp.cos(ai), er * jnp.sin(ai)
    den = lam_re * lam_re + lam_im * lam_im
    fr = ((lbr - 1.0) * lam_re + lbi * lam_im) / den
    fi = (lbi * lam_re - (lbr - 1.0) * lam_im) / den
    bbr = fr[..., None] * b_re - fi[..., None] * b_im
    bbi = fr[..., None] * b_im + fi[..., None] * b_re

    def power(d):
        d = d.astype(F32)[None, None, :, None]
        mag = jnp.exp(ar[:, :, None, :] * d)
        return mag * jnp.cos(ai[:, :, None, :] * d), mag * jnp.sin(ai[:, :, None, :] * d)

    hi = lax.Precision.HIGHEST
    pr, pi = power(jnp.arange(t + 1))
    wr = c_re[:, :, None] * pr[:, :, :, None, :] - c_im[:, :, None] * pi[:, :, :, None, :]
    wi = c_re[:, :, None] * pi[:, :, :, None, :] + c_im[:, :, None] * pr[:, :, :, None, :]
    kmat = (jnp.einsum("xgdcp,xgpk->xgdck", wr[:, :, :t], bbr, precision=hi)
            - jnp.einsum("xgdcp,xgpk->xgdck", wi[:, :, :t], bbi, precision=hi))
    kf, kr = kmat[0], kmat[1]
    kcat = jnp.concatenate([kr[:, :0:-1], kf[:, :1] + kr[:, :1], kf[:, 1:]], axis=1)
    piece = 2 * S5_GROUP
    kc = _pair_diag(kcat, "padck,ab->pakdbc", piece, (2 * t - 1) * piece)
    mmat = jnp.concatenate([kc[:, :, (t - 1 - s) * piece:(2 * t - 1 - s) * piece] for s in range(t)], axis=1)

    def contrib(pr_, pi_, br_, bi_):
        brt, bit = br_.transpose(0, 2, 1)[:, None], bi_.transpose(0, 2, 1)[:, None]
        re = pr_[:, :, None, :] * brt - pi_[:, :, None, :] * bit
        im = pr_[:, :, None, :] * bit + pi_[:, :, None, :] * brt
        return [_pair_diag(x, "paxys,ab->pxaybs", 2 * w, S5_X) for x in (re, im)]

    bmat = jnp.concatenate(
        contrib(pr[0, :, t - 1::-1], pi[0, :, t - 1::-1], bbr[0], bbi[0])
        + contrib(pr[1, :, :t], pi[1, :, :t], bbr[1], bbi[1]), axis=2)

    def readout(w_):
        return _pair_diag(w_, "patcs,ab->pastbc", S5_X, 2 * w)

    cmat = jnp.concatenate([
        readout(wr[0, :, 1:t + 1]), readout(-wi[0, :, 1:t + 1]),
        readout(wr[1, :, t:0:-1]), readout(-wi[1, :, t:0:-1]),
    ], axis=1)

    pws = {}
    for nchunk in nchunks:
        nstep = max(nchunk.bit_length() - 1, 1)
        qr, qi = power(t * (2 ** jnp.arange(nstep)))

        def lanes(q):
            q = q.reshape(2, S5_PAIRS, 2, nstep, S5_STATE).transpose(1, 0, 3, 2, 4)
            return q.reshape(S5_PAIRS, 2, nstep, S5_X)

        pws[nchunk] = jnp.stack([lanes(qr), lanes(qi)], axis=3)
    return {"m": mmat.astype(BF16), "b": bmat.astype(BF16), "c": cmat.astype(BF16), "pw": pws}


def _even_params(ev_w_in, q_norm, kv_norm, w_uq, w_ukv, rpb, ev_w_out, seq_lens):
    c0 = MLA_Q_RANK
    c1 = c0 + MLA_KV_RANK
    c2 = c1 + MLA_ROPE
    w_kr = ev_w_in[:, c1:c2]
    pad_l = jnp.zeros((D_MODEL, MLA_NOPE), F32)
    pad_r = jnp.zeros((D_MODEL, HEAD_PAD - MLA_NOPE - MLA_ROPE), F32)
    w_in = jnp.concatenate([
        ev_w_in[:, :c1],
        pad_l, w_kr, pad_r,
        pad_l, _rot_cols(w_kr), pad_r,
        ev_w_in[:, c2:],
    ], axis=1).astype(BF16)
    uq = w_uq.reshape(MLA_Q_RANK, MLA_HEADS, MLA_NOPE + MLA_ROPE)
    uq_rot = jnp.concatenate([jnp.zeros_like(uq[..., :MLA_NOPE]), _rot_cols(uq[..., MLA_NOPE:])], axis=-1)
    qk_dim = MLA_NOPE + MLA_ROPE
    v_one = np.zeros((MLA_HEADS, V_ROWS), np.float32)
    v_one[:, MLA_V] = 1.0
    nat_bias = {kh: _nat_bias(rpb, kh) for kh in {min(NAT_KH_MAX, sl // GRID_W) for sl in seq_lens}}
    return {
        "w_in": w_in,
        "q_norm": q_norm[None], "kv_norm": kv_norm[None],
        "w_uq_t": _pad_heads(w_uq, 0, qk_dim, HEAD_PAD).T.astype(BF16),
        "w_uq_rot_t": _pad_heads(uq_rot.reshape(MLA_Q_RANK, -1), 0, qk_dim, HEAD_PAD).T.astype(BF16),
        "w_uk": _pad_heads(w_ukv, 0, MLA_NOPE, HEAD_PAD).astype(BF16),
        "w_uv_t": _pad_heads(w_ukv, MLA_NOPE, MLA_NOPE + MLA_V, V_ROWS).T.astype(BF16),
        "v_one": jnp.asarray(v_one.reshape(MLA_HEADS * V_ROWS, 1)),
        "rope_tab": {sl: _rope_tables(sl) for sl in seq_lens},
        "nat_bias": {sl: nat_bias[min(NAT_KH_MAX, sl // GRID_W)] for sl in seq_lens},
        "wa": ev_w_out[:MLA_HEADS * MLA_V].astype(BF16),
        "wb": ev_w_out[MLA_HEADS * MLA_V:].astype(BF16),
    }


def _even_mixer(h, g_pre, p, batch, seq_len):
    qt, k, vt, nq, nk, nv = _ev_in(h, g_pre, p, seq_len)
    a = _mla(qt, k, vt, batch, seq_len)
    b = _nat(nq, nk, nv, p["nat_bias"][seq_len], batch, seq_len)
    return _ev_mix, (a, b), (p["wa"], p["wb"])


def _odd_mixer(h, g_pre, p, batch, seq_len):
    u, su = _od_in(h, g_pre, p["w_in"])
    c = _conv(u, p["dw_w"], p["dw_b"], p["ln_g"], p["ln_b"], seq_len)
    ys = _s5(su, p["s5"], batch, seq_len)
    return _od_mix, (c, su, ys), (p["d"], p["w_glu"], p["wc"], p["ws"])


def kernel(x_prompt, x_sample, norm_g, ffn_w_gate, ffn_w_up, ffn_w_down, ev_w_in, mla_q_norm, mla_kv_norm, mla_w_uq, mla_w_ukv, nat_rpb, ev_w_out, od_w_in, conv_dw_w, conv_dw_b, conv_ln_g, conv_ln_b, s5_lambda_re, s5_lambda_im, s5_log_step, s5_b_re, s5_b_im, s5_c_re, s5_c_im, s5_d, s5_w_glu, od_w_out):
    depth = norm_g.shape[0]
    seq_lens = sorted({x_prompt.shape[1], x_sample.shape[1]})
    ffn_w = tuple(w.astype(BF16) for w in (ffn_w_gate, ffn_w_up, ffn_w_down))
    mixers = []
    for layer in range(depth):
        i = layer // 2
        if layer % 2 == 0:
            mixers.append(_even_params(ev_w_in[i], mla_q_norm[i], mla_kv_norm[i], mla_w_uq[i], mla_w_ukv[i],
                                       nat_rpb[i], ev_w_out[i], seq_lens))
        else:
            mixers.append({
                "w_in": od_w_in[i].astype(BF16),
                "dw_w": conv_dw_w[i], "dw_b": conv_dw_b[i][None],
                "ln_g": conv_ln_g[i][None], "ln_b": conv_ln_b[i][None],
                "s5": _s5_operators(s5_lambda_re[i], s5_lambda_im[i], s5_log_step[i], s5_b_re[i], s5_b_im[i],
                                    s5_c_re[i], s5_c_im[i], [sl // S5_CHUNK for sl in seq_lens]),
                "d": s5_d[i][None],
                "w_glu": s5_w_glu[i].astype(BF16),
                "wc": od_w_out[i][:CONV_CH].astype(BF16),
                "ws": od_w_out[i][CONV_CH:].astype(BF16),
            })

    gains = [[norm_g[layer, i][None] for i in range(norm_g.shape[1])] for layer in range(depth)]

    def trunk(x):
        batch, seq_len, _ = x.shape
        h = x.reshape(batch * seq_len, D_MODEL)
        for layer in range(depth):
            g = gains[layer]
            h = _ffn(h, g[0], *ffn_w, g[1], layer, 0)
            mixer = _even_mixer if layer % 2 == 0 else _odd_mixer
            mix = mixer(h, g[2], mixers[layer], batch, seq_len)
            h = _ffn(h, g[4], *ffn_w, g[5], layer, 1, mix + (g[3],))
        return h.reshape(batch, seq_len, D_MODEL)

    return (trunk(x_prompt), trunk(x_sample))
```
